```python
import math
import jax, jax.numpy as jnp
from jax import lax
import numpy as np

D_MODEL = 2048
BATCH = 4
SEQ = 4096
DEPTH = 2

HEAD_DIM_A = 64
N_Q_A = 32
N_KV_A = 4
GROUP_A = N_Q_A // N_KV_A
WINDOW = 128
BLOCK_A = WINDOW
HEAD_DIM_B = 128
N_HEADS_B = D_MODEL // (2 * HEAD_DIM_B)
Q_BLOCK_B = 128
SUBLN_EPS = 1e-5
ROPE_THETA = 10000.0
D_FF = ((8 * D_MODEL // 3 + 255) // 256) * 256
CONV_WIDTH = 3
LN_EPS = 1e-5
ALPHA = (2 * DEPTH) ** 0.25
BETA = (8 * DEPTH) ** -0.25
NEG_INF = -1e30

W_QA = N_Q_A * HEAD_DIM_A
W_KA = N_KV_A * HEAD_DIM_A
W_VA = N_KV_A * HEAD_DIM_A
W_QB = N_HEADS_B * 2 * HEAD_DIM_B
W_KB = N_HEADS_B * 2 * HEAD_DIM_B
W_VB = N_HEADS_B * 2 * HEAD_DIM_B
W_GATES = 2 * D_MODEL
IN_SIZES = (W_QA, W_KA, W_VA, W_QB, W_KB, W_VB, W_GATES)
IN_WIDTH = sum(IN_SIZES)
SPLIT_POINTS = tuple(int(v) for v in np.cumsum(IN_SIZES)[:-1])
OUT_A = N_Q_A * HEAD_DIM_A
OUT_B = N_HEADS_B * 2 * HEAD_DIM_B

kernel_name = "hybrid_swa_sink_diffattn_convffn_deepnorm"


def rope_cos_sin(seq, dim):
    inv = 1.0 / (ROPE_THETA ** (jnp.arange(0, dim, 2, dtype=jnp.float32) / dim))
    ang = jnp.arange(seq, dtype=jnp.float32)[:, None] * inv[None, :]
    return jnp.cos(ang), jnp.sin(ang)


def apply_rope(x, cos, sin):
    half = x.shape[-1] // 2
    x1, x2 = x[..., :half], x[..., half:]
    c = cos.astype(x.dtype)
    s = sin.astype(x.dtype)
    return jnp.concatenate([x1 * c - x2 * s, x2 * c + x1 * s], axis=-1)


def layer_norm(x, g, b):
    x32 = x.astype(jnp.float32)
    mu = jnp.mean(x32, axis=-1, keepdims=True)
    var = jnp.mean(jnp.square(x32 - mu), axis=-1, keepdims=True)
    y = (x32 - mu) * lax.rsqrt(var + LN_EPS) * g.astype(jnp.float32) + b.astype(jnp.float32)
    return y.astype(x.dtype)


def sliding_window_sink_attention(q, k, v, sinks):
    B, S = q.shape[0], q.shape[1]
    nb = S // BLOCK_A
    qb = q.reshape(B, nb, BLOCK_A, N_KV_A, GROUP_A, HEAD_DIM_A)
    kb = k.reshape(B, nb, BLOCK_A, N_KV_A, HEAD_DIM_A)
    vb = v.reshape(B, nb, BLOCK_A, N_KV_A, HEAD_DIM_A)
    pad = ((0, 0), (1, 0), (0, 0), (0, 0), (0, 0))
    k_band = jnp.concatenate([jnp.pad(kb, pad)[:, :-1], kb], axis=2)
    v_band = jnp.concatenate([jnp.pad(vb, pad)[:, :-1], vb], axis=2)
    scale = HEAD_DIM_A ** -0.5
    s = jnp.einsum('bnqhgd,bnkhd->bnhgqk', qb, k_band).astype(jnp.float32) * scale
    qi = jnp.arange(BLOCK_A)[:, None]
    kj = jnp.arange(2 * BLOCK_A)[None, :]
    in_window = (kj <= BLOCK_A + qi) & (kj > BLOCK_A + qi - WINDOW)
    key_pos = jnp.arange(nb)[:, None] * BLOCK_A + jnp.arange(2 * BLOCK_A)[None, :] - BLOCK_A
    valid = key_pos >= 0
    mask = in_window[None, :, :] & valid[:, None, :]
    s = jnp.where(mask[None, :, None, None], s, NEG_INF)
    sink = sinks.astype(jnp.float32).reshape(N_KV_A, GROUP_A)[None, None, :, :, None, None]
    m = jnp.maximum(jnp.max(s, axis=-1, keepdims=True), sink)
    p = jnp.exp(s - m)
    denom = jnp.sum(p, axis=-1, keepdims=True) + jnp.exp(sink - m)
    w = (p / denom).astype(v.dtype)
    o = jnp.einsum('bnhgqk,bnkhd->bnqhgd', w, v_band)
    return o.reshape(B, S, OUT_A)


def differential_attention(q, k, v, lam, subln_g, lam_init):
    B, S = q.shape[0], q.shape[1]
    nb = S // Q_BLOCK_B
    q_blocks = q.reshape(B, nb, Q_BLOCK_B, N_HEADS_B, 2, HEAD_DIM_B).transpose(1, 0, 2, 3, 4, 5)
    key_pos = jnp.arange(S)
    scale = HEAD_DIM_B ** -0.5

    def one_block(args):
        qb, n = args
        s = jnp.einsum('bqhcd,bkhcd->bhcqk', qb, k).astype(jnp.float32) * scale
        q_pos = n * Q_BLOCK_B + jnp.arange(Q_BLOCK_B)
        causal = key_pos[None, :] <= q_pos[:, None]
        a = jax.nn.softmax(jnp.where(causal, s, NEG_INF), axis=-1)
        w = a[:, :, 0] - lam * a[:, :, 1]
        return jnp.einsum('bhqk,bkhe->bqhe', w.astype(v.dtype), v)

    o = lax.map(one_block, (q_blocks, jnp.arange(nb)))
    o = o.transpose(1, 0, 2, 3, 4).reshape(B, S, N_HEADS_B, 2 * HEAD_DIM_B)
    o32 = o.astype(jnp.float32)
    o32 = o32 * lax.rsqrt(jnp.mean(jnp.square(o32), axis=-1, keepdims=True) + SUBLN_EPS)
    o32 = o32 * subln_g.astype(jnp.float32) * (1.0 - lam_init)
    return o32.reshape(B, S, OUT_B).astype(q.dtype)


def conv_ffn(h, w_up, conv_w, conv_b, w_down):
    S = h.shape[1]
    u = jnp.einsum('bsd,df->bsf', h, w_up)
    up = jnp.pad(u, ((0, 0), (CONV_WIDTH - 1, 0), (0, 0)))
    u = conv_b + sum(conv_w[t] * up[:, t:t + S] for t in range(CONV_WIDTH))
    gate, val = jnp.split(u, 2, axis=-1)
    return jnp.einsum('bsf,fd->bsd', jax.nn.silu(gate) * val, w_down)


def setup_inputs(seed: int = 0) -> dict:
    key = jax.random.key(seed)
    ks = jax.random.split(key, 19)
    f32 = jnp.float32
    nrm = lambda k, shape, scale: jax.random.normal(k, shape, f32) * scale
    return {
        "x": nrm(ks[0], (BATCH, SEQ, D_MODEL), 1.0),
        "w_in": nrm(ks[1], (DEPTH, D_MODEL, IN_WIDTH), D_MODEL ** -0.5),
        "sinks": nrm(ks[2], (DEPTH, N_Q_A), 1.0),
        "lambda_q1": nrm(ks[3], (DEPTH, HEAD_DIM_B), 0.1),
        "lambda_k1": nrm(ks[4], (DEPTH, HEAD_DIM_B), 0.1),
        "lambda_q2": nrm(ks[5], (DEPTH, HEAD_DIM_B), 0.1),
        "lambda_k2": nrm(ks[6], (DEPTH, HEAD_DIM_B), 0.1),
        "subln_g": 1.0 + nrm(ks[7], (DEPTH, 2 * HEAD_DIM_B), 0.02),
        "w_proj_a": nrm(ks[8], (DEPTH, OUT_A, D_MODEL), OUT_A ** -0.5),
        "w_proj_b": nrm(ks[9], (DEPTH, OUT_B, D_MODEL), OUT_B ** -0.5),
        "w_out": nrm(ks[10], (DEPTH, D_MODEL, D_MODEL), BETA * D_MODEL ** -0.5),
        "ln1_g": 1.0 + nrm(ks[11], (DEPTH, D_MODEL), 0.02),
        "ln1_b": nrm(ks[12], (DEPTH, D_MODEL), 0.02),
        "w_up": nrm(ks[13], (DEPTH, D_MODEL, 2 * D_FF), D_MODEL ** -0.5),
        "conv_w": nrm(ks[14], (DEPTH, CONV_WIDTH, 2 * D_FF), CONV_WIDTH ** -0.5),
        "conv_b": nrm(ks[15], (DEPTH, 2 * D_FF), 0.02),
        "w_down": nrm(ks[16], (DEPTH, D_FF, D_MODEL), BETA * D_FF ** -0.5),
        "ln2_g": 1.0 + nrm(ks[17], (DEPTH, D_MODEL), 0.02),
        "ln2_b": nrm(ks[18], (DEPTH, D_MODEL), 0.02),
    }


def reference(x, w_in, sinks, lambda_q1, lambda_k1, lambda_q2, lambda_k2, subln_g,
              w_proj_a, w_proj_b, w_out, ln1_g, ln1_b, w_up, conv_w, conv_b, w_down,
              ln2_g, ln2_b):
    B, S = x.shape[0], x.shape[1]
    cos_a, sin_a = rope_cos_sin(S, HEAD_DIM_A)
    cos_b, sin_b = rope_cos_sin(S, HEAD_DIM_B)
    for l in range(DEPTH):
        proj = jnp.einsum('bsd,de->bse', x, w_in[l])
        qa, ka, va, qb, kb, vb, gates = jnp.split(proj, SPLIT_POINTS, axis=-1)
        qa = apply_rope(qa.reshape(B, S, N_Q_A, HEAD_DIM_A), cos_a[:, None, :], sin_a[:, None, :])
        ka = apply_rope(ka.reshape(B, S, N_KV_A, HEAD_DIM_A), cos_a[:, None, :], sin_a[:, None, :])
        va = va.reshape(B, S, N_KV_A, HEAD_DIM_A)
        o_a = sliding_window_sink_attention(qa, ka, va, sinks[l])

        qb = apply_rope(qb.reshape(B, S, N_HEADS_B, 2, HEAD_DIM_B), cos_b[:, None, None, :], sin_b[:, None, None, :])
        kb = apply_rope(kb.reshape(B, S, N_HEADS_B, 2, HEAD_DIM_B), cos_b[:, None, None, :], sin_b[:, None, None, :])
        vb = vb.reshape(B, S, N_HEADS_B, 2 * HEAD_DIM_B)
        lam_init = 0.8 - 0.6 * math.exp(-0.3 * l)
        lam = (jnp.exp(jnp.sum(lambda_q1[l].astype(jnp.float32) * lambda_k1[l].astype(jnp.float32)))
               - jnp.exp(jnp.sum(lambda_q2[l].astype(jnp.float32) * lambda_k2[l].astype(jnp.float32)))
               + lam_init)
        o_b = differential_attention(qb, kb, vb, lam, subln_g[l], lam_init)

        g_a, g_b = jnp.split(jax.nn.sigmoid(gates), 2, axis=-1)
        merged = (g_a * jnp.einsum('bse,ed->bsd', o_a, w_proj_a[l])
                  + g_b * jnp.einsum('bse,ed->bsd', o_b, w_proj_b[l]))
        mix_out = jnp.einsum('bsd,de->bse', merged, w_out[l])
        x = layer_norm(ALPHA * x + mix_out, ln1_g[l], ln1_b[l])
        f = conv_ffn(x, w_up[l], conv_w[l], conv_b[l], w_down[l])
        x = layer_norm(ALPHA * x + f, ln2_g[l], ln2_b[l])
    return x
```

```python
import functools
import math

import jax
import jax.numpy as jnp
from jax import lax
from jax.experimental import pallas as pl
from jax.experimental.pallas import tpu as pltpu

D_MODEL = 2048
DEPTH = 2
HEAD_DIM_A = 64
N_Q_A = 32
N_KV_A = 4
GROUP_A = N_Q_A // N_KV_A
WINDOW = 128
HEAD_DIM_B = 128
N_HEADS_B = D_MODEL // (2 * HEAD_DIM_B)
SUBLN_EPS = 1e-5
ROPE_THETA = 10000.0
D_FF = ((8 * D_MODEL // 3 + 255) // 256) * 256
CONV_WIDTH = 3
LN_EPS = 1e-5
ALPHA = (2 * DEPTH) ** 0.25
NEG_INF = -1e30

W_QA = N_Q_A * HEAD_DIM_A
W_KA = N_KV_A * HEAD_DIM_A
W_A = W_QA + 2 * W_KA
W_QB = N_HEADS_B * 2 * HEAD_DIM_B
W_REST = 3 * W_QB + 2 * D_MODEL

LANES = 128
BF16_ROWS = 16
VMEM_LIMIT = 56 * 1024 * 1024

F32 = jnp.float32
BF16 = jnp.bfloat16


def _params(*sem):
    return pltpu.CompilerParams(dimension_semantics=sem, vmem_limit_bytes=VMEM_LIMIT)


def _dot(a, b):
    return jnp.dot(a, b, preferred_element_type=F32)


def _dot_nt(a, b):
    return lax.dot_general(a, b, (((1,), (1,)), ((), ())), preferred_element_type=F32)


def _layer_norm(z, g, b):
    mu = jnp.mean(z, axis=-1, keepdims=True)
    zc = z - mu
    var = jnp.mean(zc * zc, axis=-1, keepdims=True)
    return zc * lax.rsqrt(var + LN_EPS) * g + b


def _rope_tables(seq):
    def cs(dim):
        inv = 1.0 / (ROPE_THETA ** (jnp.arange(0, dim, 2, dtype=F32) / dim))
        ang = jnp.arange(seq, dtype=F32)[:, None] * inv[None, :]
        return jnp.cos(ang), jnp.sin(ang)

    cos_a, sin_a = cs(HEAD_DIM_A)
    cos_b, sin_b = cs(HEAD_DIM_B)
    zeros_a = jnp.zeros_like(sin_a)
    tab_a = jnp.stack([
        jnp.tile(cos_a, (1, 4)),
        jnp.tile(jnp.concatenate([zeros_a, sin_a], axis=1), (1, 2)),
        jnp.tile(jnp.concatenate([-sin_a, zeros_a], axis=1), (1, 2)),
    ])
    tab_b = jnp.stack([
        jnp.tile(cos_b, (1, 2)),
        jnp.concatenate([-sin_b, sin_b], axis=1),
    ])
    return tab_a, tab_b


def _proj_a_kernel(x_ref, w_ref, tab_ref, o_ref):
    acc = _dot(x_ref[...], w_ref[...])
    cos, hi, lo = tab_ref[0], tab_ref[1], tab_ref[2]
    scale = HEAD_DIM_A ** -0.5
    n_rope = (W_QA + W_KA) // LANES
    for g in range(W_A // LANES):
        xg = acc[:, g * LANES:(g + 1) * LANES]
        if g < n_rope:
            xg = xg * cos + pltpu.roll(xg, 32, axis=1) * hi + pltpu.roll(xg, 96, axis=1) * lo
            if g < W_QA // LANES:
                xg = xg * scale
        o_ref[:, g * LANES:(g + 1) * LANES] = xg.astype(o_ref.dtype)


def _proj_a(xb, w_a, tab_a, seq, tm=512):
    t = xb.shape[0]
    return pl.pallas_call(
        _proj_a_kernel,
        grid=(t // tm,),
        in_specs=[
            pl.BlockSpec((tm, D_MODEL), lambda i: (i, 0)),
            pl.BlockSpec((D_MODEL, W_A), lambda i: (0, 0)),
            pl.BlockSpec((3, tm, LANES), lambda i: (0, i % (seq // tm), 0)),
        ],
        out_specs=pl.BlockSpec((tm, W_A), lambda i: (i, 0)),
        out_shape=jax.ShapeDtypeStruct((t, W_A), BF16),
        compiler_params=_params("parallel"),
        name="proj_a",
    )(xb, w_a, tab_a)


def _proj_rest_kernel(x_ref, w_ref, tab_ref, o_ref, acc_ref, *, tn):
    j = pl.program_id(1)
    acc_ref[...] = _dot(x_ref[...], w_ref[...])
    n_q = W_QB // tn

    def rope(scale):
        cos, sgn = tab_ref[0], tab_ref[1]
        for g in range(tn // LANES):
            xg = acc_ref[:, g * LANES:(g + 1) * LANES]
            xg = xg * cos + pltpu.roll(xg, 64, axis=1) * sgn
            if scale is not None:
                xg = xg * scale
            o_ref[:, g * LANES:(g + 1) * LANES] = xg.astype(o_ref.dtype)

    @pl.when(j < n_q)
    def _():
        rope(HEAD_DIM_B ** -0.5)

    @pl.when((j >= n_q) & (j < 2 * n_q))
    def _():
        rope(None)

    @pl.when((j >= 2 * n_q) & (j < 3 * n_q))
    def _():
        o_ref[...] = acc_ref[...].astype(o_ref.dtype)

    @pl.when(j >= 3 * n_q)
    def _():
        o_ref[...] = jax.nn.sigmoid(acc_ref[...]).astype(o_ref.dtype)


def _proj_rest(xb, w_rest, tab_b, seq, tm=1024, tn=1024):
    t = xb.shape[0]
    return pl.pallas_call(
        functools.partial(_proj_rest_kernel, tn=tn),
        grid=(t // tm, W_REST // tn),
        in_specs=[
            pl.BlockSpec((tm, D_MODEL), lambda i, j: (i, 0)),
            pl.BlockSpec((D_MODEL, tn), lambda i, j: (0, j)),
            pl.BlockSpec((2, tm, LANES), lambda i, j: (0, i % (seq // tm), 0)),
        ],
        out_specs=pl.BlockSpec((tm, tn), lambda i, j: (i, j)),
        out_shape=jax.ShapeDtypeStruct((t, W_REST), BF16),
        scratch_shapes=[pltpu.VMEM((tm, tn), F32)],
        compiler_params=_params("parallel", "arbitrary"),
        name="proj_rest",
    )(xb, w_rest, tab_b)


def _attn_a_kernel(sinks_ref, q_ref, kvp_ref, kvc_ref, o_ref):
    n = pl.program_id(1)
    blk = WINDOW
    row = lax.broadcasted_iota(jnp.int32, (2 * blk, 1), 0)
    qi = jnp.where(row < blk, row, row - blk)
    kj = lax.broadcasted_iota(jnp.int32, (1, 2 * blk), 1)
    mask = (kj <= blk + qi) & (kj > qi) & ((kj >= blk) | (n > 0))
    low = lax.broadcasted_iota(jnp.int32, (1, LANES), 1) < HEAD_DIM_A
    first_head = row < blk
    zero = jnp.zeros((), BF16)

    for g in range(N_KV_A):
        ks = slice(g * HEAD_DIM_A, (g + 1) * HEAD_DIM_A)
        vs = slice(W_KA + g * HEAD_DIM_A, W_KA + (g + 1) * HEAD_DIM_A)
        kband = jnp.concatenate([kvp_ref[:, ks], kvc_ref[:, ks]], axis=0)
        vband = jnp.concatenate([kvp_ref[:, vs], kvc_ref[:, vs]], axis=0)
        kdup = jnp.concatenate([kband, kband], axis=1)
        vdup = jnp.concatenate([vband, vband], axis=1)
        for pr in range(GROUP_A // 2):
            h0 = g * GROUP_A + 2 * pr
            cs = slice(h0 * HEAD_DIM_A, (h0 + 2) * HEAD_DIM_A)
            qp = q_ref[:, cs]
            q2 = jnp.concatenate([jnp.where(low, qp, zero), jnp.where(low, zero, qp)], axis=0)
            s = _dot_nt(q2, kdup)
            s = jnp.where(mask, s, NEG_INF)
            sink = jnp.where(first_head, sinks_ref[h0], sinks_ref[h0 + 1])
            m = jnp.maximum(jnp.max(s, axis=-1, keepdims=True), sink)
            p = jnp.exp(s - m)
            denom = jnp.sum(p, axis=-1, keepdims=True) + jnp.exp(sink - m)
            o2 = _dot(p.astype(BF16), vdup) / denom
            o_ref[:, cs] = jnp.where(low, o2[:blk], o2[blk:]).astype(o_ref.dtype)


def _attn_a(qkv_a, sinks, batch, seq):
    blk = WINDOW
    nb = seq // blk
    kv_col = W_QA // (2 * W_KA)
    return pl.pallas_call(
        _attn_a_kernel,
        grid=(batch, nb),
        in_specs=[
            pl.BlockSpec(memory_space=pltpu.SMEM),
            pl.BlockSpec((blk, W_QA), lambda b, n: (b * nb + n, 0)),
            pl.BlockSpec((blk, 2 * W_KA), lambda b, n: (b * nb + jnp.maximum(n - 1, 0), kv_col)),
            pl.BlockSpec((blk, 2 * W_KA), lambda b, n: (b * nb + n, kv_col)),
        ],
        out_specs=pl.BlockSpec((blk, W_QA), lambda b, n: (b * nb + n, 0)),
        out_shape=jax.ShapeDtypeStruct((batch * seq, W_QA), BF16),
        compiler_params=_params("parallel", "parallel"),
        name="attn_a",
    )(sinks, qkv_a, qkv_a, qkv_a)


def _attn_b_kernel(lq1_ref, lk1_ref, lq2_ref, lk2_ref, g_ref, q_ref, k_ref, v_ref, o_ref,
                   m_ref, l_ref, acc_ref, *, tq, lam_init):
    qi = pl.program_id(2)
    hd = HEAD_DIM_B
    m_ref[...] = jnp.full(m_ref.shape, NEG_INF, F32)
    l_ref[...] = jnp.zeros(l_ref.shape, F32)
    acc_ref[...] = jnp.zeros(acc_ref.shape, F32)

    def step(kb, masked):
        start = pl.multiple_of(kb * tq, tq)
        k = k_ref[pl.ds(start, tq), :]
        v = v_ref[pl.ds(start, tq), :]
        for c in range(2):
            s = _dot_nt(q_ref[:, c * hd:(c + 1) * hd], k[:, c * hd:(c + 1) * hd])
            if masked:
                r = lax.broadcasted_iota(jnp.int32, (tq, tq), 0)
                cc = lax.broadcasted_iota(jnp.int32, (tq, tq), 1)
                s = jnp.where(cc <= r, s, NEG_INF)
            m_old = m_ref[c]
            m_new = jnp.maximum(m_old, jnp.max(s, axis=-1, keepdims=True))
            a = jnp.exp(m_old - m_new)
            p = jnp.exp(s - m_new)
            l_ref[c] = a * l_ref[c] + jnp.sum(p, axis=-1, keepdims=True)
            acc_ref[c] = a * acc_ref[c] + _dot(p.astype(BF16), v)
            m_ref[c] = m_new

    def body(kb, carry):
        step(kb, False)
        return carry

    lax.fori_loop(0, qi, body, 0)
    step(qi, True)

    lam = (jnp.exp(jnp.sum(lq1_ref[...] * lk1_ref[...], axis=-1, keepdims=True))
           - jnp.exp(jnp.sum(lq2_ref[...] * lk2_ref[...], axis=-1, keepdims=True))
           + lam_init)
    o = acc_ref[0] / l_ref[0] - lam * (acc_ref[1] / l_ref[1])
    o = o * lax.rsqrt(jnp.mean(o * o, axis=-1, keepdims=True) + SUBLN_EPS)
    o = o * g_ref[...] * (1.0 - lam_init)
    o_ref[...] = o.astype(o_ref.dtype)


def _attn_b(rest, lq1, lk1, lq2, lk2, subln_g, lam_init, batch, seq, tq=512):
    nq = seq // tq
    w = 2 * HEAD_DIM_B
    vec = lambda n: pl.BlockSpec((1, n), lambda b, h, i: (0, 0))
    return pl.pallas_call(
        functools.partial(_attn_b_kernel, tq=tq, lam_init=lam_init),
        grid=(batch, N_HEADS_B, nq),
        in_specs=[
            vec(HEAD_DIM_B), vec(HEAD_DIM_B), vec(HEAD_DIM_B), vec(HEAD_DIM_B), vec(w),
            pl.BlockSpec((tq, w), lambda b, h, i: (b * nq + i, h)),
            pl.BlockSpec((seq, w), lambda b, h, i: (b, N_HEADS_B + h)),
            pl.BlockSpec((seq, w), lambda b, h, i: (b, 2 * N_HEADS_B + h)),
        ],
        out_specs=pl.BlockSpec((tq, w), lambda b, h, i: (b * nq + i, h)),
        out_shape=jax.ShapeDtypeStruct((batch * seq, W_QB), BF16),
        scratch_shapes=[
            pltpu.VMEM((2, tq, 1), F32),
            pltpu.VMEM((2, tq, 1), F32),
            pltpu.VMEM((2, tq, w), F32),
        ],
        compiler_params=_params("parallel", "parallel", "arbitrary"),
        name="attn_b",
    )(lq1, lk1, lq2, lk2, subln_g, rest, rest, rest)


def _merge_kernel(oa_ref, ob_ref, wa_ref, wb_ref, ga_ref, gb_ref, o_ref):
    pa = _dot(oa_ref[...], wa_ref[...])
    pb = _dot(ob_ref[...], wb_ref[...])
    o_ref[...] = (ga_ref[...].astype(F32) * pa + gb_ref[...].astype(F32) * pb).astype(o_ref.dtype)


def _merge(o_a, o_b, w_pa, w_pb, rest, tm=1024, tn=1024):
    t = o_a.shape[0]
    gcol = 3 * W_QB // tn
    return pl.pallas_call(
        _merge_kernel,
        grid=(t // tm, D_MODEL // tn),
        in_specs=[
            pl.BlockSpec((tm, W_QA), lambda i, j: (i, 0)),
            pl.BlockSpec((tm, W_QB), lambda i, j: (i, 0)),
            pl.BlockSpec((W_QA, tn), lambda i, j: (0, j)),
            pl.BlockSpec((W_QB, tn), lambda i, j: (0, j)),
            pl.BlockSpec((tm, tn), lambda i, j: (i, gcol + j)),
            pl.BlockSpec((tm, tn), lambda i, j: (i, gcol + D_MODEL // tn + j)),
        ],
        out_specs=pl.BlockSpec((tm, tn), lambda i, j: (i, j)),
        out_shape=jax.ShapeDtypeStruct((t, D_MODEL), BF16),
        compiler_params=_params("parallel", "arbitrary"),
        name="merge",
    )(o_a, o_b, w_pa, w_pb, rest, rest)


def _out_ln_kernel(m_ref, w_ref, x_ref, g_ref, b_ref, y_ref, yb_ref):
    z = ALPHA * x_ref[...] + _dot(m_ref[...], w_ref[...])
    y = _layer_norm(z, g_ref[...], b_ref[...])
    y_ref[...] = y
    yb_ref[...] = y.astype(yb_ref.dtype)


def _out_ln(merged, w_out, x, g, b, tm=512):
    t = x.shape[0]
    row = pl.BlockSpec((tm, D_MODEL), lambda i: (i, 0))
    vec = pl.BlockSpec((1, D_MODEL), lambda i: (0, 0))
    return pl.pallas_call(
        _out_ln_kernel,
        grid=(t // tm,),
        in_specs=[row, pl.BlockSpec((D_MODEL, D_MODEL), lambda i: (0, 0)), row, vec, vec],
        out_specs=[row, row],
        out_shape=[jax.ShapeDtypeStruct((t, D_MODEL), F32), jax.ShapeDtypeStruct((t, D_MODEL), BF16)],
        compiler_params=_params("parallel"),
        name="out_ln",
    )(merged, w_out, x, g, b)


def _ffn_up_kernel(xp_ref, x_ref, wg_ref, wv_ref, cwg_ref, cwv_ref, cbg_ref, cbv_ref, o_ref,
                   *, tm, tiles_per_seq):
    i = pl.program_id(0)
    halo = BF16_ROWS
    xp = jnp.where(i % tiles_per_seq == 0, jnp.zeros((), BF16), xp_ref[...])
    x = x_ref[...]

    def conv(w_ref, cw_ref, cb_ref):
        u = jnp.concatenate([_dot(xp, w_ref[...]), _dot(x, w_ref[...])], axis=0)
        out = cb_ref[...]
        for tap in range(CONV_WIDTH):
            lo = halo - (CONV_WIDTH - 1) + tap
            out = out + cw_ref[tap:tap + 1, :] * u[lo:lo + tm]
        return out

    gate = conv(wg_ref, cwg_ref, cbg_ref)
    val = conv(wv_ref, cwv_ref, cbv_ref)
    o_ref[...] = (jax.nn.silu(gate) * val).astype(o_ref.dtype)


def _ffn_up(xb, w_up, conv_w, conv_b, seq, tm=1024, tn=512):
    t = xb.shape[0]
    nf = D_FF // tn
    halo = BF16_ROWS
    return pl.pallas_call(
        functools.partial(_ffn_up_kernel, tm=tm, tiles_per_seq=seq // tm),
        grid=(t // tm, nf),
        in_specs=[
            pl.BlockSpec((halo, D_MODEL), lambda i, j: (jnp.maximum(i * (tm // halo) - 1, 0), 0)),
            pl.BlockSpec((tm, D_MODEL), lambda i, j: (i, 0)),
            pl.BlockSpec((D_MODEL, tn), lambda i, j: (0, j)),
            pl.BlockSpec((D_MODEL, tn), lambda i, j: (0, nf + j)),
            pl.BlockSpec((CONV_WIDTH, tn), lambda i, j: (0, j)),
            pl.BlockSpec((CONV_WIDTH, tn), lambda i, j: (0, nf + j)),
            pl.BlockSpec((1, tn), lambda i, j: (0, j)),
            pl.BlockSpec((1, tn), lambda i, j: (0, nf + j)),
        ],
        out_specs=pl.BlockSpec((tm, tn), lambda i, j: (i, j)),
        out_shape=jax.ShapeDtypeStruct((t, D_FF), BF16),
        compiler_params=_params("parallel", "arbitrary"),
        name="ffn_up",
    )(xb, xb, w_up, w_up, conv_w, conv_w, conv_b, conv_b)


def _down_ln_kernel(a_ref, w_ref, x_ref, g_ref, b_ref, *refs, with_bf16):
    acc_ref = refs[-1]
    k = pl.program_id(1)

    @pl.when(k == 0)
    def _():
        acc_ref[...] = jnp.zeros(acc_ref.shape, F32)

    acc_ref[...] += _dot(a_ref[...], w_ref[...])

    @pl.when(k == pl.num_programs(1) - 1)
    def _():
        y = _layer_norm(ALPHA * x_ref[...] + acc_ref[...], g_ref[...], b_ref[...])
        refs[0][...] = y
        if with_bf16:
            refs[1][...] = y.astype(BF16)


def _down_ln(act, w_down, x, g, b, with_bf16, tm=512, tk=512):
    t = x.shape[0]
    row = pl.BlockSpec((tm, D_MODEL), lambda i, k: (i, 0))
    vec = pl.BlockSpec((1, D_MODEL), lambda i, k: (0, 0))
    out_specs = [row, row] if with_bf16 else [row]
    out_shape = [jax.ShapeDtypeStruct((t, D_MODEL), F32)]
    if with_bf16:
        out_shape.append(jax.ShapeDtypeStruct((t, D_MODEL), BF16))
    return pl.pallas_call(
        functools.partial(_down_ln_kernel, with_bf16=with_bf16),
        grid=(t // tm, D_FF // tk),
        in_specs=[
            pl.BlockSpec((tm, tk), lambda i, k: (i, k)),
            pl.BlockSpec((tk, D_MODEL), lambda i, k: (k, 0)),
            row, vec, vec,
        ],
        out_specs=out_specs,
        out_shape=out_shape,
        scratch_shapes=[pltpu.VMEM((tm, D_MODEL), F32)],
        compiler_params=_params("parallel", "arbitrary"),
        name="down_ln",
    )(act, w_down, x, g, b)


def kernel(x, w_in, sinks, lambda_q1, lambda_k1, lambda_q2, lambda_k2, subln_g, w_proj_a, w_proj_b,
           w_out, ln1_g, ln1_b, w_up, conv_w, conv_b, w_down, ln2_g, ln2_b):
    batch, seq, _ = x.shape
    t = batch * seq
    tab_a, tab_b = _rope_tables(seq)
    x = x.reshape(t, D_MODEL)
    xb = x.astype(BF16)
    row = lambda v: v.reshape(1, -1).astype(F32)
    for l in range(DEPTH):
        w_in_b = w_in[l].astype(BF16)
        qkv_a = _proj_a(xb, w_in_b[:, :W_A], tab_a, seq)
        rest = _proj_rest(xb, w_in_b[:, W_A:], tab_b, seq)
        o_a = _attn_a(qkv_a, sinks[l].astype(F32), batch, seq)
        lam_init = 0.8 - 0.6 * math.exp(-0.3 * l)
        o_b = _attn_b(rest, row(lambda_q1[l]), row(lambda_k1[l]), row(lambda_q2[l]), row(lambda_k2[l]),
                      row(subln_g[l]), lam_init, batch, seq)
        merged = _merge(o_a, o_b, w_proj_a[l].astype(BF16), w_proj_b[l].astype(BF16), rest)
        x, xb = _out_ln(merged, w_out[l].astype(BF16), x, row(ln1_g[l]), row(ln1_b[l]))
        act = _ffn_up(xb, w_up[l].astype(BF16), conv_w[l].astype(F32), row(conv_b[l]), seq)
        last = l == DEPTH - 1
        outs = _down_ln(act, w_down[l].astype(BF16), x, row(ln2_g[l]), row(ln2_b[l]), with_bf16=not last)
        x = outs[0]
        if not last:
            xb = outs[1]
    return x.reshape(batch, seq, D_MODEL)
```

```python
import functools
import math

import jax
import jax.numpy as jnp
from jax import lax
from jax.experimental import pallas as pl
from jax.experimental.pallas import tpu as pltpu

D_MODEL = 2048
DEPTH = 2
HEAD_DIM_A = 64
N_Q_A = 32
N_KV_A = 4
GROUP_A = N_Q_A // N_KV_A
WINDOW = 128
HEAD_DIM_B = 128
N_HEADS_B = D_MODEL // (2 * HEAD_DIM_B)
SUBLN_EPS = 1e-5
ROPE_THETA = 10000.0
D_FF = ((8 * D_MODEL // 3 + 255) // 256) * 256
CONV_WIDTH = 3
LN_EPS = 1e-5
ALPHA = (2 * DEPTH) ** 0.25
NEG_INF = -1e30

W_QA = N_Q_A * HEAD_DIM_A
W_KA = N_KV_A * HEAD_DIM_A
W_A = W_QA + 2 * W_KA
W_QB = N_HEADS_B * 2 * HEAD_DIM_B
W_REST = 3 * W_QB + 2 * D_MODEL

LANES = 128
BF16_ROWS = 16
VMEM_LIMIT = 56 * 1024 * 1024

F32 = jnp.float32
BF16 = jnp.bfloat16


def _params(*sem):
    return pltpu.CompilerParams(dimension_semantics=sem, vmem_limit_bytes=VMEM_LIMIT)


def _dot(a, b):
    return jnp.dot(a, b, preferred_element_type=F32)


def _dot_nt(a, b):
    return lax.dot_general(a, b, (((1,), (1,)), ((), ())), preferred_element_type=F32)


def _layer_norm(z, g, b):
    mu = jnp.mean(z, axis=-1, keepdims=True)
    zc = z - mu
    var = jnp.mean(zc * zc, axis=-1, keepdims=True)
    return zc * lax.rsqrt(var + LN_EPS) * g + b


def _rope_tables(seq):
    def cs(dim):
        inv = 1.0 / (ROPE_THETA ** (jnp.arange(0, dim, 2, dtype=F32) / dim))
        ang = jnp.arange(seq, dtype=F32)[:, None] * inv[None, :]
        return jnp.cos(ang), jnp.sin(ang)

    cos_a, sin_a = cs(HEAD_DIM_A)
    cos_b, sin_b = cs(HEAD_DIM_B)
    zeros_a = jnp.zeros_like(sin_a)
    tab_a = jnp.stack([
        jnp.tile(cos_a, (1, 4)),
        jnp.tile(jnp.concatenate([zeros_a, sin_a], axis=1), (1, 2)),
        jnp.tile(jnp.concatenate([-sin_a, zeros_a], axis=1), (1, 2)),
    ])
    tab_b = jnp.stack([
        jnp.tile(cos_b, (1, 2)),
        jnp.concatenate([-sin_b, sin_b], axis=1),
    ])
    return tab_a, tab_b


def _proj_a_kernel(x_ref, w_ref, tab_ref, o_ref):
    acc = _dot(x_ref[...], w_ref[...])
    cos, hi, lo = tab_ref[0], tab_ref[1], tab_ref[2]
    scale = HEAD_DIM_A ** -0.5
    n_rope = (W_QA + W_KA) // LANES
    for g in range(W_A // LANES):
        xg = acc[:, g * LANES:(g + 1) * LANES]
        if g < n_rope:
            xg = xg * cos + pltpu.roll(xg, 32, axis=1) * hi + pltpu.roll(xg, 96, axis=1) * lo
            if g < W_QA // LANES:
                xg = xg * scale
        o_ref[:, g * LANES:(g + 1) * LANES] = xg.astype(o_ref.dtype)


def _proj_a(xb, w_a, tab_a, seq, tm=512):
    t = xb.shape[0]
    return pl.pallas_call(
        _proj_a_kernel,
        grid=(t // tm,),
        in_specs=[
            pl.BlockSpec((tm, D_MODEL), lambda i: (i, 0)),
            pl.BlockSpec((D_MODEL, W_A), lambda i: (0, 0)),
            pl.BlockSpec((3, tm, LANES), lambda i: (0, i % (seq // tm), 0)),
        ],
        out_specs=pl.BlockSpec((tm, W_A), lambda i: (i, 0)),
        out_shape=jax.ShapeDtypeStruct((t, W_A), BF16),
        compiler_params=_params("parallel"),
        name="proj_a",
    )(xb, w_a, tab_a)


def _proj_rest_kernel(x_ref, w_ref, tab_ref, o_ref, acc_ref, *, tn):
    j = pl.program_id(1)
    acc_ref[...] = _dot(x_ref[...], w_ref[...])
    n_q = W_QB // tn

    def rope(scale):
        cos, sgn = tab_ref[0], tab_ref[1]
        for g in range(tn // LANES):
            xg = acc_ref[:, g * LANES:(g + 1) * LANES]
            xg = xg * cos + pltpu.roll(xg, 64, axis=1) * sgn
            if scale is not None:
                xg = xg * scale
            o_ref[:, g * LANES:(g + 1) * LANES] = xg.astype(o_ref.dtype)

    @pl.when(j < n_q)
    def _():
        rope(HEAD_DIM_B ** -0.5)

    @pl.when((j >= n_q) & (j < 2 * n_q))
    def _():
        rope(None)

    @pl.when((j >= 2 * n_q) & (j < 3 * n_q))
    def _():
        o_ref[...] = acc_ref[...].astype(o_ref.dtype)

    @pl.when(j >= 3 * n_q)
    def _():
        o_ref[...] = jax.nn.sigmoid(acc_ref[...]).astype(o_ref.dtype)


def _proj_rest(xb, w_rest, tab_b, seq, tm=1024, tn=1024):
    t = xb.shape[0]
    return pl.pallas_call(
        functools.partial(_proj_rest_kernel, tn=tn),
        grid=(t // tm, W_REST // tn),
        in_specs=[
            pl.BlockSpec((tm, D_MODEL), lambda i, j: (i, 0)),
            pl.BlockSpec((D_MODEL, tn), lambda i, j: (0, j)),
            pl.BlockSpec((2, tm, LANES), lambda i, j: (0, i % (seq // tm), 0)),
        ],
        out_specs=pl.BlockSpec((tm, tn), lambda i, j: (i, j)),
        out_shape=jax.ShapeDtypeStruct((t, W_REST), BF16),
        scratch_shapes=[pltpu.VMEM((tm, tn), F32)],
        compiler_params=_params("parallel", "arbitrary"),
        name="proj_rest",
    )(xb, w_rest, tab_b)


def _attn_a_kernel(sinks_ref, q_ref, kvp_ref, kvc_ref, o_ref):
    n = pl.program_id(1)
    blk = WINDOW
    row = lax.broadcasted_iota(jnp.int32, (2 * blk, 1), 0)
    qi = jnp.where(row < blk, row, row - blk)
    kj = lax.broadcasted_iota(jnp.int32, (1, 2 * blk), 1)
    mask = (kj <= blk + qi) & (kj > qi) & ((kj >= blk) | (n > 0))
    low = lax.broadcasted_iota(jnp.int32, (1, LANES), 1) < HEAD_DIM_A
    first_head = row < blk
    zero = jnp.zeros((), BF16)

    for g in range(N_KV_A):
        ks = slice(g * HEAD_DIM_A, (g + 1) * HEAD_DIM_A)
        vs = slice(W_KA + g * HEAD_DIM_A, W_KA + (g + 1) * HEAD_DIM_A)
        kband = jnp.concatenate([kvp_ref[:, ks], kvc_ref[:, ks]], axis=0)
        vband = jnp.concatenate([kvp_ref[:, vs], kvc_ref[:, vs]], axis=0)
        kdup = jnp.concatenate([kband, kband], axis=1)
        vdup = jnp.concatenate([vband, vband], axis=1)
        for pr in range(GROUP_A // 2):
            h0 = g * GROUP_A + 2 * pr
            cs = slice(h0 * HEAD_DIM_A, (h0 + 2) * HEAD_DIM_A)
            qp = q_ref[:, cs]
            q2 = jnp.concatenate([jnp.where(low, qp, zero), jnp.where(low, zero, qp)], axis=0)
            s = _dot_nt(q2, kdup)
            s = jnp.where(mask, s, NEG_INF)
            sink = jnp.where(first_head, sinks_ref[h0], sinks_ref[h0 + 1])
            m = jnp.maximum(jnp.max(s, axis=-1, keepdims=True), sink)
            p = jnp.exp(s - m)
            denom = jnp.sum(p, axis=-1, keepdims=True) + jnp.exp(sink - m)
            o2 = _dot(p.astype(BF16), vdup) / denom
            o_ref[:, cs] = jnp.where(low, o2[:blk], o2[blk:]).astype(o_ref.dtype)


def _attn_a(qkv_a, sinks, batch, seq):
    blk = WINDOW
    nb = seq // blk
    kv_col = W_QA // (2 * W_KA)
    return pl.pallas_call(
        _attn_a_kernel,
        grid=(batch, nb),
        in_specs=[
            pl.BlockSpec(memory_space=pltpu.SMEM),
            pl.BlockSpec((blk, W_QA), lambda b, n: (b * nb + n, 0)),
            pl.BlockSpec((blk, 2 * W_KA), lambda b, n: (b * nb + jnp.maximum(n - 1, 0), kv_col)),
            pl.BlockSpec((blk, 2 * W_KA), lambda b, n: (b * nb + n, kv_col)),
        ],
        out_specs=pl.BlockSpec((blk, W_QA), lambda b, n: (b * nb + n, 0)),
        out_shape=jax.ShapeDtypeStruct((batch * seq, W_QA), BF16),
        compiler_params=_params("parallel", "parallel"),
        name="attn_a",
    )(sinks, qkv_a, qkv_a, qkv_a)


def _attn_b_kernel(lq1_ref, lk1_ref, lq2_ref, lk2_ref, g_ref, q_ref, k_ref, vt_ref, o_ref,
                   qt_ref, m_ref, l_ref, acc_ref, *, tq, lam_init):
    qi = pl.program_id(2)
    hd = HEAD_DIM_B
    q = q_ref[...].astype(F32)
    for c in range(2):
        qt_ref[c] = q[:, c * hd:(c + 1) * hd].T.astype(BF16)
    m_ref[...] = jnp.full(m_ref.shape, NEG_INF, F32)
    l_ref[...] = jnp.zeros(l_ref.shape, F32)
    acc_ref[...] = jnp.zeros(acc_ref.shape, F32)

    def step(kb, masked):
        start = pl.multiple_of(kb * tq, tq)
        k = k_ref[pl.ds(start, tq), :]
        vt = vt_ref[kb]
        s = [_dot(k[:, c * hd:(c + 1) * hd], qt_ref[c]) for c in range(2)]
        for c in range(2):
            sc = s[c]
            if masked:
                key = lax.broadcasted_iota(jnp.int32, (tq, tq), 0)
                qry = lax.broadcasted_iota(jnp.int32, (tq, tq), 1)
                sc = jnp.where(key <= qry, sc, NEG_INF)
            m_old = m_ref[c]
            m_new = jnp.maximum(m_old, jnp.max(sc, axis=0, keepdims=True))
            a = jnp.exp(m_old - m_new)
            p = jnp.exp(sc - m_new)
            l_ref[c] = a * l_ref[c] + jnp.sum(p, axis=0, keepdims=True)
            acc_ref[c] = a * acc_ref[c] + _dot(vt, p.astype(BF16))
            m_ref[c] = m_new

    def body(kb, carry):
        step(kb, False)
        return carry

    lax.fori_loop(0, qi, body, 0)
    step(qi, True)

    lam = (jnp.exp(jnp.sum(lq1_ref[...] * lk1_ref[...], axis=-1, keepdims=True))
           - jnp.exp(jnp.sum(lq2_ref[...] * lk2_ref[...], axis=-1, keepdims=True))
           + lam_init)
    o = acc_ref[0] / l_ref[0] - lam * (acc_ref[1] / l_ref[1])
    o = o * lax.rsqrt(jnp.mean(o * o, axis=0, keepdims=True) + SUBLN_EPS)
    o = o * g_ref[...] * (1.0 - lam_init)
    o_ref[...] = o.T.astype(o_ref.dtype)


def _attn_b(rest, vt, lq1, lk1, lq2, lk2, subln_g, lam_init, batch, seq, tq=512):
    nq = seq // tq
    w = 2 * HEAD_DIM_B
    vec = lambda n: pl.BlockSpec((1, n), lambda b, h, i: (0, 0))
    return pl.pallas_call(
        functools.partial(_attn_b_kernel, tq=tq, lam_init=lam_init),
        grid=(batch, N_HEADS_B, nq),
        in_specs=[
            vec(HEAD_DIM_B), vec(HEAD_DIM_B), vec(HEAD_DIM_B), vec(HEAD_DIM_B),
            pl.BlockSpec((w, 1), lambda b, h, i: (0, 0)),
            pl.BlockSpec((tq, w), lambda b, h, i: (b * nq + i, h)),
            pl.BlockSpec((seq, w), lambda b, h, i: (b, N_HEADS_B + h)),
            pl.BlockSpec((None, None, nq, w, tq), lambda b, h, i: (b, h, 0, 0, 0)),
        ],
        out_specs=pl.BlockSpec((tq, w), lambda b, h, i: (b * nq + i, h)),
        out_shape=jax.ShapeDtypeStruct((batch * seq, W_QB), BF16),
        scratch_shapes=[
            pltpu.VMEM((2, HEAD_DIM_B, tq), BF16),
            pltpu.VMEM((2, 1, tq), F32),
            pltpu.VMEM((2, 1, tq), F32),
            pltpu.VMEM((2, w, tq), F32),
        ],
        compiler_params=_params("parallel", "parallel", "arbitrary"),
        name="attn_b",
    )(lq1, lk1, lq2, lk2, subln_g, rest, rest, vt)


def _transposed_values(rest, batch, seq, tk):
    w = 2 * HEAD_DIM_B
    vb = rest[:, 2 * W_QB:3 * W_QB].reshape(batch, seq // tk, tk, N_HEADS_B, w)
    return vb.transpose(0, 3, 1, 4, 2)


def _merge_kernel(oa_ref, ob_ref, wa_ref, wb_ref, ga_ref, gb_ref, o_ref):
    pa = _dot(oa_ref[...], wa_ref[...])
    pb = _dot(ob_ref[...], wb_ref[...])
    o_ref[...] = (ga_ref[...].astype(F32) * pa + gb_ref[...].astype(F32) * pb).astype(o_ref.dtype)


def _merge(o_a, o_b, w_pa, w_pb, rest, tm=1024, tn=1024):
    t = o_a.shape[0]
    gcol = 3 * W_QB // tn
    return pl.pallas_call(
        _merge_kernel,
        grid=(t // tm, D_MODEL // tn),
        in_specs=[
            pl.BlockSpec((tm, W_QA), lambda i, j: (i, 0)),
            pl.BlockSpec((tm, W_QB), lambda i, j: (i, 0)),
            pl.BlockSpec((W_QA, tn), lambda i, j: (0, j)),
            pl.BlockSpec((W_QB, tn), lambda i, j: (0, j)),
            pl.BlockSpec((tm, tn), lambda i, j: (i, gcol + j)),
            pl.BlockSpec((tm, tn), lambda i, j: (i, gcol + D_MODEL // tn + j)),
        ],
        out_specs=pl.BlockSpec((tm, tn), lambda i, j: (i, j)),
        out_shape=jax.ShapeDtypeStruct((t, D_MODEL), BF16),
        compiler_params=_params("parallel", "arbitrary"),
        name="merge",
    )(o_a, o_b, w_pa, w_pb, rest, rest)


def _out_ln_kernel(m_ref, w_ref, x_ref, g_ref, b_ref, y_ref, yb_ref):
    z = ALPHA * x_ref[...] + _dot(m_ref[...], w_ref[...])
    y = _layer_norm(z, g_ref[...], b_ref[...])
    y_ref[...] = y
    yb_ref[...] = y.astype(yb_ref.dtype)


def _out_ln(merged, w_out, x, g, b, tm=512):
    t = x.shape[0]
    row = pl.BlockSpec((tm, D_MODEL), lambda i: (i, 0))
    vec = pl.BlockSpec((1, D_MODEL), lambda i: (0, 0))
    return pl.pallas_call(
        _out_ln_kernel,
        grid=(t // tm,),
        in_specs=[row, pl.BlockSpec((D_MODEL, D_MODEL), lambda i: (0, 0)), row, vec, vec],
        out_specs=[row, row],
        out_shape=[jax.ShapeDtypeStruct((t, D_MODEL), F32), jax.ShapeDtypeStruct((t, D_MODEL), BF16)],
        compiler_params=_params("parallel"),
        name="out_ln",
    )(merged, w_out, x, g, b)


def _ffn_up_kernel(xp_ref, x_ref, wg_ref, wv_ref, cwg_ref, cwv_ref, cbg_ref, cbv_ref, o_ref,
                   *, tm, tiles_per_seq):
    i = pl.program_id(0)
    halo = BF16_ROWS
    xp = jnp.where(i % tiles_per_seq == 0, jnp.zeros((), BF16), xp_ref[...])
    x = x_ref[...]

    def conv(w_ref, cw_ref, cb_ref):
        u = jnp.concatenate([_dot(xp, w_ref[...]), _dot(x, w_ref[...])], axis=0)
        out = cb_ref[...]
        for tap in range(CONV_WIDTH):
            lo = halo - (CONV_WIDTH - 1) + tap
            out = out + cw_ref[tap:tap + 1, :] * u[lo:lo + tm]
        return out

    gate = conv(wg_ref, cwg_ref, cbg_ref)
    val = conv(wv_ref, cwv_ref, cbv_ref)
    o_ref[...] = (jax.nn.silu(gate) * val).astype(o_ref.dtype)


def _ffn_up(xb, w_up, conv_w, conv_b, seq, tm=1024, tn=512):
    t = xb.shape[0]
    nf = D_FF // tn
    halo = BF16_ROWS
    return pl.pallas_call(
        functools.partial(_ffn_up_kernel, tm=tm, tiles_per_seq=seq // tm),
        grid=(t // tm, nf),
        in_specs=[
            pl.BlockSpec((halo, D_MODEL), lambda i, j: (jnp.maximum(i * (tm // halo) - 1, 0), 0)),
            pl.BlockSpec((tm, D_MODEL), lambda i, j: (i, 0)),
            pl.BlockSpec((D_MODEL, tn), lambda i, j: (0, j)),
            pl.BlockSpec((D_MODEL, tn), lambda i, j: (0, nf + j)),
            pl.BlockSpec((CONV_WIDTH, tn), lambda i, j: (0, j)),
            pl.BlockSpec((CONV_WIDTH, tn), lambda i, j: (0, nf + j)),
            pl.BlockSpec((1, tn), lambda i, j: (0, j)),
            pl.BlockSpec((1, tn), lambda i, j: (0, nf + j)),
        ],
        out_specs=pl.BlockSpec((tm, tn), lambda i, j: (i, j)),
        out_shape=jax.ShapeDtypeStruct((t, D_FF), BF16),
        compiler_params=_params("parallel", "arbitrary"),
        name="ffn_up",
    )(xb, xb, w_up, w_up, conv_w, conv_w, conv_b, conv_b)


def _down_ln_kernel(a_ref, w_ref, x_ref, g_ref, b_ref, *refs, with_bf16):
    acc_ref = refs[-1]
    k = pl.program_id(1)

    @pl.when(k == 0)
    def _():
        acc_ref[...] = jnp.zeros(acc_ref.shape, F32)

    acc_ref[...] += _dot(a_ref[...], w_ref[...])

    @pl.when(k == pl.num_programs(1) - 1)
    def _():
        y = _layer_norm(ALPHA * x_ref[...] + acc_ref[...], g_ref[...], b_ref[...])
        refs[0][...] = y
        if with_bf16:
            refs[1][...] = y.astype(BF16)


def _down_ln(act, w_down, x, g, b, with_bf16, tm=512, tk=512):
    t = x.shape[0]
    row = pl.BlockSpec((tm, D_MODEL), lambda i, k: (i, 0))
    vec = pl.BlockSpec((1, D_MODEL), lambda i, k: (0, 0))
    out_specs = [row, row] if with_bf16 else [row]
    out_shape = [jax.ShapeDtypeStruct((t, D_MODEL), F32)]
    if with_bf16:
        out_shape.append(jax.ShapeDtypeStruct((t, D_MODEL), BF16))
    return pl.pallas_call(
        functools.partial(_down_ln_kernel, with_bf16=with_bf16),
        grid=(t // tm, D_FF // tk),
        in_specs=[
            pl.BlockSpec((tm, tk), lambda i, k: (i, k)),
            pl.BlockSpec((tk, D_MODEL), lambda i, k: (k, 0)),
            row, vec, vec,
        ],
        out_specs=out_specs,
        out_shape=out_shape,
        scratch_shapes=[pltpu.VMEM((tm, D_MODEL), F32)],
        compiler_params=_params("parallel", "arbitrary"),
        name="down_ln",
    )(act, w_down, x, g, b)


def kernel(x, w_in, sinks, lambda_q1, lambda_k1, lambda_q2, lambda_k2, subln_g, w_proj_a, w_proj_b,
           w_out, ln1_g, ln1_b, w_up, conv_w, conv_b, w_down, ln2_g, ln2_b):
    batch, seq, _ = x.shape
    t = batch * seq
    tab_a, tab_b = _rope_tables(seq)
    x = x.reshape(t, D_MODEL)
    xb = x.astype(BF16)
    row = lambda v: v.reshape(1, -1).astype(F32)
    for l in range(DEPTH):
        w_in_b = w_in[l].astype(BF16)
        qkv_a = _proj_a(xb, w_in_b[:, :W_A], tab_a, seq)
        rest = _proj_rest(xb, w_in_b[:, W_A:], tab_b, seq)
        o_a = _attn_a(qkv_a, sinks[l].astype(F32), batch, seq)
        lam_init = 0.8 - 0.6 * math.exp(-0.3 * l)
        o_b = _attn_b(rest, _transposed_values(rest, batch, seq, 512),
                      row(lambda_q1[l]), row(lambda_k1[l]), row(lambda_q2[l]), row(lambda_k2[l]),
                      subln_g[l].reshape(-1, 1).astype(F32), lam_init, batch, seq)
        merged = _merge(o_a, o_b, w_proj_a[l].astype(BF16), w_proj_b[l].astype(BF16), rest)
        x, xb = _out_ln(merged, w_out[l].astype(BF16), x, row(ln1_g[l]), row(ln1_b[l]))
        act = _ffn_up(xb, w_up[l].astype(BF16), conv_w[l].astype(F32), row(conv_b[l]), seq)
        last = l == DEPTH - 1
        outs = _down_ln(act, w_down[l].astype(BF16), x, row(ln2_g[l]), row(ln2_b[l]), with_bf16=not last)
        x = outs[0]
        if not last:
            xb = outs[1]
    return x.reshape(batch, seq, D_MODEL)
```

```python
import functools
import math

import jax
import jax.numpy as jnp
from jax import lax
from jax.experimental import pallas as pl
from jax.experimental.pallas import tpu as pltpu

D_MODEL = 2048
DEPTH = 2
HEAD_DIM_A = 64
N_Q_A = 32
N_KV_A = 4
GROUP_A = N_Q_A // N_KV_A
WINDOW = 128
HEAD_DIM_B = 128
N_HEADS_B = D_MODEL // (2 * HEAD_DIM_B)
SUBLN_EPS = 1e-5
ROPE_THETA = 10000.0
D_FF = ((8 * D_MODEL // 3 + 255) // 256) * 256
CONV_WIDTH = 3
LN_EPS = 1e-5
ALPHA = (2 * DEPTH) ** 0.25
NEG_INF = -1e30

W_QA = N_Q_A * HEAD_DIM_A
W_KA = N_KV_A * HEAD_DIM_A
W_A = W_QA + 2 * W_KA
W_QB = N_HEADS_B * 2 * HEAD_DIM_B

LANES = 128
MXU_COLS = 256
BF16_ROWS = 16
VMEM_LIMIT = 56 * 1024 * 1024

F32 = jnp.float32
BF16 = jnp.bfloat16


def _params(*sem):
    return pltpu.CompilerParams(dimension_semantics=sem, vmem_limit_bytes=VMEM_LIMIT)


def _dot(a, b):
    return jnp.dot(a, b, preferred_element_type=F32)


def _dot_nt(a, b):
    return lax.dot_general(a, b, (((1,), (1,)), ((), ())), preferred_element_type=F32)


def _layer_norm(z, g, b):
    mu = jnp.mean(z, axis=-1, keepdims=True)
    zc = z - mu
    var = jnp.mean(zc * zc, axis=-1, keepdims=True)
    return zc * lax.rsqrt(var + LN_EPS) * g + b


def _rope_tables(seq):
    def cs(dim):
        inv = 1.0 / (ROPE_THETA ** (jnp.arange(0, dim, 2, dtype=F32) / dim))
        ang = jnp.arange(seq, dtype=F32)[:, None] * inv[None, :]
        return jnp.cos(ang), jnp.sin(ang)

    cos_a, sin_a = cs(HEAD_DIM_A)
    cos_b, sin_b = cs(HEAD_DIM_B)
    zeros_a = jnp.zeros_like(sin_a)
    tab_a = jnp.stack([
        jnp.tile(cos_a, (1, 4)),
        jnp.tile(jnp.concatenate([zeros_a, sin_a], axis=1), (1, 2)),
        jnp.tile(jnp.concatenate([-sin_a, zeros_a], axis=1), (1, 2)),
    ])
    tab_b = jnp.stack([
        jnp.tile(cos_b, (1, 2)),
        jnp.concatenate([-sin_b, sin_b], axis=1),
    ])
    return tab_a, tab_b


def _proj_a_kernel(x_ref, w_ref, tab_ref, o_ref):
    acc = _dot(x_ref[...], w_ref[...])
    cos, hi, lo = tab_ref[0], tab_ref[1], tab_ref[2]
    scale = HEAD_DIM_A ** -0.5
    n_rope = (W_QA + W_KA) // LANES
    for g in range(W_A // LANES):
        xg = acc[:, g * LANES:(g + 1) * LANES]
        if g < n_rope:
            xg = xg * cos + pltpu.roll(xg, 32, axis=1) * hi + pltpu.roll(xg, 96, axis=1) * lo
            if g < W_QA // LANES:
                xg = xg * scale
        o_ref[:, g * LANES:(g + 1) * LANES] = xg.astype(o_ref.dtype)


def _proj_a(xb, w_a, layer, tab_a, seq, tm=512):
    t = xb.shape[0]
    return pl.pallas_call(
        _proj_a_kernel,
        grid=(t // tm,),
        in_specs=[
            pl.BlockSpec((tm, D_MODEL), lambda i: (i, 0)),
            pl.BlockSpec((None, D_MODEL, W_A), lambda i: (layer, 0, 0)),
            pl.BlockSpec((3, tm, LANES), lambda i: (0, i % (seq // tm), 0)),
        ],
        out_specs=pl.BlockSpec((tm, W_A), lambda i: (i, 0)),
        out_shape=jax.ShapeDtypeStruct((t, W_A), BF16),
        compiler_params=_params("parallel"),
        name="proj_a",
    )(xb, w_a, tab_a)


def _proj_rope_b_kernel(x_ref, w_ref, tab_ref, o_ref, *, tn, n_scaled):
    j = pl.program_id(1)
    scale = jnp.where(j < n_scaled, HEAD_DIM_B ** -0.5, 1.0)
    cos = tab_ref[0] * scale
    sgn = tab_ref[1] * scale
    x = x_ref[...]
    for c in range(tn // MXU_COLS):
        acc = _dot(x, w_ref[:, c * MXU_COLS:(c + 1) * MXU_COLS])
        for g in range(MXU_COLS // LANES):
            xg = acc[:, g * LANES:(g + 1) * LANES]
            lo = c * MXU_COLS + g * LANES
            o_ref[:, lo:lo + LANES] = (xg * cos + pltpu.roll(xg, 64, axis=1) * sgn).astype(o_ref.dtype)


def _proj_qk_b(xb, w_qk, layer, tab_b, seq, tm=1024, tn=1024):
    t = xb.shape[0]
    return pl.pallas_call(
        functools.partial(_proj_rope_b_kernel, tn=tn, n_scaled=W_QB // tn),
        grid=(t // tm, 2 * W_QB // tn),
        in_specs=[
            pl.BlockSpec((tm, D_MODEL), lambda i, j: (i, 0)),
            pl.BlockSpec((None, D_MODEL, tn), lambda i, j: (layer, 0, j)),
            pl.BlockSpec((2, tm, LANES), lambda i, j: (0, i % (seq // tm), 0)),
        ],
        out_specs=pl.BlockSpec((tm, tn), lambda i, j: (i, j)),
        out_shape=jax.ShapeDtypeStruct((t, 2 * W_QB), BF16),
        compiler_params=_params("parallel", "arbitrary"),
        name="proj_qk_b",
    )(xb, w_qk, tab_b)


def _proj_act_kernel(x_ref, w_ref, o_ref, *, tn, sigmoid):
    x = x_ref[...]
    for c in range(tn // MXU_COLS):
        cs = slice(c * MXU_COLS, (c + 1) * MXU_COLS)
        acc = _dot(x, w_ref[:, cs])
        if sigmoid:
            acc = jax.nn.sigmoid(acc)
        o_ref[:, cs] = acc.astype(o_ref.dtype)


def _proj_act(xb, w, layer, sigmoid, name, tm=1024, tn=1024):
    t = xb.shape[0]
    n = w.shape[-1]
    return pl.pallas_call(
        functools.partial(_proj_act_kernel, tn=tn, sigmoid=sigmoid),
        grid=(t // tm, n // tn),
        in_specs=[
            pl.BlockSpec((tm, D_MODEL), lambda i, j: (i, 0)),
            pl.BlockSpec((None, D_MODEL, tn), lambda i, j: (layer, 0, j)),
        ],
        out_specs=pl.BlockSpec((tm, tn), lambda i, j: (i, j)),
        out_shape=jax.ShapeDtypeStruct((t, n), BF16),
        compiler_params=_params("parallel", "arbitrary"),
        name=name,
    )(xb, w)


def _attn_a_kernel(sinks_ref, q_ref, kvp_ref, kvc_ref, o_ref):
    n = pl.program_id(1)
    blk = WINDOW
    row = lax.broadcasted_iota(jnp.int32, (2 * blk, 1), 0)
    qi = jnp.where(row < blk, row, row - blk)
    kj = lax.broadcasted_iota(jnp.int32, (1, 2 * blk), 1)
    mask = (kj <= blk + qi) & (kj > qi) & ((kj >= blk) | (n > 0))
    low = lax.broadcasted_iota(jnp.int32, (1, LANES), 1) < HEAD_DIM_A
    first_head = row < blk
    zero = jnp.zeros((), BF16)

    for g in range(N_KV_A):
        ks = slice(g * HEAD_DIM_A, (g + 1) * HEAD_DIM_A)
        vs = slice(W_KA + g * HEAD_DIM_A, W_KA + (g + 1) * HEAD_DIM_A)
        kband = jnp.concatenate([kvp_ref[:, ks], kvc_ref[:, ks]], axis=0)
        vband = jnp.concatenate([kvp_ref[:, vs], kvc_ref[:, vs]], axis=0)
        kdup = jnp.concatenate([kband, kband], axis=1)
        vdup = jnp.concatenate([vband, vband], axis=1)
        for pr in range(GROUP_A // 2):
            h0 = g * GROUP_A + 2 * pr
            cs = slice(h0 * HEAD_DIM_A, (h0 + 2) * HEAD_DIM_A)
            qp = q_ref[:, cs]
            q2 = jnp.concatenate([jnp.where(low, qp, zero), jnp.where(low, zero, qp)], axis=0)
            s = _dot_nt(q2, kdup)
            s = jnp.where(mask, s, NEG_INF)
            sink = jnp.where(first_head, sinks_ref[h0], sinks_ref[h0 + 1])
            m = jnp.maximum(jnp.max(s, axis=-1, keepdims=True), sink)
            p = jnp.exp(s - m)
            denom = jnp.sum(p, axis=-1, keepdims=True) + jnp.exp(sink - m)
            o2 = _dot(p.astype(BF16), vdup) / denom
            o_ref[:, cs] = jnp.where(low, o2[:blk], o2[blk:]).astype(o_ref.dtype)


def _attn_a(qkv_a, sinks, batch, seq):
    blk = WINDOW
    nb = seq // blk
    kv_col = W_QA // (2 * W_KA)
    return pl.pallas_call(
        _attn_a_kernel,
        grid=(batch, nb),
        in_specs=[
            pl.BlockSpec(memory_space=pltpu.SMEM),
            pl.BlockSpec((blk, W_QA), lambda b, n: (b * nb + n, 0)),
            pl.BlockSpec((blk, 2 * W_KA), lambda b, n: (b * nb + jnp.maximum(n - 1, 0), kv_col)),
            pl.BlockSpec((blk, 2 * W_KA), lambda b, n: (b * nb + n, kv_col)),
        ],
        out_specs=pl.BlockSpec((blk, W_QA), lambda b, n: (b * nb + n, 0)),
        out_shape=jax.ShapeDtypeStruct((batch * seq, W_QA), BF16),
        compiler_params=_params("parallel", "parallel"),
        name="attn_a",
    )(sinks, qkv_a, qkv_a, qkv_a)


def _attn_b_kernel(lq1_ref, lk1_ref, lq2_ref, lk2_ref, g_ref, q_ref, k_ref, vt_ref, o_ref,
                   qt_ref, m_ref, l_ref, acc_ref, *, tq, lam_init):
    qi = pl.program_id(2)
    hd = HEAD_DIM_B
    q = q_ref[...].astype(F32)
    for c in range(2):
        qt_ref[c] = q[:, c * hd:(c + 1) * hd].T.astype(BF16)
    m_ref[...] = jnp.full(m_ref.shape, NEG_INF, F32)
    l_ref[...] = jnp.zeros(l_ref.shape, F32)
    acc_ref[...] = jnp.zeros(acc_ref.shape, F32)

    def step(kb, masked):
        start = pl.multiple_of(kb * tq, tq)
        k = k_ref[pl.ds(start, tq), :]
        vt = vt_ref[kb]
        s = [_dot(k[:, c * hd:(c + 1) * hd], qt_ref[c]) for c in range(2)]
        for c in range(2):
            sc = s[c]
            if masked:
                key = lax.broadcasted_iota(jnp.int32, (tq, tq), 0)
                qry = lax.broadcasted_iota(jnp.int32, (tq, tq), 1)
                sc = jnp.where(key <= qry, sc, NEG_INF)
            m_old = m_ref[c]
            m_new = jnp.maximum(m_old, jnp.max(sc, axis=0, keepdims=True))
            a = jnp.exp(m_old - m_new)
            p = jnp.exp(sc - m_new)
            l_ref[c] = a * l_ref[c] + jnp.sum(p, axis=0, keepdims=True)
            acc_ref[c] = a * acc_ref[c] + _dot(vt, p.astype(BF16))
            m_ref[c] = m_new

    def body(kb, carry):
        step(kb, False)
        return carry

    lax.fori_loop(0, qi, body, 0)
    step(qi, True)

    lam = (jnp.exp(jnp.sum(lq1_ref[...] * lk1_ref[...], axis=-1, keepdims=True))
           - jnp.exp(jnp.sum(lq2_ref[...] * lk2_ref[...], axis=-1, keepdims=True))
           + lam_init)
    o = acc_ref[0] / l_ref[0] - lam * (acc_ref[1] / l_ref[1])
    o = o * lax.rsqrt(jnp.mean(o * o, axis=0, keepdims=True) + SUBLN_EPS)
    o = o * g_ref[...] * (1.0 - lam_init)
    o_ref[...] = o.T.astype(o_ref.dtype)


def _attn_b(qk_b, vt, lq1, lk1, lq2, lk2, subln_g, lam_init, batch, seq, tq=512):
    nq = seq // tq
    w = 2 * HEAD_DIM_B
    vec = lambda n: pl.BlockSpec((1, n), lambda b, h, i: (0, 0))
    return pl.pallas_call(
        functools.partial(_attn_b_kernel, tq=tq, lam_init=lam_init),
        grid=(batch, N_HEADS_B, nq),
        in_specs=[
            vec(HEAD_DIM_B), vec(HEAD_DIM_B), vec(HEAD_DIM_B), vec(HEAD_DIM_B),
            pl.BlockSpec((w, 1), lambda b, h, i: (0, 0)),
            pl.BlockSpec((tq, w), lambda b, h, i: (b * nq + i, h)),
            pl.BlockSpec((seq, w), lambda b, h, i: (b, N_HEADS_B + h)),
            pl.BlockSpec((None, None, nq, w, tq), lambda b, h, i: (b, h, 0, 0, 0)),
        ],
        out_specs=pl.BlockSpec((tq, w), lambda b, h, i: (b * nq + i, h)),
        out_shape=jax.ShapeDtypeStruct((batch * seq, W_QB), BF16),
        scratch_shapes=[
            pltpu.VMEM((2, HEAD_DIM_B, tq), BF16),
            pltpu.VMEM((2, 1, tq), F32),
            pltpu.VMEM((2, 1, tq), F32),
            pltpu.VMEM((2, w, tq), F32),
        ],
        compiler_params=_params("parallel", "parallel", "arbitrary"),
        name="attn_b",
    )(lq1, lk1, lq2, lk2, subln_g, qk_b, qk_b, vt)


def _transposed_values(v_b, batch, seq, tk):
    vb = v_b.reshape(batch, seq // tk, tk, N_HEADS_B, 2 * HEAD_DIM_B)
    return vb.transpose(0, 3, 1, 4, 2)


def _merge_kernel(oa_ref, ob_ref, wa_ref, wb_ref, ga_ref, gb_ref, o_ref):
    pa = _dot(oa_ref[...], wa_ref[...])
    pb = _dot(ob_ref[...], wb_ref[...])
    o_ref[...] = (ga_ref[...].astype(F32) * pa + gb_ref[...].astype(F32) * pb).astype(o_ref.dtype)


def _merge(o_a, o_b, w_pa, w_pb, layer, gates, tm=1024, tn=1024):
    t = o_a.shape[0]
    return pl.pallas_call(
        _merge_kernel,
        grid=(t // tm, D_MODEL // tn),
        in_specs=[
            pl.BlockSpec((tm, W_QA), lambda i, j: (i, 0)),
            pl.BlockSpec((tm, W_QB), lambda i, j: (i, 0)),
            pl.BlockSpec((None, W_QA, tn), lambda i, j: (layer, 0, j)),
            pl.BlockSpec((None, W_QB, tn), lambda i, j: (layer, 0, j)),
            pl.BlockSpec((tm, tn), lambda i, j: (i, j)),
            pl.BlockSpec((tm, tn), lambda i, j: (i, D_MODEL // tn + j)),
        ],
        out_specs=pl.BlockSpec((tm, tn), lambda i, j: (i, j)),
        out_shape=jax.ShapeDtypeStruct((t, D_MODEL), BF16),
        compiler_params=_params("parallel", "arbitrary"),
        name="merge",
    )(o_a, o_b, w_pa, w_pb, gates, gates)


def _out_ln_kernel(m_ref, w_ref, x_ref, g_ref, b_ref, y_ref, yb_ref, *, rows):
    g, b = g_ref[...], b_ref[...]
    for r in range(m_ref.shape[0] // rows):
        rs = slice(r * rows, (r + 1) * rows)
        z = ALPHA * x_ref[rs, :] + _dot(m_ref[rs, :], w_ref[...])
        y = _layer_norm(z, g, b)
        y_ref[rs, :] = y
        yb_ref[rs, :] = y.astype(yb_ref.dtype)


def _out_ln(merged, w_out, layer, x, g, b, tm=512, rows=256):
    t = x.shape[0]
    row = pl.BlockSpec((tm, D_MODEL), lambda i: (i, 0))
    vec = pl.BlockSpec((1, D_MODEL), lambda i: (0, 0))
    return pl.pallas_call(
        functools.partial(_out_ln_kernel, rows=rows),
        grid=(t // tm,),
        in_specs=[row, pl.BlockSpec((None, D_MODEL, D_MODEL), lambda i: (layer, 0, 0)), row, vec, vec],
        out_specs=[row, row],
        out_shape=[jax.ShapeDtypeStruct((t, D_MODEL), F32), jax.ShapeDtypeStruct((t, D_MODEL), BF16)],
        compiler_params=_params("parallel"),
        name="out_ln",
    )(merged, w_out, x, g, b)


def _ffn_up_kernel(xp_ref, x_ref, wg_ref, wv_ref, cwg_ref, cwv_ref, cbg_ref, cbv_ref, o_ref,
                   u_ref, *, tm, tn, tiles_per_seq):
    i = pl.program_id(0)
    halo = BF16_ROWS
    xp = jnp.where(i % tiles_per_seq == 0, jnp.zeros((), BF16), xp_ref[...])
    xe = jnp.concatenate([xp, x_ref[...]], axis=0)

    def conv(u_ref, w_ref, cw_ref, cb_ref, cs):
        u_ref[...] = _dot(xe, w_ref[:, cs])
        out = cb_ref[:, cs]
        for tap in range(CONV_WIDTH):
            lo = halo - (CONV_WIDTH - 1) + tap
            out = out + cw_ref[tap:tap + 1, cs] * u_ref[lo:lo + tm, :]
        return out

    for c in range(tn // MXU_COLS):
        cs = slice(c * MXU_COLS, (c + 1) * MXU_COLS)
        gate = conv(u_ref.at[2 * c], wg_ref, cwg_ref, cbg_ref, cs)
        val = conv(u_ref.at[2 * c + 1], wv_ref, cwv_ref, cbv_ref, cs)
        o_ref[:, cs] = (jax.nn.silu(gate) * val).astype(o_ref.dtype)


def _ffn_up(xb, w_up, layer, conv_w, conv_b, seq, tm=1024, tn=512):
    t = xb.shape[0]
    nf = D_FF // tn
    halo = BF16_ROWS
    return pl.pallas_call(
        functools.partial(_ffn_up_kernel, tm=tm, tn=tn, tiles_per_seq=seq // tm),
        grid=(t // tm, nf),
        in_specs=[
            pl.BlockSpec((halo, D_MODEL), lambda i, j: (jnp.maximum(i * (tm // halo) - 1, 0), 0)),
            pl.BlockSpec((tm, D_MODEL), lambda i, j: (i, 0)),
            pl.BlockSpec((None, D_MODEL, tn), lambda i, j: (layer, 0, j)),
            pl.BlockSpec((None, D_MODEL, tn), lambda i, j: (layer, 0, nf + j)),
            pl.BlockSpec((None, CONV_WIDTH, tn), lambda i, j: (layer, 0, j)),
            pl.BlockSpec((None, CONV_WIDTH, tn), lambda i, j: (layer, 0, nf + j)),
            pl.BlockSpec((None, 1, tn), lambda i, j: (layer, 0, j)),
            pl.BlockSpec((None, 1, tn), lambda i, j: (layer, 0, nf + j)),
        ],
        out_specs=pl.BlockSpec((tm, tn), lambda i, j: (i, j)),
        out_shape=jax.ShapeDtypeStruct((t, D_FF), BF16),
        scratch_shapes=[pltpu.VMEM((2 * tn // MXU_COLS, halo + tm, MXU_COLS), F32)],
        compiler_params=_params("parallel", "arbitrary"),
        name="ffn_up",
    )(xb, xb, w_up, w_up, conv_w, conv_w, conv_b, conv_b)


def _down_ln_kernel(a_ref, w_ref, x_ref, g_ref, b_ref, *out_refs):
    z = ALPHA * x_ref[...] + _dot(a_ref[...], w_ref[...])
    y = _layer_norm(z, g_ref[...], b_ref[...])
    out_refs[0][...] = y
    if len(out_refs) > 1:
        out_refs[1][...] = y.astype(BF16)


def _down_ln(act, w_down, layer, x, g, b, with_bf16, tm=256):
    t = x.shape[0]
    row = pl.BlockSpec((tm, D_MODEL), lambda i: (i, 0))
    vec = pl.BlockSpec((1, D_MODEL), lambda i: (0, 0))
    out_specs = [row, row] if with_bf16 else [row]
    out_shape = [jax.ShapeDtypeStruct((t, D_MODEL), F32)]
    if with_bf16:
        out_shape.append(jax.ShapeDtypeStruct((t, D_MODEL), BF16))
    return pl.pallas_call(
        _down_ln_kernel,
        grid=(t // tm,),
        in_specs=[
            pl.BlockSpec((tm, D_FF), lambda i: (i, 0)),
            pl.BlockSpec((None, D_FF, D_MODEL), lambda i: (layer, 0, 0), pipeline_mode=pl.Buffered(1)),
            row, vec, vec,
        ],
        out_specs=out_specs,
        out_shape=out_shape,
        compiler_params=_params("parallel"),
        name="down_ln",
    )(act, w_down, x, g, b)


def kernel(x, w_in, sinks, lambda_q1, lambda_k1, lambda_q2, lambda_k2, subln_g, w_proj_a, w_proj_b,
           w_out, ln1_g, ln1_b, w_up, conv_w, conv_b, w_down, ln2_g, ln2_b):
    batch, seq, _ = x.shape
    t = batch * seq
    tab_a, tab_b = _rope_tables(seq)
    x = x.reshape(t, D_MODEL)
    xb = x.astype(BF16)
    row = lambda v: v.reshape(1, -1).astype(F32)
    cols = lambda lo, n: w_in[:, :, lo:lo + n].astype(BF16)
    w_a, w_qk = cols(0, W_A), cols(W_A, 2 * W_QB)
    w_v, w_g = cols(W_A + 2 * W_QB, W_QB), cols(W_A + 3 * W_QB, 2 * D_MODEL)
    w_pa, w_pb, w_o = w_proj_a.astype(BF16), w_proj_b.astype(BF16), w_out.astype(BF16)
    w_u, w_d = w_up.astype(BF16), w_down.astype(BF16)
    conv_w = conv_w.astype(F32)
    conv_b = conv_b.astype(F32).reshape(DEPTH, 1, 2 * D_FF)
    for l in range(DEPTH):
        qkv_a = _proj_a(xb, w_a, l, tab_a, seq)
        qk_b = _proj_qk_b(xb, w_qk, l, tab_b, seq)
        v_b = _proj_act(xb, w_v, l, False, "proj_v_b")
        gates = _proj_act(xb, w_g, l, True, "proj_gates")
        o_a = _attn_a(qkv_a, sinks[l].astype(F32), batch, seq)
        lam_init = 0.8 - 0.6 * math.exp(-0.3 * l)
        o_b = _attn_b(qk_b, _transposed_values(v_b, batch, seq, 512),
                      row(lambda_q1[l]), row(lambda_k1[l]), row(lambda_q2[l]), row(lambda_k2[l]),
                      subln_g[l].reshape(-1, 1).astype(F32), lam_init, batch, seq)
        merged = _merge(o_a, o_b, w_pa, w_pb, l, gates)
        x, xb = _out_ln(merged, w_o, l, x, row(ln1_g[l]), row(ln1_b[l]))
        act = _ffn_up(xb, w_u, l, conv_w, conv_b, seq)
        last = l == DEPTH - 1
        outs = _down_ln(act, w_d, l, x, row(ln2_g[l]), row(ln2_b[l]), with_bf16=not last)
        x = outs[0]
        if not last:
            xb = outs[1]
    return x.reshape(batch, seq, D_MODEL)
```

```python
import functools
import math

import jax
import jax.numpy as jnp
from jax import lax
from jax.experimental import pallas as pl
from jax.experimental.pallas import tpu as pltpu

D_MODEL = 2048
DEPTH = 2
HEAD_DIM_A = 64
N_Q_A = 32
N_KV_A = 4
GROUP_A = N_Q_A // N_KV_A
WINDOW = 128
HEAD_DIM_B = 128
N_HEADS_B = D_MODEL // (2 * HEAD_DIM_B)
SUBLN_EPS = 1e-5
ROPE_THETA = 10000.0
D_FF = ((8 * D_MODEL // 3 + 255) // 256) * 256
CONV_WIDTH = 3
LN_EPS = 1e-5
ALPHA = (2 * DEPTH) ** 0.25
NEG_INF = -1e30
LOG2_E = math.log2(math.e)

W_QA = N_Q_A * HEAD_DIM_A
W_KA = N_KV_A * HEAD_DIM_A
W_A = W_QA + 2 * W_KA
W_QB = N_HEADS_B * 2 * HEAD_DIM_B

LANES = 128
MXU_COLS = 256
BF16_ROWS = 16
VMEM_LIMIT = 56 * 1024 * 1024

F32 = jnp.float32
BF16 = jnp.bfloat16


def _params(*sem):
    return pltpu.CompilerParams(dimension_semantics=sem, vmem_limit_bytes=VMEM_LIMIT)


def _dot(a, b):
    return jnp.dot(a, b, preferred_element_type=F32)


def _dot_nt(a, b):
    return lax.dot_general(a, b, (((1,), (1,)), ((), ())), preferred_element_type=F32)


def _layer_norm(z, g, b):
    mu = jnp.mean(z, axis=-1, keepdims=True)
    zc = z - mu
    var = jnp.mean(zc * zc, axis=-1, keepdims=True)
    return zc * lax.rsqrt(var + LN_EPS) * g + b


def _rope_tables(seq):
    def cs(dim):
        inv = 1.0 / (ROPE_THETA ** (jnp.arange(0, dim, 2, dtype=F32) / dim))
        ang = jnp.arange(seq, dtype=F32)[:, None] * inv[None, :]
        return jnp.cos(ang), jnp.sin(ang)

    cos_a, sin_a = cs(HEAD_DIM_A)
    cos_b, sin_b = cs(HEAD_DIM_B)
    zeros_a = jnp.zeros_like(sin_a)
    tab_a = jnp.stack([
        jnp.tile(cos_a, (1, 4)),
        jnp.tile(jnp.concatenate([zeros_a, sin_a], axis=1), (1, 2)),
        jnp.tile(jnp.concatenate([-sin_a, zeros_a], axis=1), (1, 2)),
    ])
    tab_b = jnp.stack([
        jnp.tile(cos_b, (1, 2)),
        jnp.concatenate([-sin_b, sin_b], axis=1),
    ])
    return tab_a, tab_b


def _proj_a_kernel(x_ref, w_ref, tab_ref, o_ref):
    acc = _dot(x_ref[...], w_ref[...])
    cos, hi, lo = tab_ref[0], tab_ref[1], tab_ref[2]
    scale = HEAD_DIM_A ** -0.5
    n_rope = (W_QA + W_KA) // LANES
    for g in range(W_A // LANES):
        xg = acc[:, g * LANES:(g + 1) * LANES]
        if g < n_rope:
            xg = xg * cos + pltpu.roll(xg, 32, axis=1) * hi + pltpu.roll(xg, 96, axis=1) * lo
            if g < W_QA // LANES:
                xg = xg * scale
        o_ref[:, g * LANES:(g + 1) * LANES] = xg.astype(o_ref.dtype)


def _proj_a(xb, w_a, layer, tab_a, seq, tm=512):
    t = xb.shape[0]
    return pl.pallas_call(
        _proj_a_kernel,
        grid=(t // tm,),
        in_specs=[
            pl.BlockSpec((tm, D_MODEL), lambda i: (i, 0)),
            pl.BlockSpec((None, D_MODEL, W_A), lambda i: (layer, 0, 0)),
            pl.BlockSpec((3, tm, LANES), lambda i: (0, i % (seq // tm), 0)),
        ],
        out_specs=pl.BlockSpec((tm, W_A), lambda i: (i, 0)),
        out_shape=jax.ShapeDtypeStruct((t, W_A), BF16),
        compiler_params=_params("parallel"),
        name="proj_a",
    )(xb, w_a, tab_a)


def _proj_rope_b_kernel(x_ref, w_ref, tab_ref, o_ref, *, tn, n_scaled):
    j = pl.program_id(1)
    scale = jnp.where(j < n_scaled, LOG2_E * HEAD_DIM_B ** -0.5, 1.0)
    cos = tab_ref[0] * scale
    sgn = tab_ref[1] * scale
    x = x_ref[...]
    for c in range(tn // MXU_COLS):
        acc = _dot(x, w_ref[:, c * MXU_COLS:(c + 1) * MXU_COLS])
        for g in range(MXU_COLS // LANES):
            xg = acc[:, g * LANES:(g + 1) * LANES]
            lo = c * MXU_COLS + g * LANES
            o_ref[:, lo:lo + LANES] = (xg * cos + pltpu.roll(xg, 64, axis=1) * sgn).astype(o_ref.dtype)


def _proj_qk_b(xb, w_qk, layer, tab_b, seq, tm=1024, tn=1024):
    t = xb.shape[0]
    return pl.pallas_call(
        functools.partial(_proj_rope_b_kernel, tn=tn, n_scaled=W_QB // tn),
        grid=(t // tm, 2 * W_QB // tn),
        in_specs=[
            pl.BlockSpec((tm, D_MODEL), lambda i, j: (i, 0)),
            pl.BlockSpec((None, D_MODEL, tn), lambda i, j: (layer, 0, j)),
            pl.BlockSpec((2, tm, LANES), lambda i, j: (0, i % (seq // tm), 0)),
        ],
        out_specs=pl.BlockSpec((tm, tn), lambda i, j: (i, j)),
        out_shape=jax.ShapeDtypeStruct((t, 2 * W_QB), BF16),
        compiler_params=_params("parallel", "arbitrary"),
        name="proj_qk_b",
    )(xb, w_qk, tab_b)


def _proj_act_kernel(x_ref, w_ref, o_ref, *, tn, sigmoid):
    x = x_ref[...]
    for c in range(tn // MXU_COLS):
        cs = slice(c * MXU_COLS, (c + 1) * MXU_COLS)
        acc = _dot(x, w_ref[:, cs])
        if sigmoid:
            acc = jax.nn.sigmoid(acc)
        o_ref[:, cs] = acc.astype(o_ref.dtype)


def _proj_act(xb, w, layer, sigmoid, name, tm=1024, tn=1024):
    t = xb.shape[0]
    n = w.shape[-1]
    return pl.pallas_call(
        functools.partial(_proj_act_kernel, tn=tn, sigmoid=sigmoid),
        grid=(t // tm, n // tn),
        in_specs=[
            pl.BlockSpec((tm, D_MODEL), lambda i, j: (i, 0)),
            pl.BlockSpec((None, D_MODEL, tn), lambda i, j: (layer, 0, j)),
        ],
        out_specs=pl.BlockSpec((tm, tn), lambda i, j: (i, j)),
        out_shape=jax.ShapeDtypeStruct((t, n), BF16),
        compiler_params=_params("parallel", "arbitrary"),
        name=name,
    )(xb, w)


def _attn_a_kernel(sinks_ref, q_ref, kvp_ref, kvc_ref, o_ref):
    n = pl.program_id(1)
    blk = WINDOW
    row = lax.broadcasted_iota(jnp.int32, (2 * blk, 1), 0)
    qi = jnp.where(row < blk, row, row - blk)
    kj = lax.broadcasted_iota(jnp.int32, (1, 2 * blk), 1)
    mask = (kj <= blk + qi) & (kj > qi) & ((kj >= blk) | (n > 0))
    low = lax.broadcasted_iota(jnp.int32, (1, LANES), 1) < HEAD_DIM_A
    first_head = row < blk
    zero = jnp.zeros((), BF16)

    for g in range(N_KV_A):
        ks = slice(g * HEAD_DIM_A, (g + 1) * HEAD_DIM_A)
        vs = slice(W_KA + g * HEAD_DIM_A, W_KA + (g + 1) * HEAD_DIM_A)
        kband = jnp.concatenate([kvp_ref[:, ks], kvc_ref[:, ks]], axis=0)
        vband = jnp.concatenate([kvp_ref[:, vs], kvc_ref[:, vs]], axis=0)
        kdup = jnp.concatenate([kband, kband], axis=1)
        vdup = jnp.concatenate([vband, vband], axis=1)
        for pr in range(GROUP_A // 2):
            h0 = g * GROUP_A + 2 * pr
            cs = slice(h0 * HEAD_DIM_A, (h0 + 2) * HEAD_DIM_A)
            qp = q_ref[:, cs]
            q2 = jnp.concatenate([jnp.where(low, qp, zero), jnp.where(low, zero, qp)], axis=0)
            s = _dot_nt(q2, kdup)
            s = jnp.where(mask, s, NEG_INF)
            sink = jnp.where(first_head, sinks_ref[h0], sinks_ref[h0 + 1])
            m = jnp.maximum(jnp.max(s, axis=-1, keepdims=True), sink)
            p = jnp.exp(s - m)
            denom = jnp.sum(p, axis=-1, keepdims=True) + jnp.exp(sink - m)
            o2 = _dot(p.astype(BF16), vdup) / denom
            o_ref[:, cs] = jnp.where(low, o2[:blk], o2[blk:]).astype(o_ref.dtype)


def _attn_a(qkv_a, sinks, batch, seq):
    blk = WINDOW
    nb = seq // blk
    kv_col = W_QA // (2 * W_KA)
    return pl.pallas_call(
        _attn_a_kernel,
        grid=(batch, nb),
        in_specs=[
            pl.BlockSpec(memory_space=pltpu.SMEM),
            pl.BlockSpec((blk, W_QA), lambda b, n: (b * nb + n, 0)),
            pl.BlockSpec((blk, 2 * W_KA), lambda b, n: (b * nb + jnp.maximum(n - 1, 0), kv_col)),
            pl.BlockSpec((blk, 2 * W_KA), lambda b, n: (b * nb + n, kv_col)),
        ],
        out_specs=pl.BlockSpec((blk, W_QA), lambda b, n: (b * nb + n, 0)),
        out_shape=jax.ShapeDtypeStruct((batch * seq, W_QA), BF16),
        compiler_params=_params("parallel", "parallel"),
        name="attn_a",
    )(sinks, qkv_a, qkv_a, qkv_a)


def _attn_b_kernel(lq1_ref, lk1_ref, lq2_ref, lk2_ref, g_ref, q_ref, k_ref, vt_ref, o_ref,
                   qt_ref, m_ref, l_ref, acc_ref, *, tq, tk, lam_init):
    qi = pl.program_id(2)
    hd = HEAD_DIM_B
    cg = MXU_COLS
    n_groups = tq // cg
    q = q_ref[...].astype(F32)
    for c in range(2):
        qt_ref[c] = q[:, c * hd:(c + 1) * hd].T.astype(BF16)
    m_ref[...] = jnp.full(m_ref.shape, NEG_INF, F32)
    l_ref[...] = jnp.zeros(l_ref.shape, F32)
    acc_ref[...] = jnp.zeros(acc_ref.shape, F32)

    def chain(c, j, k, vt, first_key):
        js = slice(j * cg, (j + 1) * cg)
        s = _dot(k[:, c * hd:(c + 1) * hd], qt_ref[c, :, js])
        if first_key is not None:
            key = lax.broadcasted_iota(jnp.int32, s.shape, 0)
            qry = lax.broadcasted_iota(jnp.int32, s.shape, 1)
            s = jnp.where(key + first_key <= qry, s, NEG_INF)
        m_old = m_ref[c, :, js]
        m_new = jnp.maximum(m_old, jnp.max(s, axis=0, keepdims=True))
        a = jnp.exp2(m_old - m_new)
        p = jnp.exp2(s - m_new)
        l_ref[c, :, js] = a * l_ref[c, :, js] + jnp.sum(p, axis=0, keepdims=True)
        acc_ref[c, :, js] = a * acc_ref[c, :, js] + _dot(vt, p.astype(BF16))
        m_ref[c, :, js] = m_new

    def full_block(kb):
        k = k_ref[pl.ds(pl.multiple_of(kb * tk, tk), tk), :]
        vt = vt_ref[kb]
        for j in range(n_groups):
            for c in range(2):
                chain(c, j, k, vt, None)

    blocks_per_tile = tq // tk

    def body(t, carry):
        for u in range(blocks_per_tile):
            full_block(t * blocks_per_tile + u)
        return carry

    lax.fori_loop(0, qi, body, 0)

    for u in range(blocks_per_tile):
        kb = qi * blocks_per_tile + u
        k = k_ref[pl.ds(pl.multiple_of(kb * tk, tk), tk), :]
        vt = vt_ref[kb]
        for j in range(n_groups):
            first_key = u * tk - j * cg
            n_vis = min(tk, (j + 1) * cg - u * tk)
            if n_vis <= 0:
                continue
            masked = first_key + n_vis - 1 > 0
            for c in range(2):
                chain(c, j, k[:n_vis], vt[:, :n_vis], first_key if masked else None)

    lam = (jnp.exp(jnp.sum(lq1_ref[...] * lk1_ref[...], axis=-1, keepdims=True))
           - jnp.exp(jnp.sum(lq2_ref[...] * lk2_ref[...], axis=-1, keepdims=True))
           + lam_init)
    for j in range(n_groups):
        js = slice(j * cg, (j + 1) * cg)
        o = acc_ref[0, :, js] * (1.0 / l_ref[0, :, js]) - lam * (acc_ref[1, :, js] * (1.0 / l_ref[1, :, js]))
        o = o * lax.rsqrt(jnp.mean(o * o, axis=0, keepdims=True) + SUBLN_EPS)
        o = o * g_ref[...] * (1.0 - lam_init)
        o_ref[js, :] = o.T.astype(o_ref.dtype)


def _attn_b(qk_b, vt, lq1, lk1, lq2, lk2, subln_g, lam_init, batch, seq, tq=1024, tk=512):
    nq = seq // tq
    w = 2 * HEAD_DIM_B
    vec = lambda n: pl.BlockSpec((1, n), lambda b, h, i: (0, 0))
    return pl.pallas_call(
        functools.partial(_attn_b_kernel, tq=tq, tk=tk, lam_init=lam_init),
        grid=(batch, N_HEADS_B, nq),
        in_specs=[
            vec(HEAD_DIM_B), vec(HEAD_DIM_B), vec(HEAD_DIM_B), vec(HEAD_DIM_B),
            pl.BlockSpec((w, 1), lambda b, h, i: (0, 0)),
            pl.BlockSpec((tq, w), lambda b, h, i: (b * nq + i, h)),
            pl.BlockSpec((seq, w), lambda b, h, i: (b, N_HEADS_B + h)),
            pl.BlockSpec((None, None, seq // tk, w, tk), lambda b, h, i: (b, h, 0, 0, 0)),
        ],
        out_specs=pl.BlockSpec((tq, w), lambda b, h, i: (b * nq + i, h)),
        out_shape=jax.ShapeDtypeStruct((batch * seq, W_QB), BF16),
        scratch_shapes=[
            pltpu.VMEM((2, HEAD_DIM_B, tq), BF16),
            pltpu.VMEM((2, 1, tq), F32),
            pltpu.VMEM((2, 1, tq), F32),
            pltpu.VMEM((2, w, tq), F32),
        ],
        compiler_params=_params("parallel", "parallel", "arbitrary"),
        name="attn_b",
    )(lq1, lk1, lq2, lk2, subln_g, qk_b, qk_b, vt)


def _transposed_values(v_b, batch, seq, tk):
    vb = v_b.reshape(batch, seq // tk, tk, N_HEADS_B, 2 * HEAD_DIM_B)
    return vb.transpose(0, 3, 1, 4, 2)


def _merge_kernel(oa_ref, ob_ref, wa_ref, wb_ref, ga_ref, gb_ref, o_ref):
    pa = _dot(oa_ref[...], wa_ref[...])
    pb = _dot(ob_ref[...], wb_ref[...])
    o_ref[...] = (ga_ref[...].astype(F32) * pa + gb_ref[...].astype(F32) * pb).astype(o_ref.dtype)


def _merge(o_a, o_b, w_pa, w_pb, layer, gates, tm=1024, tn=1024):
    t = o_a.shape[0]
    return pl.pallas_call(
        _merge_kernel,
        grid=(t // tm, D_MODEL // tn),
        in_specs=[
            pl.BlockSpec((tm, W_QA), lambda i, j: (i, 0)),
            pl.BlockSpec((tm, W_QB), lambda i, j: (i, 0)),
            pl.BlockSpec((None, W_QA, tn), lambda i, j: (layer, 0, j)),
            pl.BlockSpec((None, W_QB, tn), lambda i, j: (layer, 0, j)),
            pl.BlockSpec((tm, tn), lambda i, j: (i, j)),
            pl.BlockSpec((tm, tn), lambda i, j: (i, D_MODEL // tn + j)),
        ],
        out_specs=pl.BlockSpec((tm, tn), lambda i, j: (i, j)),
        out_shape=jax.ShapeDtypeStruct((t, D_MODEL), BF16),
        compiler_params=_params("parallel", "arbitrary"),
        name="merge",
    )(o_a, o_b, w_pa, w_pb, gates, gates)


def _out_ln_kernel(m_ref, w_ref, x_ref, g_ref, b_ref, y_ref, yb_ref, *, rows):
    g, b = g_ref[...], b_ref[...]
    for r in range(m_ref.shape[0] // rows):
        rs = slice(r * rows, (r + 1) * rows)
        z = ALPHA * x_ref[rs, :] + _dot(m_ref[rs, :], w_ref[...])
        y = _layer_norm(z, g, b)
        y_ref[rs, :] = y
        yb_ref[rs, :] = y.astype(yb_ref.dtype)


def _out_ln(merged, w_out, layer, x, g, b, tm=512, rows=256):
    t = x.shape[0]
    row = pl.BlockSpec((tm, D_MODEL), lambda i: (i, 0))
    vec = pl.BlockSpec((1, D_MODEL), lambda i: (0, 0))
    return pl.pallas_call(
        functools.partial(_out_ln_kernel, rows=rows),
        grid=(t // tm,),
        in_specs=[row, pl.BlockSpec((None, D_MODEL, D_MODEL), lambda i: (layer, 0, 0)), row, vec, vec],
        out_specs=[row, row],
        out_shape=[jax.ShapeDtypeStruct((t, D_MODEL), F32), jax.ShapeDtypeStruct((t, D_MODEL), BF16)],
        compiler_params=_params("parallel"),
        name="out_ln",
    )(merged, w_out, x, g, b)


def _ffn_up_kernel(xp_ref, x_ref, wg_ref, wv_ref, cwg_ref, cwv_ref, cbg_ref, cbv_ref, o_ref,
                   u_ref, *, tm, tn, tiles_per_seq):
    i = pl.program_id(0)
    halo = BF16_ROWS
    xp = jnp.where(i % tiles_per_seq == 0, jnp.zeros((), BF16), xp_ref[...])
    xe = jnp.concatenate([xp, x_ref[...]], axis=0)

    def conv(u_ref, w_ref, cw_ref, cb_ref, cs):
        u_ref[...] = _dot(xe, w_ref[:, cs])
        out = cb_ref[:, cs]
        for tap in range(CONV_WIDTH):
            lo = halo - (CONV_WIDTH - 1) + tap
            out = out + cw_ref[tap:tap + 1, cs] * u_ref[lo:lo + tm, :]
        return out

    for c in range(tn // MXU_COLS):
        cs = slice(c * MXU_COLS, (c + 1) * MXU_COLS)
        gate = conv(u_ref.at[2 * c], wg_ref, cwg_ref, cbg_ref, cs)
        val = conv(u_ref.at[2 * c + 1], wv_ref, cwv_ref, cbv_ref, cs)
        o_ref[:, cs] = (jax.nn.silu(gate) * val).astype(o_ref.dtype)


def _ffn_up(xb, w_up, layer, conv_w, conv_b, seq, tm=1024, tn=512):
    t = xb.shape[0]
    nf = D_FF // tn
    halo = BF16_ROWS
    return pl.pallas_call(
        functools.partial(_ffn_up_kernel, tm=tm, tn=tn, tiles_per_seq=seq // tm),
        grid=(t // tm, nf),
        in_specs=[
            pl.BlockSpec((halo, D_MODEL), lambda i, j: (jnp.maximum(i * (tm // halo) - 1, 0), 0)),
            pl.BlockSpec((tm, D_MODEL), lambda i, j: (i, 0)),
            pl.BlockSpec((None, D_MODEL, tn), lambda i, j: (layer, 0, j)),
            pl.BlockSpec((None, D_MODEL, tn), lambda i, j: (layer, 0, nf + j)),
            pl.BlockSpec((None, CONV_WIDTH, tn), lambda i, j: (layer, 0, j)),
            pl.BlockSpec((None, CONV_WIDTH, tn), lambda i, j: (layer, 0, nf + j)),
            pl.BlockSpec((None, 1, tn), lambda i, j: (layer, 0, j)),
            pl.BlockSpec((None, 1, tn), lambda i, j: (layer, 0, nf + j)),
        ],
        out_specs=pl.BlockSpec((tm, tn), lambda i, j: (i, j)),
        out_shape=jax.ShapeDtypeStruct((t, D_FF), BF16),
        scratch_shapes=[pltpu.VMEM((2 * tn // MXU_COLS, halo + tm, MXU_COLS), F32)],
        compiler_params=_params("parallel", "arbitrary"),
        name="ffn_up",
    )(xb, xb, w_up, w_up, conv_w, conv_w, conv_b, conv_b)


def _down_ln_kernel(a_ref, w_ref, x_ref, g_ref, b_ref, *out_refs):
    z = ALPHA * x_ref[...] + _dot(a_ref[...], w_ref[...])
    y = _layer_norm(z, g_ref[...], b_ref[...])
    out_refs[0][...] = y
    if len(out_refs) > 1:
        out_refs[1][...] = y.astype(BF16)


def _down_ln(act, w_down, layer, x, g, b, with_bf16, tm=256):
    t = x.shape[0]
    row = pl.BlockSpec((tm, D_MODEL), lambda i: (i, 0))
    vec = pl.BlockSpec((1, D_MODEL), lambda i: (0, 0))
    out_specs = [row, row] if with_bf16 else [row]
    out_shape = [jax.ShapeDtypeStruct((t, D_MODEL), F32)]
    if with_bf16:
        out_shape.append(jax.ShapeDtypeStruct((t, D_MODEL), BF16))
    return pl.pallas_call(
        _down_ln_kernel,
        grid=(t // tm,),
        in_specs=[
            pl.BlockSpec((tm, D_FF), lambda i: (i, 0)),
            pl.BlockSpec((None, D_FF, D_MODEL), lambda i: (layer, 0, 0), pipeline_mode=pl.Buffered(1)),
            row, vec, vec,
        ],
        out_specs=out_specs,
        out_shape=out_shape,
        compiler_params=_params("parallel"),
        name="down_ln",
    )(act, w_down, x, g, b)


def kernel(x, w_in, sinks, lambda_q1, lambda_k1, lambda_q2, lambda_k2, subln_g, w_proj_a, w_proj_b,
           w_out, ln1_g, ln1_b, w_up, conv_w, conv_b, w_down, ln2_g, ln2_b):
    batch, seq, _ = x.shape
    t = batch * seq
    tab_a, tab_b = _rope_tables(seq)
    x = x.reshape(t, D_MODEL)
    xb = x.astype(BF16)
    row = lambda v: v.reshape(1, -1).astype(F32)
    cols = lambda lo, n: w_in[:, :, lo:lo + n].astype(BF16)
    w_a, w_qk = cols(0, W_A), cols(W_A, 2 * W_QB)
    w_v, w_g = cols(W_A + 2 * W_QB, W_QB), cols(W_A + 3 * W_QB, 2 * D_MODEL)
    w_pa, w_pb, w_o = w_proj_a.astype(BF16), w_proj_b.astype(BF16), w_out.astype(BF16)
    w_u, w_d = w_up.astype(BF16), w_down.astype(BF16)
    conv_w = conv_w.astype(F32)
    conv_b = conv_b.astype(F32).reshape(DEPTH, 1, 2 * D_FF)
    for l in range(DEPTH):
        qkv_a = _proj_a(xb, w_a, l, tab_a, seq)
        qk_b = _proj_qk_b(xb, w_qk, l, tab_b, seq)
        v_b = _proj_act(xb, w_v, l, False, "proj_v_b")
        gates = _proj_act(xb, w_g, l, True, "proj_gates")
        o_a = _attn_a(qkv_a, sinks[l].astype(F32), batch, seq)
        lam_init = 0.8 - 0.6 * math.exp(-0.3 * l)
        o_b = _attn_b(qk_b, _transposed_values(v_b, batch, seq, 512),
                      row(lambda_q1[l]), row(lambda_k1[l]), row(lambda_q2[l]), row(lambda_k2[l]),
                      subln_g[l].reshape(-1, 1).astype(F32), lam_init, batch, seq)
        merged = _merge(o_a, o_b, w_pa, w_pb, l, gates)
        x, xb = _out_ln(merged, w_o, l, x, row(ln1_g[l]), row(ln1_b[l]))
        act = _ffn_up(xb, w_u, l, conv_w, conv_b, seq)
        last = l == DEPTH - 1
        outs = _down_ln(act, w_d, l, x, row(ln2_g[l]), row(ln2_b[l]), with_bf16=not last)
        x = outs[0]
        if not last:
            xb = outs[1]
    return x.reshape(batch, seq, D_MODEL)
```

```python
import functools
import math

import jax
import jax.numpy as jnp
from jax import lax
from jax.experimental import pallas as pl
from jax.experimental.pallas import tpu as pltpu

D_MODEL = 2048
DEPTH = 2
HEAD_DIM_A = 64
N_Q_A = 32
N_KV_A = 4
GROUP_A = N_Q_A // N_KV_A
WINDOW = 128
HEAD_DIM_B = 128
N_HEADS_B = D_MODEL // (2 * HEAD_DIM_B)
SUBLN_EPS = 1e-5
ROPE_THETA = 10000.0
D_FF = ((8 * D_MODEL // 3 + 255) // 256) * 256
CONV_WIDTH = 3
LN_EPS = 1e-5
ALPHA = (2 * DEPTH) ** 0.25
NEG_INF = -1e30
LOG2_E = math.log2(math.e)

W_QA = N_Q_A * HEAD_DIM_A
W_KA = N_KV_A * HEAD_DIM_A
W_A = W_QA + 2 * W_KA
W_QB = N_HEADS_B * 2 * HEAD_DIM_B

LANES = 128
MXU_COLS = 256
BF16_ROWS = 16
VMEM_LIMIT = 56 * 1024 * 1024
ATTN_B_TK = 512

F32 = jnp.float32
BF16 = jnp.bfloat16


def _params(*sem):
    return pltpu.CompilerParams(dimension_semantics=sem, vmem_limit_bytes=VMEM_LIMIT)


def _dot(a, b):
    return jnp.dot(a, b, preferred_element_type=F32)


def _dot_nt(a, b):
    return lax.dot_general(a, b, (((1,), (1,)), ((), ())), preferred_element_type=F32)


def _layer_norm(z, g, b):
    mu = jnp.mean(z, axis=-1, keepdims=True)
    zc = z - mu
    var = jnp.mean(zc * zc, axis=-1, keepdims=True)
    return zc * lax.rsqrt(var + LN_EPS) * g + b


def _rope_tables(seq):
    def cs(dim):
        inv = 1.0 / (ROPE_THETA ** (jnp.arange(0, dim, 2, dtype=F32) / dim))
        ang = jnp.arange(seq, dtype=F32)[:, None] * inv[None, :]
        return jnp.cos(ang), jnp.sin(ang)

    cos_a, sin_a = cs(HEAD_DIM_A)
    cos_b, sin_b = cs(HEAD_DIM_B)
    zeros_a = jnp.zeros_like(sin_a)
    tab_a = jnp.stack([
        jnp.tile(cos_a, (1, 4)),
        jnp.tile(jnp.concatenate([zeros_a, sin_a], axis=1), (1, 2)),
        jnp.tile(jnp.concatenate([-sin_a, zeros_a], axis=1), (1, 2)),
    ])
    tab_b = jnp.stack([
        jnp.tile(cos_b, (1, 2)),
        jnp.concatenate([-sin_b, sin_b], axis=1),
    ])
    return tab_a, tab_b


def _proj_a_kernel(x_ref, w_ref, tab_ref, o_ref):
    acc = _dot(x_ref[...], w_ref[...])
    cos, hi, lo = tab_ref[0], tab_ref[1], tab_ref[2]
    scale = LOG2_E * HEAD_DIM_A ** -0.5
    n_rope = (W_QA + W_KA) // LANES
    for g in range(W_A // LANES):
        xg = acc[:, g * LANES:(g + 1) * LANES]
        if g < n_rope:
            xg = xg * cos + pltpu.roll(xg, 32, axis=1) * hi + pltpu.roll(xg, 96, axis=1) * lo
            if g < W_QA // LANES:
                xg = xg * scale
        o_ref[:, g * LANES:(g + 1) * LANES] = xg.astype(o_ref.dtype)


def _proj_a(xb, w_a, layer, tab_a, seq, tm=512):
    t = xb.shape[0]
    return pl.pallas_call(
        _proj_a_kernel,
        grid=(t // tm,),
        in_specs=[
            pl.BlockSpec((tm, D_MODEL), lambda i: (i, 0)),
            pl.BlockSpec((None, D_MODEL, W_A), lambda i: (layer, 0, 0)),
            pl.BlockSpec((3, tm, LANES), lambda i: (0, i % (seq // tm), 0)),
        ],
        out_specs=pl.BlockSpec((tm, W_A), lambda i: (i, 0)),
        out_shape=jax.ShapeDtypeStruct((t, W_A), BF16),
        compiler_params=_params("parallel"),
        name="proj_a",
    )(xb, w_a, tab_a)


def _proj_rope_b_kernel(x_ref, w_ref, tab_ref, o_ref, *, tn, n_scaled):
    j = pl.program_id(1)
    scale = jnp.where(j < n_scaled, LOG2_E * HEAD_DIM_B ** -0.5, 1.0)
    cos = tab_ref[0] * scale
    sgn = tab_ref[1] * scale
    x = x_ref[...]
    for c in range(tn // MXU_COLS):
        acc = _dot(x, w_ref[:, c * MXU_COLS:(c + 1) * MXU_COLS])
        for g in range(MXU_COLS // LANES):
            xg = acc[:, g * LANES:(g + 1) * LANES]
            lo = c * MXU_COLS + g * LANES
            o_ref[:, lo:lo + LANES] = (xg * cos + pltpu.roll(xg, 64, axis=1) * sgn).astype(o_ref.dtype)


def _proj_qk_b(xb, w_qk, layer, tab_b, seq, tm=1024, tn=1024):
    t = xb.shape[0]
    return pl.pallas_call(
        functools.partial(_proj_rope_b_kernel, tn=tn, n_scaled=W_QB // tn),
        grid=(t // tm, 2 * W_QB // tn),
        in_specs=[
            pl.BlockSpec((tm, D_MODEL), lambda i, j: (i, 0)),
            pl.BlockSpec((None, D_MODEL, tn), lambda i, j: (layer, 0, j)),
            pl.BlockSpec((2, tm, LANES), lambda i, j: (0, i % (seq // tm), 0)),
        ],
        out_specs=pl.BlockSpec((tm, tn), lambda i, j: (i, j)),
        out_shape=jax.ShapeDtypeStruct((t, 2 * W_QB), BF16),
        compiler_params=_params("parallel", "arbitrary"),
        name="proj_qk_b",
    )(xb, w_qk, tab_b)


def _proj_gates_kernel(x_ref, w_ref, o_ref, *, tn):
    x = x_ref[...]
    for c in range(tn // MXU_COLS):
        cs = slice(c * MXU_COLS, (c + 1) * MXU_COLS)
        o_ref[:, cs] = jax.nn.sigmoid(_dot(x, w_ref[:, cs])).astype(o_ref.dtype)


def _proj_gates(xb, w_g, layer, tm=1024, tn=1024):
    t = xb.shape[0]
    n = w_g.shape[-1]
    return pl.pallas_call(
        functools.partial(_proj_gates_kernel, tn=tn),
        grid=(t // tm, n // tn),
        in_specs=[
            pl.BlockSpec((tm, D_MODEL), lambda i, j: (i, 0)),
            pl.BlockSpec((None, D_MODEL, tn), lambda i, j: (layer, 0, j)),
        ],
        out_specs=pl.BlockSpec((tm, tn), lambda i, j: (i, j)),
        out_shape=jax.ShapeDtypeStruct((t, n), BF16),
        compiler_params=_params("parallel", "arbitrary"),
        name="proj_gates",
    )(xb, w_g)


def _proj_vt_b_kernel(x_ref, w_ref, o_ref, *, tk):
    x = x_ref[...]
    w = 2 * HEAD_DIM_B
    for c in range(o_ref.shape[0]):
        acc = _dot(x, w_ref[:, c * w:(c + 1) * w]).astype(o_ref.dtype)
        for kk in range(o_ref.shape[1]):
            o_ref[c, kk] = acc[kk * tk:(kk + 1) * tk].T


def _proj_vt_b(xb, w_v, layer, batch, seq, tk, tm=1024, heads=4):
    w = 2 * HEAD_DIM_B
    tiles_per_seq = seq // tm
    return pl.pallas_call(
        functools.partial(_proj_vt_b_kernel, tk=tk),
        grid=(batch * tiles_per_seq, N_HEADS_B // heads),
        in_specs=[
            pl.BlockSpec((tm, D_MODEL), lambda i, j: (i, 0)),
            pl.BlockSpec((None, D_MODEL, heads * w), lambda i, j: (layer, 0, j)),
        ],
        out_specs=pl.BlockSpec((None, heads, tm // tk, w, tk),
                               lambda i, j: (i // tiles_per_seq, j, i % tiles_per_seq, 0, 0)),
        out_shape=jax.ShapeDtypeStruct((batch, N_HEADS_B, seq // tk, w, tk), BF16),
        compiler_params=_params("parallel", "arbitrary"),
        name="proj_vt_b",
    )(xb, w_v)


def _attn_a_kernel(sinks_ref, q_ref, kvp_ref, kvc_ref, o_ref):
    n = pl.program_id(1)
    blk = WINDOW
    hd = HEAD_DIM_A
    key = lax.broadcasted_iota(jnp.int32, (2 * blk, blk), 0)
    qry = lax.broadcasted_iota(jnp.int32, (2 * blk, blk), 1)
    mask = (key <= blk + qry) & (key > qry) & ((key >= blk) | (n > 0))
    vpt = kvp_ref[:, W_KA:].T
    vct = kvc_ref[:, W_KA:].T

    for g in range(N_KV_A):
        ks = slice(g * hd, (g + 1) * hd)
        kband = jnp.concatenate([kvp_ref[:, ks], kvc_ref[:, ks]], axis=0)
        vt = jnp.concatenate([vpt[ks], vct[ks]], axis=1)
        qts = []
        for pr in range(GROUP_A // 2):
            h0 = g * GROUP_A + 2 * pr
            t = q_ref[:, h0 * hd:(h0 + 2) * hd].T
            qts += [t[:hd], t[hd:]]
        qt = jnp.concatenate(qts, axis=1)
        s = _dot(kband, qt)
        ps, rs = [], []
        for hh in range(GROUP_A):
            sink = sinks_ref[g * GROUP_A + hh] * LOG2_E
            sh = jnp.where(mask, s[:, hh * blk:(hh + 1) * blk], NEG_INF)
            m = jnp.maximum(jnp.max(sh, axis=0, keepdims=True), sink)
            p = jnp.exp2(sh - m)
            denom = jnp.sum(p, axis=0, keepdims=True) + jnp.exp2(sink - m)
            ps.append(p.astype(BF16))
            rs.append(1.0 / denom)
        ot = _dot(vt, jnp.concatenate(ps, axis=1))
        for pr in range(GROUP_A // 2):
            h0 = g * GROUP_A + 2 * pr
            pair = jnp.concatenate([ot[:, (2 * pr + e) * blk:(2 * pr + e + 1) * blk] * rs[2 * pr + e]
                                    for e in range(2)], axis=0)
            o_ref[:, h0 * hd:(h0 + 2) * hd] = pair.astype(o_ref.dtype).T


def _attn_a(qkv_a, sinks, batch, seq):
    blk = WINDOW
    nb = seq // blk
    kv_col = W_QA // (2 * W_KA)
    return pl.pallas_call(
        _attn_a_kernel,
        grid=(batch, nb),
        in_specs=[
            pl.BlockSpec(memory_space=pltpu.SMEM),
            pl.BlockSpec((blk, W_QA), lambda b, n: (b * nb + n, 0)),
            pl.BlockSpec((blk, 2 * W_KA), lambda b, n: (b * nb + jnp.maximum(n - 1, 0), kv_col)),
            pl.BlockSpec((blk, 2 * W_KA), lambda b, n: (b * nb + n, kv_col)),
        ],
        out_specs=pl.BlockSpec((blk, W_QA), lambda b, n: (b * nb + n, 0)),
        out_shape=jax.ShapeDtypeStruct((batch * seq, W_QA), BF16),
        compiler_params=_params("parallel", "parallel"),
        name="attn_a",
    )(sinks, qkv_a, qkv_a, qkv_a)


def _attn_b_kernel(lq1_ref, lk1_ref, lq2_ref, lk2_ref, g_ref, q_ref, k_ref, vt_ref, o_ref,
                   qt_ref, m_ref, l_ref, acc_ref, *, tq, tk, lam_init):
    qi = pl.program_id(2)
    hd = HEAD_DIM_B
    cg = MXU_COLS
    n_groups = tq // cg
    q = q_ref[...].astype(F32)
    for c in range(2):
        qt_ref[c] = q[:, c * hd:(c + 1) * hd].T.astype(BF16)
    m_ref[...] = jnp.full(m_ref.shape, NEG_INF, F32)
    l_ref[...] = jnp.zeros(l_ref.shape, F32)
    acc_ref[...] = jnp.zeros(acc_ref.shape, F32)

    def chain(c, j, k, vt, first_key):
        js = slice(j * cg, (j + 1) * cg)
        s = _dot(k[:, c * hd:(c + 1) * hd], qt_ref[c, :, js])
        if first_key is not None:
            key = lax.broadcasted_iota(jnp.int32, s.shape, 0)
            qry = lax.broadcasted_iota(jnp.int32, s.shape, 1)
            s = jnp.where(key + first_key <= qry, s, NEG_INF)
        m_old = m_ref[c, :, js]
        m_new = jnp.maximum(m_old, jnp.max(s, axis=0, keepdims=True))
        a = jnp.exp2(m_old - m_new)
        p = jnp.exp2(s - m_new)
        l_ref[c, :, js] = a * l_ref[c, :, js] + jnp.sum(p, axis=0, keepdims=True)
        acc_ref[c, :, js] = a * acc_ref[c, :, js] + _dot(vt, p.astype(BF16))
        m_ref[c, :, js] = m_new

    def full_block(kb):
        k = k_ref[pl.ds(pl.multiple_of(kb * tk, tk), tk), :]
        vt = vt_ref[kb]
        for j in range(n_groups):
            for c in range(2):
                chain(c, j, k, vt, None)

    blocks_per_tile = tq // tk

    def body(t, carry):
        for u in range(blocks_per_tile):
            full_block(t * blocks_per_tile + u)
        return carry

    lax.fori_loop(0, qi, body, 0)

    for u in range(blocks_per_tile):
        kb = qi * blocks_per_tile + u
        k = k_ref[pl.ds(pl.multiple_of(kb * tk, tk), tk), :]
        vt = vt_ref[kb]
        for j in range(n_groups):
            first_key = u * tk - j * cg
            n_vis = min(tk, (j + 1) * cg - u * tk)
            if n_vis <= 0:
                continue
            masked = first_key + n_vis - 1 > 0
            for c in range(2):
                chain(c, j, k[:n_vis], vt[:, :n_vis], first_key if masked else None)

    lam = (jnp.exp(jnp.sum(lq1_ref[...] * lk1_ref[...], axis=-1, keepdims=True))
           - jnp.exp(jnp.sum(lq2_ref[...] * lk2_ref[...], axis=-1, keepdims=True))
           + lam_init)
    for j in range(n_groups):
        js = slice(j * cg, (j + 1) * cg)
        o = acc_ref[0, :, js] * (1.0 / l_ref[0, :, js]) - lam * (acc_ref[1, :, js] * (1.0 / l_ref[1, :, js]))
        o = o * lax.rsqrt(jnp.mean(o * o, axis=0, keepdims=True) + SUBLN_EPS)
        o = o * g_ref[...] * (1.0 - lam_init)
        o_ref[js, :] = o.T.astype(o_ref.dtype)


def _attn_b(qk_b, vt, lq1, lk1, lq2, lk2, subln_g, lam_init, batch, seq, tq=1024, tk=512):
    nq = seq // tq
    w = 2 * HEAD_DIM_B
    vec = lambda n: pl.BlockSpec((1, n), lambda b, h, i: (0, 0))
    return pl.pallas_call(
        functools.partial(_attn_b_kernel, tq=tq, tk=tk, lam_init=lam_init),
        grid=(batch, N_HEADS_B, nq),
        in_specs=[
            vec(HEAD_DIM_B), vec(HEAD_DIM_B), vec(HEAD_DIM_B), vec(HEAD_DIM_B),
            pl.BlockSpec((w, 1), lambda b, h, i: (0, 0)),
            pl.BlockSpec((tq, w), lambda b, h, i: (b * nq + i, h)),
            pl.BlockSpec((seq, w), lambda b, h, i: (b, N_HEADS_B + h)),
            pl.BlockSpec((None, None, seq // tk, w, tk), lambda b, h, i: (b, h, 0, 0, 0)),
        ],
        out_specs=pl.BlockSpec((tq, w), lambda b, h, i: (b * nq + i, h)),
        out_shape=jax.ShapeDtypeStruct((batch * seq, W_QB), BF16),
        scratch_shapes=[
            pltpu.VMEM((2, HEAD_DIM_B, tq), BF16),
            pltpu.VMEM((2, 1, tq), F32),
            pltpu.VMEM((2, 1, tq), F32),
            pltpu.VMEM((2, w, tq), F32),
        ],
        compiler_params=_params("parallel", "parallel", "arbitrary"),
        name="attn_b",
    )(lq1, lk1, lq2, lk2, subln_g, qk_b, qk_b, vt)


def _merge_kernel(oa_ref, ob_ref, wa_ref, wb_ref, ga_ref, gb_ref, o_ref):
    pa = _dot(oa_ref[...], wa_ref[...])
    pb = _dot(ob_ref[...], wb_ref[...])
    o_ref[...] = (ga_ref[...].astype(F32) * pa + gb_ref[...].astype(F32) * pb).astype(o_ref.dtype)


def _merge(o_a, o_b, w_pa, w_pb, layer, gates, tm=1024, tn=1024):
    t = o_a.shape[0]
    return pl.pallas_call(
        _merge_kernel,
        grid=(t // tm, D_MODEL // tn),
        in_specs=[
            pl.BlockSpec((tm, W_QA), lambda i, j: (i, 0)),
            pl.BlockSpec((tm, W_QB), lambda i, j: (i, 0)),
            pl.BlockSpec((None, W_QA, tn), lambda i, j: (layer, 0, j)),
            pl.BlockSpec((None, W_QB, tn), lambda i, j: (layer, 0, j)),
            pl.BlockSpec((tm, tn), lambda i, j: (i, j)),
            pl.BlockSpec((tm, tn), lambda i, j: (i, D_MODEL // tn + j)),
        ],
        out_specs=pl.BlockSpec((tm, tn), lambda i, j: (i, j)),
        out_shape=jax.ShapeDtypeStruct((t, D_MODEL), BF16),
        compiler_params=_params("parallel", "arbitrary"),
        name="merge",
    )(o_a, o_b, w_pa, w_pb, gates, gates)


def _out_ln_kernel(m_ref, w_ref, x_ref, g_ref, b_ref, y_ref, yb_ref, *, rows):
    g, b = g_ref[...], b_ref[...]
    for r in range(m_ref.shape[0] // rows):
        rs = slice(r * rows, (r + 1) * rows)
        z = ALPHA * x_ref[rs, :] + _dot(m_ref[rs, :], w_ref[...])
        y = _layer_norm(z, g, b)
        y_ref[rs, :] = y
        yb_ref[rs, :] = y.astype(yb_ref.dtype)


def _out_ln(merged, w_out, layer, x, g, b, tm=512, rows=256):
    t = x.shape[0]
    row = pl.BlockSpec((tm, D_MODEL), lambda i: (i, 0))
    vec = pl.BlockSpec((1, D_MODEL), lambda i: (0, 0))
    return pl.pallas_call(
        functools.partial(_out_ln_kernel, rows=rows),
        grid=(t // tm,),
        in_specs=[row, pl.BlockSpec((None, D_MODEL, D_MODEL), lambda i: (layer, 0, 0)), row, vec, vec],
        out_specs=[row, row],
        out_shape=[jax.ShapeDtypeStruct((t, D_MODEL), F32), jax.ShapeDtypeStruct((t, D_MODEL), BF16)],
        compiler_params=_params("parallel"),
        name="out_ln",
    )(merged, w_out, x, g, b)


def _ffn_up_kernel(xp_ref, x_ref, wg_ref, wv_ref, cwg_ref, cwv_ref, cbg_ref, cbv_ref, o_ref,
                   u_ref, *, tm, tn, tiles_per_seq):
    i = pl.program_id(0)
    halo = BF16_ROWS
    xp = jnp.where(i % tiles_per_seq == 0, jnp.zeros((), BF16), xp_ref[...])
    xe = jnp.concatenate([xp, x_ref[...]], axis=0)

    def conv(u_ref, w_ref, cw_ref, cb_ref, cs):
        u_ref[...] = _dot(xe, w_ref[:, cs])
        out = cb_ref[:, cs]
        for tap in range(CONV_WIDTH):
            lo = halo - (CONV_WIDTH - 1) + tap
            out = out + cw_ref[tap:tap + 1, cs] * u_ref[lo:lo + tm, :]
        return out

    for c in range(tn // MXU_COLS):
        cs = slice(c * MXU_COLS, (c + 1) * MXU_COLS)
        gate = conv(u_ref.at[2 * c], wg_ref, cwg_ref, cbg_ref, cs)
        val = conv(u_ref.at[2 * c + 1], wv_ref, cwv_ref, cbv_ref, cs)
        o_ref[:, cs] = (jax.nn.silu(gate) * val).astype(o_ref.dtype)


def _ffn_up(xb, w_up, layer, conv_w, conv_b, seq, tm=1024, tn=512):
    t = xb.shape[0]
    nf = D_FF // tn
    halo = BF16_ROWS
    return pl.pallas_call(
        functools.partial(_ffn_up_kernel, tm=tm, tn=tn, tiles_per_seq=seq // tm),
        grid=(t // tm, nf),
        in_specs=[
            pl.BlockSpec((halo, D_MODEL), lambda i, j: (jnp.maximum(i * (tm // halo) - 1, 0), 0)),
            pl.BlockSpec((tm, D_MODEL), lambda i, j: (i, 0)),
            pl.BlockSpec((None, D_MODEL, tn), lambda i, j: (layer, 0, j)),
            pl.BlockSpec((None, D_MODEL, tn), lambda i, j: (layer, 0, nf + j)),
            pl.BlockSpec((None, CONV_WIDTH, tn), lambda i, j: (layer, 0, j)),
            pl.BlockSpec((None, CONV_WIDTH, tn), lambda i, j: (layer, 0, nf + j)),
            pl.BlockSpec((None, 1, tn), lambda i, j: (layer, 0, j)),
            pl.BlockSpec((None, 1, tn), lambda i, j: (layer, 0, nf + j)),
        ],
        out_specs=pl.BlockSpec((tm, tn), lambda i, j: (i, j)),
        out_shape=jax.ShapeDtypeStruct((t, D_FF), BF16),
        scratch_shapes=[pltpu.VMEM((2 * tn // MXU_COLS, halo + tm, MXU_COLS), F32)],
        compiler_params=_params("parallel", "arbitrary"),
        name="ffn_up",
    )(xb, xb, w_up, w_up, conv_w, conv_w, conv_b, conv_b)


def _down_ln_kernel(a_ref, w_ref, x_ref, g_ref, b_ref, *out_refs):
    z = ALPHA * x_ref[...] + _dot(a_ref[...], w_ref[...])
    y = _layer_norm(z, g_ref[...], b_ref[...])
    out_refs[0][...] = y
    if len(out_refs) > 1:
        out_refs[1][...] = y.astype(BF16)


def _down_ln(act, w_down, layer, x, g, b, with_bf16, tm=256):
    t = x.shape[0]
    row = pl.BlockSpec((tm, D_MODEL), lambda i: (i, 0))
    vec = pl.BlockSpec((1, D_MODEL), lambda i: (0, 0))
    out_specs = [row, row] if with_bf16 else [row]
    out_shape = [jax.ShapeDtypeStruct((t, D_MODEL), F32)]
    if with_bf16:
        out_shape.append(jax.ShapeDtypeStruct((t, D_MODEL), BF16))
    return pl.pallas_call(
        _down_ln_kernel,
        grid=(t // tm,),
        in_specs=[
            pl.BlockSpec((tm, D_FF), lambda i: (i, 0)),
            pl.BlockSpec((None, D_FF, D_MODEL), lambda i: (layer, 0, 0), pipeline_mode=pl.Buffered(1)),
            row, vec, vec,
        ],
        out_specs=out_specs,
        out_shape=out_shape,
        compiler_params=_params("parallel"),
        name="down_ln",
    )(act, w_down, x, g, b)


def kernel(x, w_in, sinks, lambda_q1, lambda_k1, lambda_q2, lambda_k2, subln_g, w_proj_a, w_proj_b,
           w_out, ln1_g, ln1_b, w_up, conv_w, conv_b, w_down, ln2_g, ln2_b):
    batch, seq, _ = x.shape
    t = batch * seq
    tab_a, tab_b = _rope_tables(seq)
    x = x.reshape(t, D_MODEL)
    xb = x.astype(BF16)
    row = lambda v: v.reshape(1, -1).astype(F32)
    cols = lambda lo, n: w_in[:, :, lo:lo + n].astype(BF16)
    w_a, w_qk = cols(0, W_A), cols(W_A, 2 * W_QB)
    w_v, w_g = cols(W_A + 2 * W_QB, W_QB), cols(W_A + 3 * W_QB, 2 * D_MODEL)
    w_pa, w_pb, w_o = w_proj_a.astype(BF16), w_proj_b.astype(BF16), w_out.astype(BF16)
    w_u, w_d = w_up.astype(BF16), w_down.astype(BF16)
    conv_w = conv_w.astype(F32)
    conv_b = conv_b.astype(F32).reshape(DEPTH, 1, 2 * D_FF)
    for l in range(DEPTH):
        qkv_a = _proj_a(xb, w_a, l, tab_a, seq)
        qk_b = _proj_qk_b(xb, w_qk, l, tab_b, seq)
        vt_b = _proj_vt_b(xb, w_v, l, batch, seq, ATTN_B_TK)
        gates = _proj_gates(xb, w_g, l)
        o_a = _attn_a(qkv_a, sinks[l].astype(F32), batch, seq)
        lam_init = 0.8 - 0.6 * math.exp(-0.3 * l)
        o_b = _attn_b(qk_b, vt_b, row(lambda_q1[l]), row(lambda_k1[l]), row(lambda_q2[l]), row(lambda_k2[l]),
                      subln_g[l].reshape(-1, 1).astype(F32), lam_init, batch, seq, tk=ATTN_B_TK)
        merged = _merge(o_a, o_b, w_pa, w_pb, l, gates)
        x, xb = _out_ln(merged, w_o, l, x, row(ln1_g[l]), row(ln1_b[l]))
        act = _ffn_up(xb, w_u, l, conv_w, conv_b, seq)
        last = l == DEPTH - 1
        outs = _down_ln(act, w_d, l, x, row(ln2_g[l]), row(ln2_b[l]), with_bf16=not last)
        x = outs[0]
        if not last:
            xb = outs[1]
    return x.reshape(batch, seq, D_MODEL)
```

```python
import functools
import math

import jax
import jax.numpy as jnp
from jax import lax
from jax.experimental import pallas as pl
from jax.experimental.pallas import tpu as pltpu

D_MODEL = 2048
DEPTH = 2
HEAD_DIM_A = 64
N_Q_A = 32
N_KV_A = 4
GROUP_A = N_Q_A // N_KV_A
WINDOW = 128
HEAD_DIM_B = 128
N_HEADS_B = D_MODEL // (2 * HEAD_DIM_B)
SUBLN_EPS = 1e-5
ROPE_THETA = 10000.0
D_FF = ((8 * D_MODEL // 3 + 255) // 256) * 256
CONV_WIDTH = 3
LN_EPS = 1e-5
ALPHA = (2 * DEPTH) ** 0.25
NEG_INF = -1e30
LOG2_E = math.log2(math.e)

W_QA = N_Q_A * HEAD_DIM_A
W_KA = N_KV_A * HEAD_DIM_A
W_A = W_QA + 2 * W_KA
W_QB = N_HEADS_B * 2 * HEAD_DIM_B

LANES = 128
MXU_COLS = 256
BF16_ROWS = 16
VMEM_LIMIT = 56 * 1024 * 1024
ATTN_B_TK = 512

F32 = jnp.float32
BF16 = jnp.bfloat16


def _params(*sem):
    return pltpu.CompilerParams(dimension_semantics=sem, vmem_limit_bytes=VMEM_LIMIT)


def _dot(a, b):
    return jnp.dot(a, b, preferred_element_type=F32)


def _dot_nt(a, b):
    return lax.dot_general(a, b, (((1,), (1,)), ((), ())), preferred_element_type=F32)


def _layer_norm(z, g, b):
    mu = jnp.mean(z, axis=-1, keepdims=True)
    zc = z - mu
    var = jnp.mean(zc * zc, axis=-1, keepdims=True)
    return zc * lax.rsqrt(var + LN_EPS) * g + b


def _rope_tables(seq):
    def cs(dim):
        inv = 1.0 / (ROPE_THETA ** (jnp.arange(0, dim, 2, dtype=F32) / dim))
        ang = jnp.arange(seq, dtype=F32)[:, None] * inv[None, :]
        return jnp.cos(ang), jnp.sin(ang)

    cos_a, sin_a = cs(HEAD_DIM_A)
    cos_b, sin_b = cs(HEAD_DIM_B)
    zeros_a = jnp.zeros_like(sin_a)
    tab_a = jnp.stack([
        jnp.tile(cos_a, (1, 4)),
        jnp.tile(jnp.concatenate([zeros_a, sin_a], axis=1), (1, 2)),
        jnp.tile(jnp.concatenate([-sin_a, zeros_a], axis=1), (1, 2)),
    ])
    tab_b = jnp.stack([
        jnp.tile(cos_b, (1, 2)),
        jnp.concatenate([-sin_b, sin_b], axis=1),
    ])
    return tab_a, tab_b


def _proj_a_kernel(x_ref, w_ref, tab_ref, o_ref):
    acc = _dot(x_ref[...], w_ref[...])
    cos, hi, lo = tab_ref[0], tab_ref[1], tab_ref[2]
    scale = LOG2_E * HEAD_DIM_A ** -0.5
    n_rope = (W_QA + W_KA) // LANES
    for g in range(W_A // LANES):
        xg = acc[:, g * LANES:(g + 1) * LANES]
        if g < n_rope:
            xg = xg * cos + pltpu.roll(xg, 32, axis=1) * hi + pltpu.roll(xg, 96, axis=1) * lo
            if g < W_QA // LANES:
                xg = xg * scale
        o_ref[:, g * LANES:(g + 1) * LANES] = xg.astype(o_ref.dtype)


def _proj_a(xb, w_a, layer, tab_a, seq, tm=512):
    t = xb.shape[0]
    return pl.pallas_call(
        _proj_a_kernel,
        grid=(t // tm,),
        in_specs=[
            pl.BlockSpec((tm, D_MODEL), lambda i: (i, 0)),
            pl.BlockSpec((None, D_MODEL, W_A), lambda i: (layer, 0, 0)),
            pl.BlockSpec((3, tm, LANES), lambda i: (0, i % (seq // tm), 0)),
        ],
        out_specs=pl.BlockSpec((tm, W_A), lambda i: (i, 0)),
        out_shape=jax.ShapeDtypeStruct((t, W_A), BF16),
        compiler_params=_params("parallel"),
        name="proj_a",
    )(xb, w_a, tab_a)


def _proj_rope_b_kernel(x_ref, w_ref, tab_ref, o_ref, *, tn, n_scaled):
    j = pl.program_id(1)
    scale = jnp.where(j < n_scaled, LOG2_E * HEAD_DIM_B ** -0.5, 1.0)
    cos = tab_ref[0] * scale
    sgn = tab_ref[1] * scale
    x = x_ref[...]
    for c in range(tn // MXU_COLS):
        acc = _dot(x, w_ref[:, c * MXU_COLS:(c + 1) * MXU_COLS])
        for g in range(MXU_COLS // LANES):
            xg = acc[:, g * LANES:(g + 1) * LANES]
            lo = c * MXU_COLS + g * LANES
            o_ref[:, lo:lo + LANES] = (xg * cos + pltpu.roll(xg, 64, axis=1) * sgn).astype(o_ref.dtype)


def _proj_qk_b(xb, w_qk, layer, tab_b, seq, tm=1024, tn=1024):
    t = xb.shape[0]
    return pl.pallas_call(
        functools.partial(_proj_rope_b_kernel, tn=tn, n_scaled=W_QB // tn),
        grid=(t // tm, 2 * W_QB // tn),
        in_specs=[
            pl.BlockSpec((tm, D_MODEL), lambda i, j: (i, 0)),
            pl.BlockSpec((None, D_MODEL, tn), lambda i, j: (layer, 0, j)),
            pl.BlockSpec((2, tm, LANES), lambda i, j: (0, i % (seq // tm), 0)),
        ],
        out_specs=pl.BlockSpec((tm, tn), lambda i, j: (i, j)),
        out_shape=jax.ShapeDtypeStruct((t, 2 * W_QB), BF16),
        compiler_params=_params("parallel", "arbitrary"),
        name="proj_qk_b",
    )(xb, w_qk, tab_b)


def _proj_gates_kernel(x_ref, w_ref, o_ref, *, tn):
    x = x_ref[...]
    for c in range(tn // MXU_COLS):
        cs = slice(c * MXU_COLS, (c + 1) * MXU_COLS)
        o_ref[:, cs] = jax.nn.sigmoid(_dot(x, w_ref[:, cs])).astype(o_ref.dtype)


def _proj_gates(xb, w_g, layer, tm=1024, tn=1024):
    t = xb.shape[0]
    n = w_g.shape[-1]
    return pl.pallas_call(
        functools.partial(_proj_gates_kernel, tn=tn),
        grid=(t // tm, n // tn),
        in_specs=[
            pl.BlockSpec((tm, D_MODEL), lambda i, j: (i, 0)),
            pl.BlockSpec((None, D_MODEL, tn), lambda i, j: (layer, 0, j)),
        ],
        out_specs=pl.BlockSpec((tm, tn), lambda i, j: (i, j)),
        out_shape=jax.ShapeDtypeStruct((t, n), BF16),
        compiler_params=_params("parallel", "arbitrary"),
        name="proj_gates",
    )(xb, w_g)


def _proj_vt_b_kernel(x_ref, w_ref, o_ref, *, tk):
    x = x_ref[...]
    w = 2 * HEAD_DIM_B
    for c in range(o_ref.shape[0]):
        acc = _dot(x, w_ref[:, c * w:(c + 1) * w]).astype(o_ref.dtype)
        for kk in range(o_ref.shape[1]):
            o_ref[c, kk] = acc[kk * tk:(kk + 1) * tk].T


def _proj_vt_b(xb, w_v, layer, batch, seq, tk, tm=1024, heads=4):
    w = 2 * HEAD_DIM_B
    tiles_per_seq = seq // tm
    return pl.pallas_call(
        functools.partial(_proj_vt_b_kernel, tk=tk),
        grid=(batch * tiles_per_seq, N_HEADS_B // heads),
        in_specs=[
            pl.BlockSpec((tm, D_MODEL), lambda i, j: (i, 0)),
            pl.BlockSpec((None, D_MODEL, heads * w), lambda i, j: (layer, 0, j)),
        ],
        out_specs=pl.BlockSpec((None, heads, tm // tk, w, tk),
                               lambda i, j: (i // tiles_per_seq, j, i % tiles_per_seq, 0, 0)),
        out_shape=jax.ShapeDtypeStruct((batch, N_HEADS_B, seq // tk, w, tk), BF16),
        compiler_params=_params("parallel", "arbitrary"),
        name="proj_vt_b",
    )(xb, w_v)


def _attn_a_kernel(sinks_ref, q_ref, kvp_ref, kvc_ref, o_ref):
    n = pl.program_id(1)
    blk = WINDOW
    hd = HEAD_DIM_A
    key = lax.broadcasted_iota(jnp.int32, (2 * blk, blk), 0)
    qry = lax.broadcasted_iota(jnp.int32, (2 * blk, blk), 1)
    mask = (key <= blk + qry) & (key > qry) & ((key >= blk) | (n > 0))
    vpt = kvp_ref[:, W_KA:].T
    vct = kvc_ref[:, W_KA:].T

    for g in range(N_KV_A):
        ks = slice(g * hd, (g + 1) * hd)
        kband = jnp.concatenate([kvp_ref[:, ks], kvc_ref[:, ks]], axis=0)
        vt = jnp.concatenate([vpt[ks], vct[ks]], axis=1)
        qts = []
        for pr in range(GROUP_A // 2):
            h0 = g * GROUP_A + 2 * pr
            t = q_ref[:, h0 * hd:(h0 + 2) * hd].T
            qts += [t[:hd], t[hd:]]
        qt = jnp.concatenate(qts, axis=1)
        s = _dot(kband, qt)
        ps, rs = [], []
        for hh in range(GROUP_A):
            sink = sinks_ref[g * GROUP_A + hh] * LOG2_E
            sh = jnp.where(mask, s[:, hh * blk:(hh + 1) * blk], NEG_INF)
            m = jnp.maximum(jnp.max(sh, axis=0, keepdims=True), sink)
            p = jnp.exp2(sh - m)
            denom = jnp.sum(p, axis=0, keepdims=True) + jnp.exp2(sink - m)
            ps.append(p.astype(BF16))
            rs.append(1.0 / denom)
        ot = _dot(vt, jnp.concatenate(ps, axis=1))
        for pr in range(GROUP_A // 2):
            h0 = g * GROUP_A + 2 * pr
            pair = jnp.concatenate([ot[:, (2 * pr + e) * blk:(2 * pr + e + 1) * blk] * rs[2 * pr + e]
                                    for e in range(2)], axis=0)
            o_ref[:, h0 * hd:(h0 + 2) * hd] = pair.astype(o_ref.dtype).T


def _attn_a(qkv_a, sinks, batch, seq):
    blk = WINDOW
    nb = seq // blk
    kv_col = W_QA // (2 * W_KA)
    return pl.pallas_call(
        _attn_a_kernel,
        grid=(batch, nb),
        in_specs=[
            pl.BlockSpec(memory_space=pltpu.SMEM),
            pl.BlockSpec((blk, W_QA), lambda b, n: (b * nb + n, 0)),
            pl.BlockSpec((blk, 2 * W_KA), lambda b, n: (b * nb + jnp.maximum(n - 1, 0), kv_col)),
            pl.BlockSpec((blk, 2 * W_KA), lambda b, n: (b * nb + n, kv_col)),
        ],
        out_specs=pl.BlockSpec((blk, W_QA), lambda b, n: (b * nb + n, 0)),
        out_shape=jax.ShapeDtypeStruct((batch * seq, W_QA), BF16),
        compiler_params=_params("parallel", "parallel"),
        name="attn_a",
    )(sinks, qkv_a, qkv_a, qkv_a)


def _attn_b_kernel(lq1_ref, lk1_ref, lq2_ref, lk2_ref, g_ref, q_ref, k_ref, vt_ref, o_ref,
                   qt_ref, m_ref, l_ref, acc_ref, *, tq, tk, n_q_tiles, lam_init):
    qi = pl.program_id(2)
    hd = HEAD_DIM_B
    cg = MXU_COLS
    n_groups = tq // cg
    q = q_ref[...].astype(F32)
    for c in range(2):
        qt_ref[c] = q[:, c * hd:(c + 1) * hd].T.astype(BF16)
    m_ref[...] = jnp.full(m_ref.shape, NEG_INF, F32)
    l_ref[...] = jnp.zeros(l_ref.shape, F32)
    acc_ref[...] = jnp.zeros(acc_ref.shape, F32)

    def chain(c, j, k, vt, first_key):
        js = slice(j * cg, (j + 1) * cg)
        s = _dot(k[:, c * hd:(c + 1) * hd], qt_ref[c, :, js])
        if first_key is not None:
            key = lax.broadcasted_iota(jnp.int32, s.shape, 0)
            qry = lax.broadcasted_iota(jnp.int32, s.shape, 1)
            s = jnp.where(key + first_key <= qry, s, NEG_INF)
        m_old = m_ref[c, :, js]
        m_new = jnp.maximum(m_old, jnp.max(s, axis=0, keepdims=True))
        a = jnp.exp2(m_old - m_new)
        p = jnp.exp2(s - m_new)
        l_ref[c, :, js] = a * l_ref[c, :, js] + jnp.sum(p, axis=0, keepdims=True)
        acc_ref[c, :, js] = a * acc_ref[c, :, js] + _dot(vt, p.astype(BF16))
        m_ref[c, :, js] = m_new

    blocks_per_tile = tq // tk

    def finish(j):
        js = slice(j * cg, (j + 1) * cg)
        o = acc_ref[0, :, js] * (1.0 / l_ref[0, :, js]) - lam * (acc_ref[1, :, js] * (1.0 / l_ref[1, :, js]))
        o = o * lax.rsqrt(jnp.mean(o * o, axis=0, keepdims=True) + SUBLN_EPS)
        o = o * g_ref[...] * (1.0 - lam_init)
        o_ref[js, :] = o.T.astype(o_ref.dtype)

    def tile(q_tile):
        for kb in range(q_tile * blocks_per_tile):
            k = k_ref[kb * tk:(kb + 1) * tk, :]
            vt = vt_ref[kb]
            for j in range(n_groups):
                for c in range(2):
                    chain(c, j, k, vt, None)
        for u in range(blocks_per_tile):
            kb = q_tile * blocks_per_tile + u
            k = k_ref[kb * tk:(kb + 1) * tk, :]
            vt = vt_ref[kb]
            for j in range(n_groups):
                first_key = u * tk - j * cg
                n_vis = min(tk, (j + 1) * cg - u * tk)
                if n_vis <= 0:
                    continue
                masked = first_key + n_vis - 1 > 0
                for c in range(2):
                    chain(c, j, k[:n_vis], vt[:, :n_vis], first_key if masked else None)
        for j in range(n_groups):
            finish(j)

    lam = (jnp.exp(jnp.sum(lq1_ref[...] * lk1_ref[...], axis=-1, keepdims=True))
           - jnp.exp(jnp.sum(lq2_ref[...] * lk2_ref[...], axis=-1, keepdims=True))
           + lam_init)
    for q_tile in range(n_q_tiles):
        pl.when(qi == q_tile)(functools.partial(tile, q_tile))


def _attn_b(qk_b, vt, lq1, lk1, lq2, lk2, subln_g, lam_init, batch, seq, tq=1024, tk=512):
    nq = seq // tq
    w = 2 * HEAD_DIM_B
    vec = lambda n: pl.BlockSpec((1, n), lambda b, h, i: (0, 0))
    return pl.pallas_call(
        functools.partial(_attn_b_kernel, tq=tq, tk=tk, n_q_tiles=nq, lam_init=lam_init),
        grid=(batch, N_HEADS_B, nq),
        in_specs=[
            vec(HEAD_DIM_B), vec(HEAD_DIM_B), vec(HEAD_DIM_B), vec(HEAD_DIM_B),
            pl.BlockSpec((w, 1), lambda b, h, i: (0, 0)),
            pl.BlockSpec((tq, w), lambda b, h, i: (b * nq + i, h)),
            pl.BlockSpec((seq, w), lambda b, h, i: (b, N_HEADS_B + h)),
            pl.BlockSpec((None, None, seq // tk, w, tk), lambda b, h, i: (b, h, 0, 0, 0)),
        ],
        out_specs=pl.BlockSpec((tq, w), lambda b, h, i: (b * nq + i, h)),
        out_shape=jax.ShapeDtypeStruct((batch * seq, W_QB), BF16),
        scratch_shapes=[
            pltpu.VMEM((2, HEAD_DIM_B, tq), BF16),
            pltpu.VMEM((2, 1, tq), F32),
            pltpu.VMEM((2, 1, tq), F32),
            pltpu.VMEM((2, w, tq), F32),
        ],
        compiler_params=_params("parallel", "parallel", "arbitrary"),
        name="attn_b",
    )(lq1, lk1, lq2, lk2, subln_g, qk_b, qk_b, vt)


def _merge_kernel(oa_ref, ob_ref, wa_ref, wb_ref, ga_ref, gb_ref, o_ref):
    pa = _dot(oa_ref[...], wa_ref[...])
    pb = _dot(ob_ref[...], wb_ref[...])
    o_ref[...] = (ga_ref[...].astype(F32) * pa + gb_ref[...].astype(F32) * pb).astype(o_ref.dtype)


def _merge(o_a, o_b, w_pa, w_pb, layer, gates, tm=1024, tn=1024):
    t = o_a.shape[0]
    return pl.pallas_call(
        _merge_kernel,
        grid=(t // tm, D_MODEL // tn),
        in_specs=[
            pl.BlockSpec((tm, W_QA), lambda i, j: (i, 0)),
            pl.BlockSpec((tm, W_QB), lambda i, j: (i, 0)),
            pl.BlockSpec((None, W_QA, tn), lambda i, j: (layer, 0, j)),
            pl.BlockSpec((None, W_QB, tn), lambda i, j: (layer, 0, j)),
            pl.BlockSpec((tm, tn), lambda i, j: (i, j)),
            pl.BlockSpec((tm, tn), lambda i, j: (i, D_MODEL // tn + j)),
        ],
        out_specs=pl.BlockSpec((tm, tn), lambda i, j: (i, j)),
        out_shape=jax.ShapeDtypeStruct((t, D_MODEL), BF16),
        compiler_params=_params("parallel", "arbitrary"),
        name="merge",
    )(o_a, o_b, w_pa, w_pb, gates, gates)


def _out_ln_kernel(m_ref, w_ref, x_ref, g_ref, b_ref, y_ref, yb_ref, *, rows):
    g, b = g_ref[...], b_ref[...]
    for r in range(m_ref.shape[0] // rows):
        rs = slice(r * rows, (r + 1) * rows)
        z = ALPHA * x_ref[rs, :] + _dot(m_ref[rs, :], w_ref[...])
        y = _layer_norm(z, g, b)
        y_ref[rs, :] = y
        yb_ref[rs, :] = y.astype(yb_ref.dtype)


def _out_ln(merged, w_out, layer, x, g, b, tm=512, rows=256):
    t = x.shape[0]
    row = pl.BlockSpec((tm, D_MODEL), lambda i: (i, 0))
    vec = pl.BlockSpec((1, D_MODEL), lambda i: (0, 0))
    return pl.pallas_call(
        functools.partial(_out_ln_kernel, rows=rows),
        grid=(t // tm,),
        in_specs=[row, pl.BlockSpec((None, D_MODEL, D_MODEL), lambda i: (layer, 0, 0)), row, vec, vec],
        out_specs=[row, row],
        out_shape=[jax.ShapeDtypeStruct((t, D_MODEL), F32), jax.ShapeDtypeStruct((t, D_MODEL), BF16)],
        compiler_params=_params("parallel"),
        name="out_ln",
    )(merged, w_out, x, g, b)


def _ffn_up_kernel(xp_ref, x_ref, wg_ref, wv_ref, cwg_ref, cwv_ref, cbg_ref, cbv_ref, o_ref,
                   *, tm, tn, tiles_per_seq):
    i = pl.program_id(0)
    halo = BF16_ROWS
    xp = jnp.where(i % tiles_per_seq == 0, jnp.zeros((), BF16), xp_ref[...])
    xe = jnp.concatenate([xp, x_ref[...]], axis=0)

    def conv(w_ref, cw_ref, cb_ref, cs):
        u = _dot(xe, w_ref[:, cs])
        out = cb_ref[:, cs] + cw_ref[CONV_WIDTH - 1:CONV_WIDTH, cs] * u[halo:]
        for back in range(1, CONV_WIDTH):
            tap = CONV_WIDTH - 1 - back
            out = out + cw_ref[tap:tap + 1, cs] * pltpu.roll(u, back, axis=0)[halo:]
        return out

    for c in range(tn // MXU_COLS):
        cs = slice(c * MXU_COLS, (c + 1) * MXU_COLS)
        gate = conv(wg_ref, cwg_ref, cbg_ref, cs)
        val = conv(wv_ref, cwv_ref, cbv_ref, cs)
        o_ref[:, cs] = (jax.nn.silu(gate) * val).astype(o_ref.dtype)


def _ffn_up(xb, w_up, layer, conv_w, conv_b, seq, tm=1024, tn=512):
    t = xb.shape[0]
    nf = D_FF // tn
    halo = BF16_ROWS
    return pl.pallas_call(
        functools.partial(_ffn_up_kernel, tm=tm, tn=tn, tiles_per_seq=seq // tm),
        grid=(t // tm, nf),
        in_specs=[
            pl.BlockSpec((halo, D_MODEL), lambda i, j: (jnp.maximum(i * (tm // halo) - 1, 0), 0)),
            pl.BlockSpec((tm, D_MODEL), lambda i, j: (i, 0)),
            pl.BlockSpec((None, D_MODEL, tn), lambda i, j: (layer, 0, j)),
            pl.BlockSpec((None, D_MODEL, tn), lambda i, j: (layer, 0, nf + j)),
            pl.BlockSpec((None, CONV_WIDTH, tn), lambda i, j: (layer, 0, j)),
            pl.BlockSpec((None, CONV_WIDTH, tn), lambda i, j: (layer, 0, nf + j)),
            pl.BlockSpec((None, 1, tn), lambda i, j: (layer, 0, j)),
            pl.BlockSpec((None, 1, tn), lambda i, j: (layer, 0, nf + j)),
        ],
        out_specs=pl.BlockSpec((tm, tn), lambda i, j: (i, j)),
        out_shape=jax.ShapeDtypeStruct((t, D_FF), BF16),
        compiler_params=_params("parallel", "arbitrary"),
        name="ffn_up",
    )(xb, xb, w_up, w_up, conv_w, conv_w, conv_b, conv_b)


def _down_ln_kernel(a_ref, w_ref, x_ref, g_ref, b_ref, *out_refs):
    z = ALPHA * x_ref[...] + _dot(a_ref[...], w_ref[...])
    y = _layer_norm(z, g_ref[...], b_ref[...])
    out_refs[0][...] = y
    if len(out_refs) > 1:
        out_refs[1][...] = y.astype(BF16)


def _down_ln(act, w_down, layer, x, g, b, with_bf16, tm=256):
    t = x.shape[0]
    row = pl.BlockSpec((tm, D_MODEL), lambda i: (i, 0))
    vec = pl.BlockSpec((1, D_MODEL), lambda i: (0, 0))
    out_specs = [row, row] if with_bf16 else [row]
    out_shape = [jax.ShapeDtypeStruct((t, D_MODEL), F32)]
    if with_bf16:
        out_shape.append(jax.ShapeDtypeStruct((t, D_MODEL), BF16))
    return pl.pallas_call(
        _down_ln_kernel,
        grid=(t // tm,),
        in_specs=[
            pl.BlockSpec((tm, D_FF), lambda i: (i, 0)),
            pl.BlockSpec((None, D_FF, D_MODEL), lambda i: (layer, 0, 0), pipeline_mode=pl.Buffered(1)),
            row, vec, vec,
        ],
        out_specs=out_specs,
        out_shape=out_shape,
        compiler_params=_params("parallel"),
        name="down_ln",
    )(act, w_down, x, g, b)


def kernel(x, w_in, sinks, lambda_q1, lambda_k1, lambda_q2, lambda_k2, subln_g, w_proj_a, w_proj_b,
           w_out, ln1_g, ln1_b, w_up, conv_w, conv_b, w_down, ln2_g, ln2_b):
    batch, seq, _ = x.shape
    t = batch * seq
    tab_a, tab_b = _rope_tables(seq)
    x = x.reshape(t, D_MODEL)
    xb = x.astype(BF16)
    row = lambda v: v.reshape(1, -1).astype(F32)
    cols = lambda lo, n: w_in[:, :, lo:lo + n].astype(BF16)
    w_a, w_qk = cols(0, W_A), cols(W_A, 2 * W_QB)
    w_v, w_g = cols(W_A + 2 * W_QB, W_QB), cols(W_A + 3 * W_QB, 2 * D_MODEL)
    w_pa, w_pb, w_o = w_proj_a.astype(BF16), w_proj_b.astype(BF16), w_out.astype(BF16)
    w_u, w_d = w_up.astype(BF16), w_down.astype(BF16)
    conv_w = conv_w.astype(F32)
    conv_b = conv_b.astype(F32).reshape(DEPTH, 1, 2 * D_FF)
    for l in range(DEPTH):
        qkv_a = _proj_a(xb, w_a, l, tab_a, seq)
        qk_b = _proj_qk_b(xb, w_qk, l, tab_b, seq)
        vt_b = _proj_vt_b(xb, w_v, l, batch, seq, ATTN_B_TK)
        gates = _proj_gates(xb, w_g, l)
        o_a = _attn_a(qkv_a, sinks[l].astype(F32), batch, seq)
        lam_init = 0.8 - 0.6 * math.exp(-0.3 * l)
        o_b = _attn_b(qk_b, vt_b, row(lambda_q1[l]), row(lambda_k1[l]), row(lambda_q2[l]), row(lambda_k2[l]),
                      subln_g[l].reshape(-1, 1).astype(F32), lam_init, batch, seq, tk=ATTN_B_TK)
        merged = _merge(o_a, o_b, w_pa, w_pb, l, gates)
        x, xb = _out_ln(merged, w_o, l, x, row(ln1_g[l]), row(ln1_b[l]))
        act = _ffn_up(xb, w_u, l, conv_w, conv_b, seq)
        last = l == DEPTH - 1
        outs = _down_ln(act, w_d, l, x, row(ln2_g[l]), row(ln2_b[l]), with_bf16=not last)
        x = outs[0]
        if not last:
            xb = outs[1]
    return x.reshape(batch, seq, D_MODEL)
```

```python
import functools
import math

import jax
import jax.numpy as jnp
from jax import lax
from jax.experimental import pallas as pl
from jax.experimental.pallas import tpu as pltpu

D_MODEL = 2048
DEPTH = 2
HEAD_DIM_A = 64
N_Q_A = 32
N_KV_A = 4
GROUP_A = N_Q_A // N_KV_A
WINDOW = 128
HEAD_DIM_B = 128
N_HEADS_B = D_MODEL // (2 * HEAD_DIM_B)
SUBLN_EPS = 1e-5
ROPE_THETA = 10000.0
D_FF = ((8 * D_MODEL // 3 + 255) // 256) * 256
CONV_WIDTH = 3
LN_EPS = 1e-5
ALPHA = (2 * DEPTH) ** 0.25
NEG_INF = -1e30
LOG2_E = math.log2(math.e)

W_QA = N_Q_A * HEAD_DIM_A
W_KA = N_KV_A * HEAD_DIM_A
W_A = W_QA + 2 * W_KA
W_QB = N_HEADS_B * 2 * HEAD_DIM_B

LANES = 128
MXU_COLS = 256
BF16_ROWS = 16
VMEM_LIMIT = 56 * 1024 * 1024
ATTN_B_TK = 512

F32 = jnp.float32
BF16 = jnp.bfloat16


def _params(*sem):
    return pltpu.CompilerParams(dimension_semantics=sem, vmem_limit_bytes=VMEM_LIMIT)


def _dot(a, b):
    return jnp.dot(a, b, preferred_element_type=F32)


def _dot_nt(a, b):
    return lax.dot_general(a, b, (((1,), (1,)), ((), ())), preferred_element_type=F32)


def _layer_norm(z, g, b):
    mu = jnp.mean(z, axis=-1, keepdims=True)
    zc = z - mu
    var = jnp.mean(zc * zc, axis=-1, keepdims=True)
    return zc * lax.rsqrt(var + LN_EPS) * g + b


def _rope_tables(seq):
    def cs(dim):
        inv = 1.0 / (ROPE_THETA ** (jnp.arange(0, dim, 2, dtype=F32) / dim))
        ang = jnp.arange(seq, dtype=F32)[:, None] * inv[None, :]
        return jnp.cos(ang), jnp.sin(ang)

    cos_a, sin_a = cs(HEAD_DIM_A)
    cos_b, sin_b = cs(HEAD_DIM_B)
    zeros_a = jnp.zeros_like(sin_a)
    tab_a = jnp.stack([
        jnp.tile(cos_a, (1, 4)),
        jnp.tile(jnp.concatenate([zeros_a, sin_a], axis=1), (1, 2)),
        jnp.tile(jnp.concatenate([-sin_a, zeros_a], axis=1), (1, 2)),
    ])
    tab_b = jnp.stack([
        jnp.tile(cos_b, (1, 2)),
        jnp.concatenate([-sin_b, sin_b], axis=1),
    ])
    return tab_a, tab_b


def _proj_a_kernel(x_ref, w_ref, tab_ref, o_ref, *xb_refs):
    x = x_ref[...].astype(BF16)
    if xb_refs:
        xb_refs[0][...] = x
    acc = _dot(x, w_ref[...])
    cos, hi, lo = tab_ref[0], tab_ref[1], tab_ref[2]
    scale = LOG2_E * HEAD_DIM_A ** -0.5
    n_rope = (W_QA + W_KA) // LANES
    for g in range(W_A // LANES):
        xg = acc[:, g * LANES:(g + 1) * LANES]
        if g < n_rope:
            xg = xg * cos + pltpu.roll(xg, 32, axis=1) * hi + pltpu.roll(xg, 96, axis=1) * lo
            if g < W_QA // LANES:
                xg = xg * scale
        o_ref[:, g * LANES:(g + 1) * LANES] = xg.astype(o_ref.dtype)


def _proj_a(x, w_a, layer, tab_a, seq, tm=512):
    t = x.shape[0]
    out_specs = [pl.BlockSpec((tm, W_A), lambda i: (i, 0))]
    out_shape = [jax.ShapeDtypeStruct((t, W_A), BF16)]
    if x.dtype != BF16:
        out_specs.append(pl.BlockSpec((tm, D_MODEL), lambda i: (i, 0)))
        out_shape.append(jax.ShapeDtypeStruct((t, D_MODEL), BF16))
    return pl.pallas_call(
        _proj_a_kernel,
        grid=(t // tm,),
        in_specs=[
            pl.BlockSpec((tm, D_MODEL), lambda i: (i, 0)),
            pl.BlockSpec((None, D_MODEL, W_A), lambda i: (layer, 0, 0)),
            pl.BlockSpec((3, tm, LANES), lambda i: (0, i % (seq // tm), 0)),
        ],
        out_specs=out_specs,
        out_shape=out_shape,
        compiler_params=_params("parallel"),
        name="proj_a",
    )(x, w_a, tab_a)


def _proj_rope_b_kernel(x_ref, w_ref, tab_ref, o_ref, *, tn, n_scaled):
    j = pl.program_id(1)
    scale = jnp.where(j < n_scaled, LOG2_E * HEAD_DIM_B ** -0.5, 1.0)
    cos = tab_ref[0] * scale
    sgn = tab_ref[1] * scale
    x = x_ref[...]
    for c in range(tn // MXU_COLS):
        acc = _dot(x, w_ref[:, c * MXU_COLS:(c + 1) * MXU_COLS])
        for g in range(MXU_COLS // LANES):
            xg = acc[:, g * LANES:(g + 1) * LANES]
            lo = c * MXU_COLS + g * LANES
            o_ref[:, lo:lo + LANES] = (xg * cos + pltpu.roll(xg, 64, axis=1) * sgn).astype(o_ref.dtype)


def _proj_qk_b(xb, w_qk, layer, tab_b, seq, tm=1024, tn=2048):
    t = xb.shape[0]
    return pl.pallas_call(
        functools.partial(_proj_rope_b_kernel, tn=tn, n_scaled=W_QB // tn),
        grid=(t // tm, 2 * W_QB // tn),
        in_specs=[
            pl.BlockSpec((tm, D_MODEL), lambda i, j: (i, 0)),
            pl.BlockSpec((None, D_MODEL, tn), lambda i, j: (layer, 0, j)),
            pl.BlockSpec((2, tm, LANES), lambda i, j: (0, i % (seq // tm), 0)),
        ],
        out_specs=pl.BlockSpec((tm, tn), lambda i, j: (i, j)),
        out_shape=jax.ShapeDtypeStruct((t, 2 * W_QB), BF16),
        compiler_params=_params("parallel", "arbitrary"),
        name="proj_qk_b",
    )(xb, w_qk, tab_b)


def _proj_gates_kernel(x_ref, w_ref, o_ref, *, tn):
    x = x_ref[...]
    for c in range(tn // MXU_COLS):
        cs = slice(c * MXU_COLS, (c + 1) * MXU_COLS)
        o_ref[:, cs] = jax.nn.sigmoid(_dot(x, w_ref[:, cs])).astype(o_ref.dtype)


def _proj_gates(xb, w_g, layer, tm=1024, tn=2048):
    t = xb.shape[0]
    n = w_g.shape[-1]
    return pl.pallas_call(
        functools.partial(_proj_gates_kernel, tn=tn),
        grid=(t // tm, n // tn),
        in_specs=[
            pl.BlockSpec((tm, D_MODEL), lambda i, j: (i, 0)),
            pl.BlockSpec((None, D_MODEL, tn), lambda i, j: (layer, 0, j)),
        ],
        out_specs=pl.BlockSpec((tm, tn), lambda i, j: (i, j)),
        out_shape=jax.ShapeDtypeStruct((t, n), BF16),
        compiler_params=_params("parallel", "arbitrary"),
        name="proj_gates",
    )(xb, w_g)


def _proj_vt_b_kernel(x_ref, w_ref, o_ref, *, tk):
    x = x_ref[...]
    w = 2 * HEAD_DIM_B
    for c in range(o_ref.shape[0]):
        acc = _dot(x, w_ref[:, c * w:(c + 1) * w]).astype(o_ref.dtype)
        for kk in range(o_ref.shape[1]):
            o_ref[c, kk] = acc[kk * tk:(kk + 1) * tk].T


def _proj_vt_b(xb, w_v, layer, batch, seq, tk, tm=1024, heads=4):
    w = 2 * HEAD_DIM_B
    tiles_per_seq = seq // tm
    return pl.pallas_call(
        functools.partial(_proj_vt_b_kernel, tk=tk),
        grid=(batch * tiles_per_seq, N_HEADS_B // heads),
        in_specs=[
            pl.BlockSpec((tm, D_MODEL), lambda i, j: (i, 0)),
            pl.BlockSpec((None, D_MODEL, heads * w), lambda i, j: (layer, 0, j)),
        ],
        out_specs=pl.BlockSpec((None, heads, tm // tk, w, tk),
                               lambda i, j: (i // tiles_per_seq, j, i % tiles_per_seq, 0, 0)),
        out_shape=jax.ShapeDtypeStruct((batch, N_HEADS_B, seq // tk, w, tk), BF16),
        compiler_params=_params("parallel", "arbitrary"),
        name="proj_vt_b",
    )(xb, w_v)


def _attn_a_kernel(sinks_ref, q_ref, kvp_ref, kvc_ref, o_ref):
    n = pl.program_id(1)
    blk = WINDOW
    hd = HEAD_DIM_A
    key = lax.broadcasted_iota(jnp.int32, (2 * blk, blk), 0)
    qry = lax.broadcasted_iota(jnp.int32, (2 * blk, blk), 1)
    mask = (key <= blk + qry) & (key > qry) & ((key >= blk) | (n > 0))
    vpt = kvp_ref[:, W_KA:].T
    vct = kvc_ref[:, W_KA:].T

    for g in range(N_KV_A):
        ks = slice(g * hd, (g + 1) * hd)
        kband = jnp.concatenate([kvp_ref[:, ks], kvc_ref[:, ks]], axis=0)
        vt = jnp.concatenate([vpt[ks], vct[ks]], axis=1)
        qts = []
        for pr in range(GROUP_A // 2):
            h0 = g * GROUP_A + 2 * pr
            t = q_ref[:, h0 * hd:(h0 + 2) * hd].T
            qts += [t[:hd], t[hd:]]
        qt = jnp.concatenate(qts, axis=1)
        s = _dot(kband, qt)
        ps, rs = [], []
        for hh in range(GROUP_A):
            sink = sinks_ref[g * GROUP_A + hh] * LOG2_E
            sh = jnp.where(mask, s[:, hh * blk:(hh + 1) * blk], NEG_INF)
            m = jnp.maximum(jnp.max(sh, axis=0, keepdims=True), sink)
            p = jnp.exp2(sh - m)
            denom = jnp.sum(p, axis=0, keepdims=True) + jnp.exp2(sink - m)
            ps.append(p.astype(BF16))
            rs.append(1.0 / denom)
        ot = _dot(vt, jnp.concatenate(ps, axis=1))
        for pr in range(GROUP_A // 2):
            h0 = g * GROUP_A + 2 * pr
            pair = jnp.concatenate([ot[:, (2 * pr + e) * blk:(2 * pr + e + 1) * blk] * rs[2 * pr + e]
                                    for e in range(2)], axis=0)
            o_ref[:, h0 * hd:(h0 + 2) * hd] = pair.astype(o_ref.dtype).T


def _attn_a(qkv_a, sinks, batch, seq):
    blk = WINDOW
    nb = seq // blk
    kv_col = W_QA // (2 * W_KA)
    return pl.pallas_call(
        _attn_a_kernel,
        grid=(batch, nb),
        in_specs=[
            pl.BlockSpec(memory_space=pltpu.SMEM),
            pl.BlockSpec((blk, W_QA), lambda b, n: (b * nb + n, 0)),
            pl.BlockSpec((blk, 2 * W_KA), lambda b, n: (b * nb + jnp.maximum(n - 1, 0), kv_col)),
            pl.BlockSpec((blk, 2 * W_KA), lambda b, n: (b * nb + n, kv_col)),
        ],
        out_specs=pl.BlockSpec((blk, W_QA), lambda b, n: (b * nb + n, 0)),
        out_shape=jax.ShapeDtypeStruct((batch * seq, W_QA), BF16),
        compiler_params=_params("parallel", "parallel"),
        name="attn_a",
    )(sinks, qkv_a, qkv_a, qkv_a)


def _attn_b_kernel(lq1_ref, lk1_ref, lq2_ref, lk2_ref, g_ref, q_ref, k_ref, vt_ref, o_ref,
                   qt_ref, m_ref, l_ref, acc_ref, *, tq, tk, n_q_tiles, lam_init):
    qi = pl.program_id(2)
    hd = HEAD_DIM_B
    cg = MXU_COLS
    n_groups = tq // cg
    for c in range(2):
        qt_ref[c] = q_ref[:, c * hd:(c + 1) * hd].T

    def chain(c, j, k, vt, first_key, start):
        js = slice(j * cg, (j + 1) * cg)
        s = _dot(k[:, c * hd:(c + 1) * hd], qt_ref[c, :, js])
        if first_key is not None:
            key = lax.broadcasted_iota(jnp.int32, s.shape, 0)
            qry = lax.broadcasted_iota(jnp.int32, s.shape, 1)
            s = jnp.where(key + first_key <= qry, s, NEG_INF)
        m_new = jnp.max(s, axis=0, keepdims=True)
        if not start:
            m_old = m_ref[c, :, js]
            m_new = jnp.maximum(m_old, m_new)
            a = jnp.exp2(m_old - m_new)
        p = jnp.exp2(s - m_new)
        l_new = jnp.sum(p, axis=0, keepdims=True)
        acc_new = _dot(vt, p.astype(BF16))
        if not start:
            l_new = a * l_ref[c, :, js] + l_new
            acc_new = a * acc_ref[c, :, js] + acc_new
        l_ref[c, :, js] = l_new
        acc_ref[c, :, js] = acc_new
        m_ref[c, :, js] = m_new

    blocks_per_tile = tq // tk

    def finish(j):
        js = slice(j * cg, (j + 1) * cg)
        o = acc_ref[0, :, js] * (1.0 / l_ref[0, :, js]) - lam * (acc_ref[1, :, js] * (1.0 / l_ref[1, :, js]))
        o = o * lax.rsqrt(jnp.mean(o * o, axis=0, keepdims=True) + SUBLN_EPS)
        o = o * g_ref[...] * (1.0 - lam_init)
        o_ref[js, :] = o.T.astype(o_ref.dtype)

    def tile(q_tile):
        started = set()

        def run(c, j, k, vt, first_key):
            chain(c, j, k, vt, first_key, (c, j) not in started)
            started.add((c, j))

        for kb in range(q_tile * blocks_per_tile):
            k = k_ref[kb * tk:(kb + 1) * tk, :]
            vt = vt_ref[kb]
            for j in range(n_groups):
                for c in range(2):
                    run(c, j, k, vt, None)
        for u in range(blocks_per_tile):
            kb = q_tile * blocks_per_tile + u
            k = k_ref[kb * tk:(kb + 1) * tk, :]
            vt = vt_ref[kb]
            for j in range(n_groups):
                first_key = u * tk - j * cg
                n_vis = min(tk, (j + 1) * cg - u * tk)
                if n_vis <= 0:
                    continue
                masked = first_key + n_vis - 1 > 0
                for c in range(2):
                    run(c, j, k[:n_vis], vt[:, :n_vis], first_key if masked else None)
        for j in range(n_groups):
            finish(j)

    lam = (jnp.exp(jnp.sum(lq1_ref[...] * lk1_ref[...], axis=-1, keepdims=True))
           - jnp.exp(jnp.sum(lq2_ref[...] * lk2_ref[...], axis=-1, keepdims=True))
           + lam_init)
    for q_tile in range(n_q_tiles):
        pl.when(qi == q_tile)(functools.partial(tile, q_tile))


def _attn_b(qk_b, vt, lq1, lk1, lq2, lk2, subln_g, lam_init, batch, seq, tq=1024, tk=512):
    nq = seq // tq
    w = 2 * HEAD_DIM_B
    vec = lambda n: pl.BlockSpec((1, n), lambda b, h, i: (0, 0))
    return pl.pallas_call(
        functools.partial(_attn_b_kernel, tq=tq, tk=tk, n_q_tiles=nq, lam_init=lam_init),
        grid=(batch, N_HEADS_B, nq),
        in_specs=[
            vec(HEAD_DIM_B), vec(HEAD_DIM_B), vec(HEAD_DIM_B), vec(HEAD_DIM_B),
            pl.BlockSpec((w, 1), lambda b, h, i: (0, 0)),
            pl.BlockSpec((tq, w), lambda b, h, i: (b * nq + i, h)),
            pl.BlockSpec((seq, w), lambda b, h, i: (b, N_HEADS_B + h)),
            pl.BlockSpec((None, None, seq // tk, w, tk), lambda b, h, i: (b, h, 0, 0, 0)),
        ],
        out_specs=pl.BlockSpec((tq, w), lambda b, h, i: (b * nq + i, h)),
        out_shape=jax.ShapeDtypeStruct((batch * seq, W_QB), BF16),
        scratch_shapes=[
            pltpu.VMEM((2, HEAD_DIM_B, tq), BF16),
            pltpu.VMEM((2, 1, tq), F32),
            pltpu.VMEM((2, 1, tq), F32),
            pltpu.VMEM((2, w, tq), F32),
        ],
        compiler_params=_params("parallel", "parallel", "arbitrary"),
        name="attn_b",
    )(lq1, lk1, lq2, lk2, subln_g, qk_b, qk_b, vt)


def _merge_kernel(oa_ref, ob_ref, wa_ref, wb_ref, ga_ref, gb_ref, o_ref):
    pa = _dot(oa_ref[...], wa_ref[...])
    pb = _dot(ob_ref[...], wb_ref[...])
    o_ref[...] = (ga_ref[...].astype(F32) * pa + gb_ref[...].astype(F32) * pb).astype(o_ref.dtype)


def _merge(o_a, o_b, w_pa, w_pb, layer, gates, tm=1024, tn=1024):
    t = o_a.shape[0]
    return pl.pallas_call(
        _merge_kernel,
        grid=(t // tm, D_MODEL // tn),
        in_specs=[
            pl.BlockSpec((tm, W_QA), lambda i, j: (i, 0)),
            pl.BlockSpec((tm, W_QB), lambda i, j: (i, 0)),
            pl.BlockSpec((None, W_QA, tn), lambda i, j: (layer, 0, j)),
            pl.BlockSpec((None, W_QB, tn), lambda i, j: (layer, 0, j)),
            pl.BlockSpec((tm, tn), lambda i, j: (i, j)),
            pl.BlockSpec((tm, tn), lambda i, j: (i, D_MODEL // tn + j)),
        ],
        out_specs=pl.BlockSpec((tm, tn), lambda i, j: (i, j)),
        out_shape=jax.ShapeDtypeStruct((t, D_MODEL), BF16),
        compiler_params=_params("parallel", "arbitrary"),
        name="merge",
    )(o_a, o_b, w_pa, w_pb, gates, gates)


def _out_ln_kernel(m_ref, w_ref, x_ref, g_ref, b_ref, y_ref, yb_ref, *, rows):
    g, b = g_ref[...], b_ref[...]
    for r in range(m_ref.shape[0] // rows):
        rs = slice(r * rows, (r + 1) * rows)
        z = ALPHA * x_ref[rs, :] + _dot(m_ref[rs, :], w_ref[...])
        y = _layer_norm(z, g, b)
        y_ref[rs, :] = y
        yb_ref[rs, :] = y.astype(yb_ref.dtype)


def _out_ln(merged, w_out, layer, x, g, b, tm=512, rows=256):
    t = x.shape[0]
    row = pl.BlockSpec((tm, D_MODEL), lambda i: (i, 0))
    vec = pl.BlockSpec((1, D_MODEL), lambda i: (0, 0))
    return pl.pallas_call(
        functools.partial(_out_ln_kernel, rows=rows),
        grid=(t // tm,),
        in_specs=[row, pl.BlockSpec((None, D_MODEL, D_MODEL), lambda i: (layer, 0, 0)), row, vec, vec],
        out_specs=[row, row],
        out_shape=[jax.ShapeDtypeStruct((t, D_MODEL), F32), jax.ShapeDtypeStruct((t, D_MODEL), BF16)],
        compiler_params=_params("parallel"),
        name="out_ln",
    )(merged, w_out, x, g, b)


def _ffn_up_kernel(xp_ref, x_ref, wg_ref, wv_ref, cwg_ref, cwv_ref, cbg_ref, cbv_ref, o_ref,
                   *, tm, tn, tiles_per_seq):
    i = pl.program_id(0)
    halo = BF16_ROWS
    xp = jnp.where(i % tiles_per_seq == 0, jnp.zeros((), BF16), xp_ref[...])
    xe = jnp.concatenate([xp, x_ref[...]], axis=0)

    def conv(w_ref, cw_ref, cb_ref, cs):
        u = _dot(xe, w_ref[:, cs])
        out = cb_ref[:, cs] + cw_ref[CONV_WIDTH - 1:CONV_WIDTH, cs] * u[halo:]
        for back in range(1, CONV_WIDTH):
            tap = CONV_WIDTH - 1 - back
            out = out + cw_ref[tap:tap + 1, cs] * pltpu.roll(u, back, axis=0)[halo:]
        return out

    for c in range(tn // MXU_COLS):
        cs = slice(c * MXU_COLS, (c + 1) * MXU_COLS)
        gate = conv(wg_ref, cwg_ref, cbg_ref, cs)
        val = conv(wv_ref, cwv_ref, cbv_ref, cs)
        o_ref[:, cs] = (jax.nn.silu(gate) * val).astype(o_ref.dtype)


def _ffn_up(xb, w_up, layer, conv_w, conv_b, seq, tm=1024, tn=512):
    t = xb.shape[0]
    nf = D_FF // tn
    halo = BF16_ROWS
    return pl.pallas_call(
        functools.partial(_ffn_up_kernel, tm=tm, tn=tn, tiles_per_seq=seq // tm),
        grid=(t // tm, nf),
        in_specs=[
            pl.BlockSpec((halo, D_MODEL), lambda i, j: (jnp.maximum(i * (tm // halo) - 1, 0), 0)),
            pl.BlockSpec((tm, D_MODEL), lambda i, j: (i, 0)),
            pl.BlockSpec((None, D_MODEL, tn), lambda i, j: (layer, 0, j)),
            pl.BlockSpec((None, D_MODEL, tn), lambda i, j: (layer, 0, nf + j)),
            pl.BlockSpec((None, CONV_WIDTH, tn), lambda i, j: (layer, 0, j)),
            pl.BlockSpec((None, CONV_WIDTH, tn), lambda i, j: (layer, 0, nf + j)),
            pl.BlockSpec((None, 1, tn), lambda i, j: (layer, 0, j)),
            pl.BlockSpec((None, 1, tn), lambda i, j: (layer, 0, nf + j)),
        ],
        out_specs=pl.BlockSpec((tm, tn), lambda i, j: (i, j)),
        out_shape=jax.ShapeDtypeStruct((t, D_FF), BF16),
        compiler_params=_params("parallel", "arbitrary"),
        name="ffn_up",
    )(xb, xb, w_up, w_up, conv_w, conv_w, conv_b, conv_b)


def _down_ln_kernel(a_ref, w_ref, x_ref, g_ref, b_ref, *out_refs):
    z = ALPHA * x_ref[...] + _dot(a_ref[...], w_ref[...])
    y = _layer_norm(z, g_ref[...], b_ref[...])
    out_refs[0][...] = y
    if len(out_refs) > 1:
        out_refs[1][...] = y.astype(BF16)


def _down_ln(act, w_down, layer, x, g, b, with_bf16, tm=256):
    t = x.shape[0]
    row = pl.BlockSpec((tm, D_MODEL), lambda i: (i, 0))
    vec = pl.BlockSpec((1, D_MODEL), lambda i: (0, 0))
    out_specs = [row, row] if with_bf16 else [row]
    out_shape = [jax.ShapeDtypeStruct((t, D_MODEL), F32)]
    if with_bf16:
        out_shape.append(jax.ShapeDtypeStruct((t, D_MODEL), BF16))
    return pl.pallas_call(
        _down_ln_kernel,
        grid=(t // tm,),
        in_specs=[
            pl.BlockSpec((tm, D_FF), lambda i: (i, 0)),
            pl.BlockSpec((None, D_FF, D_MODEL), lambda i: (layer, 0, 0), pipeline_mode=pl.Buffered(1)),
            row, vec, vec,
        ],
        out_specs=out_specs,
        out_shape=out_shape,
        compiler_params=_params("parallel"),
        name="down_ln",
    )(act, w_down, x, g, b)


def kernel(x, w_in, sinks, lambda_q1, lambda_k1, lambda_q2, lambda_k2, subln_g, w_proj_a, w_proj_b,
           w_out, ln1_g, ln1_b, w_up, conv_w, conv_b, w_down, ln2_g, ln2_b):
    batch, seq, _ = x.shape
    t = batch * seq
    tab_a, tab_b = _rope_tables(seq)
    x = x.reshape(t, D_MODEL)
    xb = None
    row = lambda v: v.reshape(1, -1).astype(F32)
    cols = lambda lo, n: w_in[:, :, lo:lo + n].astype(BF16)
    w_a, w_qk = cols(0, W_A), cols(W_A, 2 * W_QB)
    w_v, w_g = cols(W_A + 2 * W_QB, W_QB), cols(W_A + 3 * W_QB, 2 * D_MODEL)
    w_pa, w_pb, w_o = w_proj_a.astype(BF16), w_proj_b.astype(BF16), w_out.astype(BF16)
    w_u, w_d = w_up.astype(BF16), w_down.astype(BF16)
    conv_w = conv_w.astype(F32)
    conv_b = conv_b.astype(F32).reshape(DEPTH, 1, 2 * D_FF)
    for l in range(DEPTH):
        if xb is None:
            qkv_a, xb = _proj_a(x, w_a, l, tab_a, seq)
        else:
            qkv_a, = _proj_a(xb, w_a, l, tab_a, seq)
        qk_b = _proj_qk_b(xb, w_qk, l, tab_b, seq)
        vt_b = _proj_vt_b(xb, w_v, l, batch, seq, ATTN_B_TK)
        gates = _proj_gates(xb, w_g, l)
        o_a = _attn_a(qkv_a, sinks[l].astype(F32), batch, seq)
        lam_init = 0.8 - 0.6 * math.exp(-0.3 * l)
        o_b = _attn_b(qk_b, vt_b, row(lambda_q1[l]), row(lambda_k1[l]), row(lambda_q2[l]), row(lambda_k2[l]),
                      subln_g[l].reshape(-1, 1).astype(F32), lam_init, batch, seq, tk=ATTN_B_TK)
        merged = _merge(o_a, o_b, w_pa, w_pb, l, gates)
        x, xb = _out_ln(merged, w_o, l, x, row(ln1_g[l]), row(ln1_b[l]))
        act = _ffn_up(xb, w_u, l, conv_w, conv_b, seq)
        last = l == DEPTH - 1
        outs = _down_ln(act, w_d, l, x, row(ln2_g[l]), row(ln2_b[l]), with_bf16=not last)
        x = outs[0]
        if not last:
            xb = outs[1]
    return x.reshape(batch, seq, D_MODEL)
```

```python
import functools
import math

import jax
import jax.numpy as jnp
from jax import lax
from jax.experimental import pallas as pl
from jax.experimental.pallas import tpu as pltpu

D_MODEL = 2048
DEPTH = 2
HEAD_DIM_A = 64
N_Q_A = 32
N_KV_A = 4
GROUP_A = N_Q_A // N_KV_A
WINDOW = 128
HEAD_DIM_B = 128
N_HEADS_B = D_MODEL // (2 * HEAD_DIM_B)
SUBLN_EPS = 1e-5
ROPE_THETA = 10000.0
D_FF = ((8 * D_MODEL // 3 + 255) // 256) * 256
CONV_WIDTH = 3
LN_EPS = 1e-5
ALPHA = (2 * DEPTH) ** 0.25
NEG_INF = -1e30
LOG2_E = math.log2(math.e)

W_QA = N_Q_A * HEAD_DIM_A
W_KA = N_KV_A * HEAD_DIM_A
W_A = W_QA + 2 * W_KA
W_QB = N_HEADS_B * 2 * HEAD_DIM_B

LANES = 128
MXU_COLS = 256
BF16_ROWS = 16
VMEM_LIMIT = 56 * 1024 * 1024
ATTN_B_TK = 512

F32 = jnp.float32
BF16 = jnp.bfloat16


def _params(*sem):
    return pltpu.CompilerParams(dimension_semantics=sem, vmem_limit_bytes=VMEM_LIMIT)


def _dot(a, b):
    return jnp.dot(a, b, preferred_element_type=F32)


def _dot_nt(a, b):
    return lax.dot_general(a, b, (((1,), (1,)), ((), ())), preferred_element_type=F32)


def _layer_norm(z, g, b):
    mu = jnp.mean(z, axis=-1, keepdims=True)
    zc = z - mu
    var = jnp.mean(zc * zc, axis=-1, keepdims=True)
    return zc * lax.rsqrt(var + LN_EPS) * g + b


def _rope_tables(seq):
    def cs(dim):
        inv = 1.0 / (ROPE_THETA ** (jnp.arange(0, dim, 2, dtype=F32) / dim))
        ang = jnp.arange(seq, dtype=F32)[:, None] * inv[None, :]
        return jnp.cos(ang), jnp.sin(ang)

    cos_a, sin_a = cs(HEAD_DIM_A)
    cos_b, sin_b = cs(HEAD_DIM_B)
    zeros_a = jnp.zeros_like(sin_a)
    tab_a = jnp.stack([
        jnp.tile(cos_a, (1, 4)),
        jnp.tile(jnp.concatenate([zeros_a, sin_a], axis=1), (1, 2)),
        jnp.tile(jnp.concatenate([-sin_a, zeros_a], axis=1), (1, 2)),
    ])
    tab_b = jnp.stack([
        jnp.tile(cos_b, (1, 2)),
        jnp.concatenate([-sin_b, sin_b], axis=1),
    ])
    return tab_a, tab_b


def _proj_a_kernel(x_ref, w_ref, tab_ref, o_ref, *xb_refs):
    x = x_ref[...].astype(BF16)
    if xb_refs:
        xb_refs[0][...] = x
    acc = _dot(x, w_ref[...])
    cos, hi, lo = tab_ref[0], tab_ref[1], tab_ref[2]
    scale = LOG2_E * HEAD_DIM_A ** -0.5
    n_rope = (W_QA + W_KA) // LANES
    for g in range(W_A // LANES):
        xg = acc[:, g * LANES:(g + 1) * LANES]
        if g < n_rope:
            xg = xg * cos + pltpu.roll(xg, 32, axis=1) * hi + pltpu.roll(xg, 96, axis=1) * lo
            if g < W_QA // LANES:
                xg = xg * scale
        o_ref[:, g * LANES:(g + 1) * LANES] = xg.astype(o_ref.dtype)


def _proj_a(x, w_a, layer, tab_a, seq, tm=512):
    t = x.shape[0]
    out_specs = [pl.BlockSpec((tm, W_A), lambda i: (i, 0))]
    out_shape = [jax.ShapeDtypeStruct((t, W_A), BF16)]
    if x.dtype != BF16:
        out_specs.append(pl.BlockSpec((tm, D_MODEL), lambda i: (i, 0)))
        out_shape.append(jax.ShapeDtypeStruct((t, D_MODEL), BF16))
    return pl.pallas_call(
        _proj_a_kernel,
        grid=(t // tm,),
        in_specs=[
            pl.BlockSpec((tm, D_MODEL), lambda i: (i, 0)),
            pl.BlockSpec((None, D_MODEL, W_A), lambda i: (layer, 0, 0)),
            pl.BlockSpec((3, tm, LANES), lambda i: (0, i % (seq // tm), 0)),
        ],
        out_specs=out_specs,
        out_shape=out_shape,
        compiler_params=_params("parallel"),
        name="proj_a",
    )(x, w_a, tab_a)


def _proj_rope_b_kernel(x_ref, w_ref, tab_ref, o_ref, *, tn, n_scaled):
    j = pl.program_id(1)
    scale = jnp.where(j < n_scaled, LOG2_E * HEAD_DIM_B ** -0.5, 1.0)
    cos = tab_ref[0] * scale
    sgn = tab_ref[1] * scale
    x = x_ref[...]
    for c in range(tn // MXU_COLS):
        acc = _dot(x, w_ref[:, c * MXU_COLS:(c + 1) * MXU_COLS])
        for g in range(MXU_COLS // LANES):
            xg = acc[:, g * LANES:(g + 1) * LANES]
            lo = c * MXU_COLS + g * LANES
            o_ref[:, lo:lo + LANES] = (xg * cos + pltpu.roll(xg, 64, axis=1) * sgn).astype(o_ref.dtype)


def _proj_qk_b(xb, w_qk, layer, tab_b, seq, tm=1024, tn=2048):
    t = xb.shape[0]
    return pl.pallas_call(
        functools.partial(_proj_rope_b_kernel, tn=tn, n_scaled=W_QB // tn),
        grid=(t // tm, 2 * W_QB // tn),
        in_specs=[
            pl.BlockSpec((tm, D_MODEL), lambda i, j: (i, 0)),
            pl.BlockSpec((None, D_MODEL, tn), lambda i, j: (layer, 0, j)),
            pl.BlockSpec((2, tm, LANES), lambda i, j: (0, i % (seq // tm), 0)),
        ],
        out_specs=pl.BlockSpec((tm, tn), lambda i, j: (i, j)),
        out_shape=jax.ShapeDtypeStruct((t, 2 * W_QB), BF16),
        compiler_params=_params("parallel", "arbitrary"),
        name="proj_qk_b",
    )(xb, w_qk, tab_b)


def _proj_gates_kernel(x_ref, w_ref, o_ref, *, tn):
    x = x_ref[...]
    for c in range(tn // MXU_COLS):
        cs = slice(c * MXU_COLS, (c + 1) * MXU_COLS)
        o_ref[:, cs] = jax.nn.sigmoid(_dot(x, w_ref[:, cs])).astype(o_ref.dtype)


def _proj_gates(xb, w_g, layer, tm=1024, tn=2048):
    t = xb.shape[0]
    n = w_g.shape[-1]
    return pl.pallas_call(
        functools.partial(_proj_gates_kernel, tn=tn),
        grid=(t // tm, n // tn),
        in_specs=[
            pl.BlockSpec((tm, D_MODEL), lambda i, j: (i, 0)),
            pl.BlockSpec((None, D_MODEL, tn), lambda i, j: (layer, 0, j)),
        ],
        out_specs=pl.BlockSpec((tm, tn), lambda i, j: (i, j)),
        out_shape=jax.ShapeDtypeStruct((t, n), BF16),
        compiler_params=_params("parallel", "arbitrary"),
        name="proj_gates",
    )(xb, w_g)


def _attn_a_block(sinks_ref, q_ref, kvp_ref, kvc_ref, o_ref, has_prev):
    blk = WINDOW
    hd = HEAD_DIM_A
    key = lax.broadcasted_iota(jnp.int32, (2 * blk, blk), 0)
    qry = lax.broadcasted_iota(jnp.int32, (2 * blk, blk), 1)
    mask = (key <= blk + qry) & (key > qry)
    if has_prev is not True:
        mask = mask & ((key >= blk) | has_prev)
    vpt = kvp_ref[:, W_KA:].T
    vct = kvc_ref[:, W_KA:].T

    for g in range(N_KV_A):
        ks = slice(g * hd, (g + 1) * hd)
        kband = jnp.concatenate([kvp_ref[:, ks], kvc_ref[:, ks]], axis=0)
        vt = jnp.concatenate([vpt[ks], vct[ks]], axis=1)
        qts = []
        for pr in range(GROUP_A // 2):
            h0 = g * GROUP_A + 2 * pr
            t = q_ref[:, h0 * hd:(h0 + 2) * hd].T
            qts += [t[:hd], t[hd:]]
        qt = jnp.concatenate(qts, axis=1)
        s = _dot(kband, qt)
        ps, rs = [], []
        for hh in range(GROUP_A):
            sink = sinks_ref[g * GROUP_A + hh] * LOG2_E
            sh = jnp.where(mask, s[:, hh * blk:(hh + 1) * blk], NEG_INF)
            m = jnp.maximum(jnp.max(sh, axis=0, keepdims=True), sink)
            p = jnp.exp2(sh - m)
            denom = jnp.sum(p, axis=0, keepdims=True) + jnp.exp2(sink - m)
            ps.append(p.astype(BF16))
            rs.append(1.0 / denom)
        ot = _dot(vt, jnp.concatenate(ps, axis=1))
        for pr in range(GROUP_A // 2):
            h0 = g * GROUP_A + 2 * pr
            pair = jnp.concatenate([ot[:, (2 * pr + e) * blk:(2 * pr + e + 1) * blk] * rs[2 * pr + e]
                                    for e in range(2)], axis=0)
            o_ref[:, h0 * hd:(h0 + 2) * hd] = pair.astype(o_ref.dtype).T


def _vt_attn_a_kernel(sinks_ref, x_ref, w_ref, q_ref, kvp_ref, kvc_ref, vt_ref, o_ref,
                      *, tk, blocks_per_seq):
    blk = WINDOW
    n_blocks = o_ref.shape[0] // blk
    first = (pl.program_id(0) * pl.num_programs(1) + pl.program_id(1)) * n_blocks
    x = x_ref[...]
    w = 2 * HEAD_DIM_B
    for c in range(max(vt_ref.shape[0], n_blocks)):
        if c < vt_ref.shape[0]:
            acc = _dot(x, w_ref[:, c * w:(c + 1) * w]).astype(vt_ref.dtype)
            for kk in range(vt_ref.shape[1]):
                vt_ref[c, kk] = acc[kk * tk:(kk + 1) * tk].T
        if c < n_blocks:
            rows = pl.ds(c * blk, blk)
            prev = kvp_ref if c == 0 else kvc_ref.at[pl.ds((c - 1) * blk, blk)]
            has_prev = (first % blocks_per_seq != 0) if c == 0 else True
            _attn_a_block(sinks_ref, q_ref.at[rows], prev, kvc_ref.at[rows], o_ref.at[rows], has_prev)


def _vt_attn_a(xb, w_v, layer, qkv_a, sinks, batch, seq, tk, tm=1024, heads=4, n_blocks=4):
    t = xb.shape[0]
    blk = WINDOW
    w = 2 * HEAD_DIM_B
    tiles_per_seq = seq // tm
    nj = N_HEADS_B // heads
    assert (t // tm) * nj * n_blocks * blk == t and (seq // blk) % n_blocks == 0
    kv_col = W_QA // (2 * W_KA)
    step = lambda i, j: i * nj + j
    return pl.pallas_call(
        functools.partial(_vt_attn_a_kernel, tk=tk, blocks_per_seq=seq // blk),
        grid=(t // tm, nj),
        in_specs=[
            pl.BlockSpec(memory_space=pltpu.SMEM),
            pl.BlockSpec((tm, D_MODEL), lambda i, j: (i, 0)),
            pl.BlockSpec((None, D_MODEL, heads * w), lambda i, j: (layer, 0, j)),
            pl.BlockSpec((n_blocks * blk, W_QA), lambda i, j: (step(i, j), 0)),
            pl.BlockSpec((blk, 2 * W_KA), lambda i, j: (jnp.maximum(step(i, j) * n_blocks - 1, 0), kv_col)),
            pl.BlockSpec((n_blocks * blk, 2 * W_KA), lambda i, j: (step(i, j), kv_col)),
        ],
        out_specs=[
            pl.BlockSpec((None, heads, tm // tk, w, tk),
                         lambda i, j: (i // tiles_per_seq, j, i % tiles_per_seq, 0, 0)),
            pl.BlockSpec((n_blocks * blk, W_QA), lambda i, j: (step(i, j), 0)),
        ],
        out_shape=[
            jax.ShapeDtypeStruct((batch, N_HEADS_B, seq // tk, w, tk), BF16),
            jax.ShapeDtypeStruct((t, W_QA), BF16),
        ],
        compiler_params=_params("parallel", "arbitrary"),
        name="vt_attn_a",
    )(sinks, xb, w_v, qkv_a, qkv_a, qkv_a)


def _attn_b_kernel(lq1_ref, lk1_ref, lq2_ref, lk2_ref, g_ref, q_ref, k_ref, vt_ref, o_ref,
                   qt_ref, m_ref, l_ref, acc_ref, *, tq, tk, n_q_tiles, lam_init):
    qi = pl.program_id(2)
    hd = HEAD_DIM_B
    cg = MXU_COLS
    n_groups = tq // cg
    for c in range(2):
        qt_ref[c] = q_ref[:, c * hd:(c + 1) * hd].T

    def chain(c, j, k, vt, first_key, start):
        js = slice(j * cg, (j + 1) * cg)
        s = _dot(k[:, c * hd:(c + 1) * hd], qt_ref[c, :, js])
        if first_key is not None:
            key = lax.broadcasted_iota(jnp.int32, s.shape, 0)
            qry = lax.broadcasted_iota(jnp.int32, s.shape, 1)
            s = jnp.where(key + first_key <= qry, s, NEG_INF)
        m_new = jnp.max(s, axis=0, keepdims=True)
        if not start:
            m_old = m_ref[c, :, js]
            m_new = jnp.maximum(m_old, m_new)
            a = jnp.exp2(m_old - m_new)
        p = jnp.exp2(s - m_new)
        l_new = jnp.sum(p, axis=0, keepdims=True)
        acc_new = _dot(vt, p.astype(BF16))
        if not start:
            l_new = a * l_ref[c, :, js] + l_new
            acc_new = a * acc_ref[c, :, js] + acc_new
        l_ref[c, :, js] = l_new
        acc_ref[c, :, js] = acc_new
        m_ref[c, :, js] = m_new

    blocks_per_tile = tq // tk

    def finish(j):
        js = slice(j * cg, (j + 1) * cg)
        o = acc_ref[0, :, js] * (1.0 / l_ref[0, :, js]) - lam * (acc_ref[1, :, js] * (1.0 / l_ref[1, :, js]))
        o = o * lax.rsqrt(jnp.mean(o * o, axis=0, keepdims=True) + SUBLN_EPS)
        o = o * g_ref[...] * (1.0 - lam_init)
        o_ref[js, :] = o.T.astype(o_ref.dtype)

    def tile(q_tile):
        started = set()

        def run(c, j, k, vt, first_key):
            chain(c, j, k, vt, first_key, (c, j) not in started)
            started.add((c, j))

        for kb in range(q_tile * blocks_per_tile):
            k = k_ref[kb * tk:(kb + 1) * tk, :]
            vt = vt_ref[kb]
            for j in range(n_groups):
                for c in range(2):
                    run(c, j, k, vt, None)
        for u in range(blocks_per_tile):
            kb = q_tile * blocks_per_tile + u
            k = k_ref[kb * tk:(kb + 1) * tk, :]
            vt = vt_ref[kb]
            for j in range(n_groups):
                first_key = u * tk - j * cg
                n_vis = min(tk, (j + 1) * cg - u * tk)
                if n_vis <= 0:
                    continue
                masked = first_key + n_vis - 1 > 0
                for c in range(2):
                    run(c, j, k[:n_vis], vt[:, :n_vis], first_key if masked else None)
        for j in range(n_groups):
            finish(j)

    lam = (jnp.exp(jnp.sum(lq1_ref[...] * lk1_ref[...], axis=-1, keepdims=True))
           - jnp.exp(jnp.sum(lq2_ref[...] * lk2_ref[...], axis=-1, keepdims=True))
           + lam_init)
    for q_tile in range(n_q_tiles):
        pl.when(qi == q_tile)(functools.partial(tile, q_tile))


def _attn_b(qk_b, vt, lq1, lk1, lq2, lk2, subln_g, lam_init, batch, seq, tq=1024, tk=512):
    nq = seq // tq
    w = 2 * HEAD_DIM_B
    vec = lambda n: pl.BlockSpec((1, n), lambda b, h, i: (0, 0))
    return pl.pallas_call(
        functools.partial(_attn_b_kernel, tq=tq, tk=tk, n_q_tiles=nq, lam_init=lam_init),
        grid=(batch, N_HEADS_B, nq),
        in_specs=[
            vec(HEAD_DIM_B), vec(HEAD_DIM_B), vec(HEAD_DIM_B), vec(HEAD_DIM_B),
            pl.BlockSpec((w, 1), lambda b, h, i: (0, 0)),
            pl.BlockSpec((tq, w), lambda b, h, i: (b * nq + i, h)),
            pl.BlockSpec((seq, w), lambda b, h, i: (b, N_HEADS_B + h)),
            pl.BlockSpec((None, None, seq // tk, w, tk), lambda b, h, i: (b, h, 0, 0, 0)),
        ],
        out_specs=pl.BlockSpec((tq, w), lambda b, h, i: (b * nq + i, h)),
        out_shape=jax.ShapeDtypeStruct((batch * seq, W_QB), BF16),
        scratch_shapes=[
            pltpu.VMEM((2, HEAD_DIM_B, tq), BF16),
            pltpu.VMEM((2, 1, tq), F32),
            pltpu.VMEM((2, 1, tq), F32),
            pltpu.VMEM((2, w, tq), F32),
        ],
        compiler_params=_params("parallel", "parallel", "arbitrary"),
        name="attn_b",
    )(lq1, lk1, lq2, lk2, subln_g, qk_b, qk_b, vt)


def _merge_kernel(oa_ref, ob_ref, wa_ref, wb_ref, ga_ref, gb_ref, o_ref):
    pa = _dot(oa_ref[...], wa_ref[...])
    pb = _dot(ob_ref[...], wb_ref[...])
    o_ref[...] = (ga_ref[...].astype(F32) * pa + gb_ref[...].astype(F32) * pb).astype(o_ref.dtype)


def _merge(o_a, o_b, w_pa, w_pb, layer, gates, tm=1024, tn=1024):
    t = o_a.shape[0]
    return pl.pallas_call(
        _merge_kernel,
        grid=(t // tm, D_MODEL // tn),
        in_specs=[
            pl.BlockSpec((tm, W_QA), lambda i, j: (i, 0)),
            pl.BlockSpec((tm, W_QB), lambda i, j: (i, 0)),
            pl.BlockSpec((None, W_QA, tn), lambda i, j: (layer, 0, j)),
            pl.BlockSpec((None, W_QB, tn), lambda i, j: (layer, 0, j)),
            pl.BlockSpec((tm, tn), lambda i, j: (i, j)),
            pl.BlockSpec((tm, tn), lambda i, j: (i, D_MODEL // tn + j)),
        ],
        out_specs=pl.BlockSpec((tm, tn), lambda i, j: (i, j)),
        out_shape=jax.ShapeDtypeStruct((t, D_MODEL), BF16),
        compiler_params=_params("parallel", "arbitrary"),
        name="merge",
    )(o_a, o_b, w_pa, w_pb, gates, gates)


def _out_ln_kernel(m_ref, w_ref, x_ref, g_ref, b_ref, y_ref, yb_ref, *, rows):
    g, b = g_ref[...], b_ref[...]
    for r in range(m_ref.shape[0] // rows):
        rs = slice(r * rows, (r + 1) * rows)
        z = ALPHA * x_ref[rs, :] + _dot(m_ref[rs, :], w_ref[...])
        y = _layer_norm(z, g, b)
        y_ref[rs, :] = y
        yb_ref[rs, :] = y.astype(yb_ref.dtype)


def _out_ln(merged, w_out, layer, x, g, b, tm=512, rows=256):
    t = x.shape[0]
    row = pl.BlockSpec((tm, D_MODEL), lambda i: (i, 0))
    vec = pl.BlockSpec((1, D_MODEL), lambda i: (0, 0))
    return pl.pallas_call(
        functools.partial(_out_ln_kernel, rows=rows),
        grid=(t // tm,),
        in_specs=[row, pl.BlockSpec((None, D_MODEL, D_MODEL), lambda i: (layer, 0, 0)), row, vec, vec],
        out_specs=[row, row],
        out_shape=[jax.ShapeDtypeStruct((t, D_MODEL), F32), jax.ShapeDtypeStruct((t, D_MODEL), BF16)],
        compiler_params=_params("parallel"),
        name="out_ln",
    )(merged, w_out, x, g, b)


def _ffn_up_kernel(xp_ref, x_ref, wg_ref, wv_ref, cwg_ref, cwv_ref, cbg_ref, cbv_ref, o_ref,
                   *, tm, tn, tiles_per_seq):
    i = pl.program_id(1)
    halo = BF16_ROWS
    xp = jnp.where(i % tiles_per_seq == 0, jnp.zeros((), BF16), xp_ref[...])
    xe = jnp.concatenate([xp, x_ref[...]], axis=0)

    def conv(w_ref, cw_ref, cb_ref, cs):
        u = _dot(xe, w_ref[:, cs])
        out = cb_ref[:, cs] + cw_ref[CONV_WIDTH - 1:CONV_WIDTH, cs] * u[halo:]
        for back in range(1, CONV_WIDTH):
            tap = CONV_WIDTH - 1 - back
            out = out + cw_ref[tap:tap + 1, cs] * pltpu.roll(u, back, axis=0)[halo:]
        return out

    for c in range(tn // MXU_COLS):
        cs = slice(c * MXU_COLS, (c + 1) * MXU_COLS)
        gate = conv(wg_ref, cwg_ref, cbg_ref, cs)
        val = conv(wv_ref, cwv_ref, cbv_ref, cs)
        o_ref[:, cs] = (jax.nn.silu(gate) * val).astype(o_ref.dtype)


def _ffn_up(xb, w_up, layer, conv_w, conv_b, seq, tm=1024, tn=D_FF // 2):
    t = xb.shape[0]
    nf = D_FF // tn
    halo = BF16_ROWS
    resident = pl.Buffered(1)
    return pl.pallas_call(
        functools.partial(_ffn_up_kernel, tm=tm, tn=tn, tiles_per_seq=seq // tm),
        grid=(nf, t // tm),
        in_specs=[
            pl.BlockSpec((halo, D_MODEL), lambda j, i: (jnp.maximum(i * (tm // halo) - 1, 0), 0)),
            pl.BlockSpec((tm, D_MODEL), lambda j, i: (i, 0)),
            pl.BlockSpec((None, D_MODEL, tn), lambda j, i: (layer, 0, j), pipeline_mode=resident),
            pl.BlockSpec((None, D_MODEL, tn), lambda j, i: (layer, 0, nf + j), pipeline_mode=resident),
            pl.BlockSpec((None, CONV_WIDTH, tn), lambda j, i: (layer, 0, j)),
            pl.BlockSpec((None, CONV_WIDTH, tn), lambda j, i: (layer, 0, nf + j)),
            pl.BlockSpec((None, 1, tn), lambda j, i: (layer, 0, j)),
            pl.BlockSpec((None, 1, tn), lambda j, i: (layer, 0, nf + j)),
        ],
        out_specs=pl.BlockSpec((tm, tn), lambda j, i: (i, j)),
        out_shape=jax.ShapeDtypeStruct((t, D_FF), BF16),
        compiler_params=_params("arbitrary", "parallel"),
        name="ffn_up",
    )(xb, xb, w_up, w_up, conv_w, conv_w, conv_b, conv_b)


def _down_ln_kernel(a_ref, w_ref, x_ref, g_ref, b_ref, *out_refs):
    z = ALPHA * x_ref[...] + _dot(a_ref[...], w_ref[...])
    y = _layer_norm(z, g_ref[...], b_ref[...])
    out_refs[0][...] = y
    if len(out_refs) > 1:
        out_refs[1][...] = y.astype(BF16)


def _down_ln(act, w_down, layer, x, g, b, with_bf16, tm=256):
    t = x.shape[0]
    row = pl.BlockSpec((tm, D_MODEL), lambda i: (i, 0))
    vec = pl.BlockSpec((1, D_MODEL), lambda i: (0, 0))
    out_specs = [row, row] if with_bf16 else [row]
    out_shape = [jax.ShapeDtypeStruct((t, D_MODEL), F32)]
    if with_bf16:
        out_shape.append(jax.ShapeDtypeStruct((t, D_MODEL), BF16))
    return pl.pallas_call(
        _down_ln_kernel,
        grid=(t // tm,),
        in_specs=[
            pl.BlockSpec((tm, D_FF), lambda i: (i, 0)),
            pl.BlockSpec((None, D_FF, D_MODEL), lambda i: (layer, 0, 0), pipeline_mode=pl.Buffered(1)),
            row, vec, vec,
        ],
        out_specs=out_specs,
        out_shape=out_shape,
        compiler_params=_params("parallel"),
        name="down_ln",
    )(act, w_down, x, g, b)


def kernel(x, w_in, sinks, lambda_q1, lambda_k1, lambda_q2, lambda_k2, subln_g, w_proj_a, w_proj_b,
           w_out, ln1_g, ln1_b, w_up, conv_w, conv_b, w_down, ln2_g, ln2_b):
    batch, seq, _ = x.shape
    t = batch * seq
    tab_a, tab_b = _rope_tables(seq)
    x = x.reshape(t, D_MODEL)
    xb = None
    row = lambda v: v.reshape(1, -1).astype(F32)
    cols = lambda lo, n: w_in[:, :, lo:lo + n].astype(BF16)
    w_a, w_qk = cols(0, W_A), cols(W_A, 2 * W_QB)
    w_v, w_g = cols(W_A + 2 * W_QB, W_QB), cols(W_A + 3 * W_QB, 2 * D_MODEL)
    w_pa, w_pb, w_o = w_proj_a.astype(BF16), w_proj_b.astype(BF16), w_out.astype(BF16)
    w_u, w_d = w_up.astype(BF16), w_down.astype(BF16)
    conv_w = conv_w.astype(F32)
    conv_b = conv_b.astype(F32).reshape(DEPTH, 1, 2 * D_FF)
    for l in range(DEPTH):
        if xb is None:
            qkv_a, xb = _proj_a(x, w_a, l, tab_a, seq)
        else:
            qkv_a, = _proj_a(xb, w_a, l, tab_a, seq)
        qk_b = _proj_qk_b(xb, w_qk, l, tab_b, seq)
        vt_b, o_a = _vt_attn_a(xb, w_v, l, qkv_a, sinks[l].astype(F32), batch, seq, ATTN_B_TK)
        gates = _proj_gates(xb, w_g, l)
        lam_init = 0.8 - 0.6 * math.exp(-0.3 * l)
        o_b = _attn_b(qk_b, vt_b, row(lambda_q1[l]), row(lambda_k1[l]), row(lambda_q2[l]), row(lambda_k2[l]),
                      subln_g[l].reshape(-1, 1).astype(F32), lam_init, batch, seq, tk=ATTN_B_TK)
        merged = _merge(o_a, o_b, w_pa, w_pb, l, gates)
        x, xb = _out_ln(merged, w_o, l, x, row(ln1_g[l]), row(ln1_b[l]))
        act = _ffn_up(xb, w_u, l, conv_w, conv_b, seq)
        last = l == DEPTH - 1
        outs = _down_ln(act, w_d, l, x, row(ln2_g[l]), row(ln2_b[l]), with_bf16=not last)
        x = outs[0]
        if not last:
            xb = outs[1]
    return x.reshape(batch, seq, D_MODEL)
```

```python
import functools
import math

import jax
import jax.numpy as jnp
from jax import lax
from jax.experimental import pallas as pl
from jax.experimental.pallas import tpu as pltpu

D_MODEL = 2048
DEPTH = 2
HEAD_DIM_A = 64
N_Q_A = 32
N_KV_A = 4
GROUP_A = N_Q_A // N_KV_A
WINDOW = 128
HEAD_DIM_B = 128
N_HEADS_B = D_MODEL // (2 * HEAD_DIM_B)
SUBLN_EPS = 1e-5
ROPE_THETA = 10000.0
D_FF = ((8 * D_MODEL // 3 + 255) // 256) * 256
CONV_WIDTH = 3
LN_EPS = 1e-5
ALPHA = (2 * DEPTH) ** 0.25
NEG_INF = -1e30
LOG2_E = math.log2(math.e)

W_QA = N_Q_A * HEAD_DIM_A
W_KA = N_KV_A * HEAD_DIM_A
W_A = W_QA + 2 * W_KA
W_QB = N_HEADS_B * 2 * HEAD_DIM_B

LANES = 128
MXU_COLS = 256
BF16_ROWS = 16
VMEM_LIMIT = 56 * 1024 * 1024
ATTN_B_TK = 512

F32 = jnp.float32
BF16 = jnp.bfloat16


def _params(*sem):
    return pltpu.CompilerParams(dimension_semantics=sem, vmem_limit_bytes=VMEM_LIMIT)


def _dot(a, b):
    return jnp.dot(a, b, preferred_element_type=F32)


def _dot_nt(a, b):
    return lax.dot_general(a, b, (((1,), (1,)), ((), ())), preferred_element_type=F32)


def _layer_norm(z, g, b):
    mu = jnp.mean(z, axis=-1, keepdims=True)
    zc = z - mu
    var = jnp.mean(zc * zc, axis=-1, keepdims=True)
    return zc * lax.rsqrt(var + LN_EPS) * g + b


def _rope_tables(seq):
    def cs(dim):
        inv = 1.0 / (ROPE_THETA ** (jnp.arange(0, dim, 2, dtype=F32) / dim))
        ang = jnp.arange(seq, dtype=F32)[:, None] * inv[None, :]
        return jnp.cos(ang), jnp.sin(ang)

    cos_a, sin_a = cs(HEAD_DIM_A)
    cos_b, sin_b = cs(HEAD_DIM_B)
    zeros_a = jnp.zeros_like(sin_a)
    tab_a = jnp.stack([
        jnp.tile(cos_a, (1, 4)),
        jnp.tile(jnp.concatenate([zeros_a, sin_a], axis=1), (1, 2)),
        jnp.tile(jnp.concatenate([-sin_a, zeros_a], axis=1), (1, 2)),
    ])
    tab_b = jnp.stack([
        jnp.tile(cos_b, (1, 2)),
        jnp.concatenate([-sin_b, sin_b], axis=1),
    ])
    return tab_a, tab_b


def _proj_a_kernel(x_ref, w_ref, tab_ref, o_ref, *xb_refs):
    x = x_ref[...].astype(BF16)
    if xb_refs:
        xb_refs[0][...] = x
    acc = _dot(x, w_ref[...])
    cos, hi, lo = tab_ref[0], tab_ref[1], tab_ref[2]
    scale = LOG2_E * HEAD_DIM_A ** -0.5
    n_rope = (W_QA + W_KA) // LANES
    for g in range(W_A // LANES):
        xg = acc[:, g * LANES:(g + 1) * LANES]
        if g < n_rope:
            xg = xg * cos + pltpu.roll(xg, 32, axis=1) * hi + pltpu.roll(xg, 96, axis=1) * lo
            if g < W_QA // LANES:
                xg = xg * scale
        o_ref[:, g * LANES:(g + 1) * LANES] = xg.astype(o_ref.dtype)


def _proj_a(x, w_a, layer, tab_a, seq, tm=512):
    t = x.shape[0]
    out_specs = [pl.BlockSpec((tm, W_A), lambda i: (i, 0))]
    out_shape = [jax.ShapeDtypeStruct((t, W_A), BF16)]
    if x.dtype != BF16:
        out_specs.append(pl.BlockSpec((tm, D_MODEL), lambda i: (i, 0)))
        out_shape.append(jax.ShapeDtypeStruct((t, D_MODEL), BF16))
    return pl.pallas_call(
        _proj_a_kernel,
        grid=(t // tm,),
        in_specs=[
            pl.BlockSpec((tm, D_MODEL), lambda i: (i, 0)),
            pl.BlockSpec((D_MODEL, W_A), lambda i: (layer, 0)),
            pl.BlockSpec((3, tm, LANES), lambda i: (0, i % (seq // tm), 0)),
        ],
        out_specs=out_specs,
        out_shape=out_shape,
        compiler_params=_params("parallel"),
        name="proj_a",
    )(x, w_a, tab_a)


def _proj_rope_b_kernel(x_ref, w_ref, tab_ref, o_ref, *, tn, n_scaled):
    j = pl.program_id(1)
    scale = jnp.where(j < n_scaled, LOG2_E * HEAD_DIM_B ** -0.5, 1.0)
    cos = tab_ref[0] * scale
    sgn = tab_ref[1] * scale
    x = x_ref[...]
    for c in range(tn // MXU_COLS):
        acc = _dot(x, w_ref[:, c * MXU_COLS:(c + 1) * MXU_COLS])
        for g in range(MXU_COLS // LANES):
            xg = acc[:, g * LANES:(g + 1) * LANES]
            lo = c * MXU_COLS + g * LANES
            o_ref[:, lo:lo + LANES] = (xg * cos + pltpu.roll(xg, 64, axis=1) * sgn).astype(o_ref.dtype)


def _w_in_cols(layer, col0, width):
    return pl.BlockSpec((pl.Element(D_MODEL), pl.Element(width)),
                        lambda i, j: (layer * D_MODEL, pl.multiple_of(col0 + j * width, LANES)))


def _proj_qk_b(xb, w_in_b, layer, tab_b, seq, tm=1024, tn=2048):
    t = xb.shape[0]
    return pl.pallas_call(
        functools.partial(_proj_rope_b_kernel, tn=tn, n_scaled=W_QB // tn),
        grid=(t // tm, 2 * W_QB // tn),
        in_specs=[
            pl.BlockSpec((tm, D_MODEL), lambda i, j: (i, 0)),
            _w_in_cols(layer, W_A, tn),
            pl.BlockSpec((2, tm, LANES), lambda i, j: (0, i % (seq // tm), 0)),
        ],
        out_specs=pl.BlockSpec((tm, tn), lambda i, j: (i, j)),
        out_shape=jax.ShapeDtypeStruct((t, 2 * W_QB), BF16),
        compiler_params=_params("parallel", "arbitrary"),
        name="proj_qk_b",
    )(xb, w_in_b, tab_b)


def _proj_gates_kernel(x_ref, w_ref, o_ref, *, tn):
    x = x_ref[...]
    for c in range(tn // MXU_COLS):
        cs = slice(c * MXU_COLS, (c + 1) * MXU_COLS)
        o_ref[:, cs] = jax.nn.sigmoid(_dot(x, w_ref[:, cs])).astype(o_ref.dtype)


def _proj_gates(xb, w_in_b, layer, tm=1024, tn=2048):
    t = xb.shape[0]
    n = 2 * D_MODEL
    return pl.pallas_call(
        functools.partial(_proj_gates_kernel, tn=tn),
        grid=(t // tm, n // tn),
        in_specs=[
            pl.BlockSpec((tm, D_MODEL), lambda i, j: (i, 0)),
            _w_in_cols(layer, W_A + 3 * W_QB, tn),
        ],
        out_specs=pl.BlockSpec((tm, tn), lambda i, j: (i, j)),
        out_shape=jax.ShapeDtypeStruct((t, n), BF16),
        compiler_params=_params("parallel", "arbitrary"),
        name="proj_gates",
    )(xb, w_in_b)


def _attn_a_block(sinks_ref, q_ref, kvp_ref, kvc_ref, o_ref, has_prev):
    blk = WINDOW
    hd = HEAD_DIM_A
    key = lax.broadcasted_iota(jnp.int32, (2 * blk, blk), 0)
    qry = lax.broadcasted_iota(jnp.int32, (2 * blk, blk), 1)
    mask = (key <= blk + qry) & (key > qry)
    if has_prev is not True:
        mask = mask & ((key >= blk) | has_prev)
    vpt = kvp_ref[:, W_KA:].T
    vct = kvc_ref[:, W_KA:].T

    for g in range(N_KV_A):
        ks = slice(g * hd, (g + 1) * hd)
        kband = jnp.concatenate([kvp_ref[:, ks], kvc_ref[:, ks]], axis=0)
        vt = jnp.concatenate([vpt[ks], vct[ks]], axis=1)
        qts = []
        for pr in range(GROUP_A // 2):
            h0 = g * GROUP_A + 2 * pr
            t = q_ref[:, h0 * hd:(h0 + 2) * hd].T
            qts += [t[:hd], t[hd:]]
        qt = jnp.concatenate(qts, axis=1)
        s = _dot(kband, qt)
        ps, rs = [], []
        for hh in range(GROUP_A):
            sink = sinks_ref[g * GROUP_A + hh] * LOG2_E
            sh = jnp.where(mask, s[:, hh * blk:(hh + 1) * blk], NEG_INF)
            m = jnp.maximum(jnp.max(sh, axis=0, keepdims=True), sink)
            p = jnp.exp2(sh - m)
            denom = jnp.sum(p, axis=0, keepdims=True) + jnp.exp2(sink - m)
            ps.append(p.astype(BF16))
            rs.append(1.0 / denom)
        ot = _dot(vt, jnp.concatenate(ps, axis=1))
        for pr in range(GROUP_A // 2):
            h0 = g * GROUP_A + 2 * pr
            pair = jnp.concatenate([ot[:, (2 * pr + e) * blk:(2 * pr + e + 1) * blk] * rs[2 * pr + e]
                                    for e in range(2)], axis=0)
            o_ref[:, h0 * hd:(h0 + 2) * hd] = pair.astype(o_ref.dtype).T


def _vt_attn_a_kernel(sinks_ref, x_ref, w_ref, q_ref, kvp_ref, kvc_ref, vt_ref, o_ref,
                      *, tk, blocks_per_seq):
    blk = WINDOW
    n_blocks = o_ref.shape[0] // blk
    first = (pl.program_id(0) * pl.num_programs(1) + pl.program_id(1)) * n_blocks
    x = x_ref[...]
    w = 2 * HEAD_DIM_B
    for c in range(max(vt_ref.shape[0], n_blocks)):
        if c < vt_ref.shape[0]:
            acc = _dot(x, w_ref[:, c * w:(c + 1) * w]).astype(vt_ref.dtype)
            for kk in range(vt_ref.shape[1]):
                vt_ref[c, kk] = acc[kk * tk:(kk + 1) * tk].T
        if c < n_blocks:
            rows = pl.ds(c * blk, blk)
            prev = kvp_ref if c == 0 else kvc_ref.at[pl.ds((c - 1) * blk, blk)]
            has_prev = (first % blocks_per_seq != 0) if c == 0 else True
            _attn_a_block(sinks_ref, q_ref.at[rows], prev, kvc_ref.at[rows], o_ref.at[rows], has_prev)


def _vt_attn_a(xb, w_v, layer, qkv_a, sinks, batch, seq, tk, tm=1024, heads=4, n_blocks=4):
    t = xb.shape[0]
    blk = WINDOW
    w = 2 * HEAD_DIM_B
    tiles_per_seq = seq // tm
    nj = N_HEADS_B // heads
    assert (t // tm) * nj * n_blocks * blk == t and (seq // blk) % n_blocks == 0
    kv_col = W_QA // (2 * W_KA)
    step = lambda i, j: i * nj + j
    return pl.pallas_call(
        functools.partial(_vt_attn_a_kernel, tk=tk, blocks_per_seq=seq // blk),
        grid=(t // tm, nj),
        in_specs=[
            pl.BlockSpec(memory_space=pltpu.SMEM),
            pl.BlockSpec((tm, D_MODEL), lambda i, j: (i, 0)),
            _w_in_cols(layer, W_A + 2 * W_QB, heads * w),
            pl.BlockSpec((n_blocks * blk, W_QA), lambda i, j: (step(i, j), 0)),
            pl.BlockSpec((blk, 2 * W_KA), lambda i, j: (jnp.maximum(step(i, j) * n_blocks - 1, 0), kv_col)),
            pl.BlockSpec((n_blocks * blk, 2 * W_KA), lambda i, j: (step(i, j), kv_col)),
        ],
        out_specs=[
            pl.BlockSpec((None, heads, tm // tk, w, tk),
                         lambda i, j: (i // tiles_per_seq, j, i % tiles_per_seq, 0, 0)),
            pl.BlockSpec((n_blocks * blk, W_QA), lambda i, j: (step(i, j), 0)),
        ],
        out_shape=[
            jax.ShapeDtypeStruct((batch, N_HEADS_B, seq // tk, w, tk), BF16),
            jax.ShapeDtypeStruct((t, W_QA), BF16),
        ],
        compiler_params=_params("parallel", "arbitrary"),
        name="vt_attn_a",
    )(sinks, xb, w_v, qkv_a, qkv_a, qkv_a)


def _attn_b_kernel(lq1_ref, lk1_ref, lq2_ref, lk2_ref, g_ref, q_ref, k_ref, vt_ref, o_ref,
                   qt_ref, m_ref, l_ref, acc_ref, *, tq, tk, n_q_tiles, lam_init):
    qi = pl.program_id(2)
    hd = HEAD_DIM_B
    cg = MXU_COLS
    n_groups = tq // cg
    for c in range(2):
        qt_ref[c] = q_ref[:, c * hd:(c + 1) * hd].T

    def chain(c, j, k, vt, first_key, start):
        js = slice(j * cg, (j + 1) * cg)
        s = _dot(k[:, c * hd:(c + 1) * hd], qt_ref[c, :, js])
        if first_key is not None:
            key = lax.broadcasted_iota(jnp.int32, s.shape, 0)
            qry = lax.broadcasted_iota(jnp.int32, s.shape, 1)
            s = jnp.where(key + first_key <= qry, s, NEG_INF)
        m_new = jnp.max(s, axis=0, keepdims=True)
        if not start:
            m_old = m_ref[c, :, js]
            m_new = jnp.maximum(m_old, m_new)
            a = jnp.exp2(m_old - m_new)
        p = jnp.exp2(s - m_new)
        l_new = jnp.sum(p, axis=0, keepdims=True)
        acc_new = _dot(vt, p.astype(BF16))
        if not start:
            l_new = a * l_ref[c, :, js] + l_new
            acc_new = a * acc_ref[c, :, js] + acc_new
        l_ref[c, :, js] = l_new
        acc_ref[c, :, js] = acc_new
        m_ref[c, :, js] = m_new

    blocks_per_tile = tq // tk

    def finish(j):
        js = slice(j * cg, (j + 1) * cg)
        o = acc_ref[0, :, js] * (1.0 / l_ref[0, :, js]) - lam * (acc_ref[1, :, js] * (1.0 / l_ref[1, :, js]))
        o = o * lax.rsqrt(jnp.mean(o * o, axis=0, keepdims=True) + SUBLN_EPS)
        o = o * g_ref[...] * (1.0 - lam_init)
        o_ref[js, :] = o.T.astype(o_ref.dtype)

    def tile(q_tile):
        started = set()

        def run(c, j, k, vt, first_key):
            chain(c, j, k, vt, first_key, (c, j) not in started)
            started.add((c, j))

        for kb in range(q_tile * blocks_per_tile):
            k = k_ref[kb * tk:(kb + 1) * tk, :]
            vt = vt_ref[kb]
            for j in range(n_groups):
                for c in range(2):
                    run(c, j, k, vt, None)
        for u in range(blocks_per_tile):
            kb = q_tile * blocks_per_tile + u
            k = k_ref[kb * tk:(kb + 1) * tk, :]
            vt = vt_ref[kb]
            for j in range(n_groups):
                first_key = u * tk - j * cg
                n_vis = min(tk, (j + 1) * cg - u * tk)
                if n_vis <= 0:
                    continue
                masked = first_key + n_vis - 1 > 0
                for c in range(2):
                    run(c, j, k[:n_vis], vt[:, :n_vis], first_key if masked else None)
        for j in range(n_groups):
            finish(j)

    lam = (jnp.exp(jnp.sum(lq1_ref[...] * lk1_ref[...], axis=-1, keepdims=True))
           - jnp.exp(jnp.sum(lq2_ref[...] * lk2_ref[...], axis=-1, keepdims=True))
           + lam_init)
    for q_tile in range(n_q_tiles):
        pl.when(qi == q_tile)(functools.partial(tile, q_tile))


def _attn_b(qk_b, vt, lq1, lk1, lq2, lk2, subln_g, lam_init, batch, seq, tq=1024, tk=512):
    nq = seq // tq
    w = 2 * HEAD_DIM_B
    vec = lambda n: pl.BlockSpec((1, n), lambda b, h, i: (0, 0))
    return pl.pallas_call(
        functools.partial(_attn_b_kernel, tq=tq, tk=tk, n_q_tiles=nq, lam_init=lam_init),
        grid=(batch, N_HEADS_B, nq),
        in_specs=[
            vec(HEAD_DIM_B), vec(HEAD_DIM_B), vec(HEAD_DIM_B), vec(HEAD_DIM_B),
            pl.BlockSpec((w, 1), lambda b, h, i: (0, 0)),
            pl.BlockSpec((tq, w), lambda b, h, i: (b * nq + i, h)),
            pl.BlockSpec((seq, w), lambda b, h, i: (b, N_HEADS_B + h)),
            pl.BlockSpec((None, None, seq // tk, w, tk), lambda b, h, i: (b, h, 0, 0, 0)),
        ],
        out_specs=pl.BlockSpec((tq, w), lambda b, h, i: (b * nq + i, h)),
        out_shape=jax.ShapeDtypeStruct((batch * seq, W_QB), BF16),
        scratch_shapes=[
            pltpu.VMEM((2, HEAD_DIM_B, tq), BF16),
            pltpu.VMEM((2, 1, tq), F32),
            pltpu.VMEM((2, 1, tq), F32),
            pltpu.VMEM((2, w, tq), F32),
        ],
        compiler_params=_params("parallel", "parallel", "arbitrary"),
        name="attn_b",
    )(lq1, lk1, lq2, lk2, subln_g, qk_b, qk_b, vt)


def _merge_kernel(oa_ref, ob_ref, wa_ref, wb_ref, ga_ref, gb_ref, o_ref):
    pa = _dot(oa_ref[...], wa_ref[...])
    pb = _dot(ob_ref[...], wb_ref[...])
    o_ref[...] = (ga_ref[...].astype(F32) * pa + gb_ref[...].astype(F32) * pb).astype(o_ref.dtype)


def _merge(o_a, o_b, w_pa, w_pb, layer, gates, tm=1024, tn=1024):
    t = o_a.shape[0]
    return pl.pallas_call(
        _merge_kernel,
        grid=(t // tm, D_MODEL // tn),
        in_specs=[
            pl.BlockSpec((tm, W_QA), lambda i, j: (i, 0)),
            pl.BlockSpec((tm, W_QB), lambda i, j: (i, 0)),
            pl.BlockSpec((None, W_QA, tn), lambda i, j: (layer, 0, j)),
            pl.BlockSpec((None, W_QB, tn), lambda i, j: (layer, 0, j)),
            pl.BlockSpec((tm, tn), lambda i, j: (i, j)),
            pl.BlockSpec((tm, tn), lambda i, j: (i, D_MODEL // tn + j)),
        ],
        out_specs=pl.BlockSpec((tm, tn), lambda i, j: (i, j)),
        out_shape=jax.ShapeDtypeStruct((t, D_MODEL), BF16),
        compiler_params=_params("parallel", "arbitrary"),
        name="merge",
    )(o_a, o_b, w_pa, w_pb, gates, gates)


def _out_ln_kernel(m_ref, w_ref, x_ref, g_ref, b_ref, y_ref, yb_ref, *, rows):
    g, b = g_ref[...], b_ref[...]
    for r in range(m_ref.shape[0] // rows):
        rs = slice(r * rows, (r + 1) * rows)
        z = ALPHA * x_ref[rs, :] + _dot(m_ref[rs, :], w_ref[...])
        y = _layer_norm(z, g, b)
        y_ref[rs, :] = y
        yb_ref[rs, :] = y.astype(yb_ref.dtype)


def _out_ln(merged, w_out, layer, x, g, b, tm=512, rows=256):
    t = x.shape[0]
    row = pl.BlockSpec((tm, D_MODEL), lambda i: (i, 0))
    vec = pl.BlockSpec((1, D_MODEL), lambda i: (0, 0))
    return pl.pallas_call(
        functools.partial(_out_ln_kernel, rows=rows),
        grid=(t // tm,),
        in_specs=[row, pl.BlockSpec((None, D_MODEL, D_MODEL), lambda i: (layer, 0, 0)), row, vec, vec],
        out_specs=[row, row],
        out_shape=[jax.ShapeDtypeStruct((t, D_MODEL), F32), jax.ShapeDtypeStruct((t, D_MODEL), BF16)],
        compiler_params=_params("parallel"),
        name="out_ln",
    )(merged, w_out, x, g, b)


def _ffn_up_kernel(xp_ref, x_ref, wg_ref, wv_ref, cwg_ref, cwv_ref, cbg_ref, cbv_ref, o_ref,
                   *, tm, tn, tiles_per_seq):
    i = pl.program_id(1)
    halo = BF16_ROWS
    xp = jnp.where(i % tiles_per_seq == 0, jnp.zeros((), BF16), xp_ref[...])
    xe = jnp.concatenate([xp, x_ref[...]], axis=0)

    def conv(w_ref, cw_ref, cb_ref, cs):
        u = _dot(xe, w_ref[:, cs])
        out = cb_ref[:, cs] + cw_ref[CONV_WIDTH - 1:CONV_WIDTH, cs] * u[halo:]
        for back in range(1, CONV_WIDTH):
            tap = CONV_WIDTH - 1 - back
            out = out + cw_ref[tap:tap + 1, cs] * pltpu.roll(u, back, axis=0)[halo:]
        return out

    for c in range(tn // MXU_COLS):
        cs = slice(c * MXU_COLS, (c + 1) * MXU_COLS)
        gate = conv(wg_ref, cwg_ref, cbg_ref, cs)
        val = conv(wv_ref, cwv_ref, cbv_ref, cs)
        o_ref[:, cs] = (jax.nn.silu(gate) * val).astype(o_ref.dtype)


def _ffn_up(xb, w_up, layer, conv_w, conv_b, seq, tm=1024, tn=D_FF // 2):
    t = xb.shape[0]
    nf = D_FF // tn
    halo = BF16_ROWS
    resident = pl.Buffered(1)
    return pl.pallas_call(
        functools.partial(_ffn_up_kernel, tm=tm, tn=tn, tiles_per_seq=seq // tm),
        grid=(nf, t // tm),
        in_specs=[
            pl.BlockSpec((halo, D_MODEL), lambda j, i: (jnp.maximum(i * (tm // halo) - 1, 0), 0)),
            pl.BlockSpec((tm, D_MODEL), lambda j, i: (i, 0)),
            pl.BlockSpec((None, D_MODEL, tn), lambda j, i: (layer, 0, j), pipeline_mode=resident),
            pl.BlockSpec((None, D_MODEL, tn), lambda j, i: (layer, 0, nf + j), pipeline_mode=resident),
            pl.BlockSpec((None, CONV_WIDTH, tn), lambda j, i: (layer, 0, j)),
            pl.BlockSpec((None, CONV_WIDTH, tn), lambda j, i: (layer, 0, nf + j)),
            pl.BlockSpec((None, 1, tn), lambda j, i: (layer, 0, j)),
            pl.BlockSpec((None, 1, tn), lambda j, i: (layer, 0, nf + j)),
        ],
        out_specs=pl.BlockSpec((tm, tn), lambda j, i: (i, j)),
        out_shape=jax.ShapeDtypeStruct((t, D_FF), BF16),
        compiler_params=_params("arbitrary", "parallel"),
        name="ffn_up",
    )(xb, xb, w_up, w_up, conv_w, conv_w, conv_b, conv_b)


def _down_ln_kernel(a_ref, w_ref, x_ref, g_ref, b_ref, *out_refs):
    z = ALPHA * x_ref[...] + _dot(a_ref[...], w_ref[...])
    y = _layer_norm(z, g_ref[...], b_ref[...])
    out_refs[0][...] = y
    if len(out_refs) > 1:
        out_refs[1][...] = y.astype(BF16)


def _down_ln(act, w_down, layer, x, g, b, with_bf16, tm=256):
    t = x.shape[0]
    row = pl.BlockSpec((tm, D_MODEL), lambda i: (i, 0))
    vec = pl.BlockSpec((1, D_MODEL), lambda i: (0, 0))
    out_specs = [row, row] if with_bf16 else [row]
    out_shape = [jax.ShapeDtypeStruct((t, D_MODEL), F32)]
    if with_bf16:
        out_shape.append(jax.ShapeDtypeStruct((t, D_MODEL), BF16))
    return pl.pallas_call(
        _down_ln_kernel,
        grid=(t // tm,),
        in_specs=[
            pl.BlockSpec((tm, D_FF), lambda i: (i, 0)),
            pl.BlockSpec((None, D_FF, D_MODEL), lambda i: (layer, 0, 0), pipeline_mode=pl.Buffered(1)),
            row, vec, vec,
        ],
        out_specs=out_specs,
        out_shape=out_shape,
        compiler_params=_params("parallel"),
        name="down_ln",
    )(act, w_down, x, g, b)


def kernel(x, w_in, sinks, lambda_q1, lambda_k1, lambda_q2, lambda_k2, subln_g, w_proj_a, w_proj_b,
           w_out, ln1_g, ln1_b, w_up, conv_w, conv_b, w_down, ln2_g, ln2_b):
    batch, seq, _ = x.shape
    t = batch * seq
    tab_a, tab_b = _rope_tables(seq)
    x = x.reshape(t, D_MODEL)
    xb = None
    row = lambda v: v.reshape(1, -1).astype(F32)
    w_in_b = w_in.astype(BF16).reshape(DEPTH * D_MODEL, -1)
    w_pa, w_pb, w_o = w_proj_a.astype(BF16), w_proj_b.astype(BF16), w_out.astype(BF16)
    w_u, w_d = w_up.astype(BF16), w_down.astype(BF16)
    conv_w = conv_w.astype(F32)
    conv_b = conv_b.astype(F32).reshape(DEPTH, 1, 2 * D_FF)
    for l in range(DEPTH):
        if xb is None:
            qkv_a, xb = _proj_a(x, w_in_b, l, tab_a, seq)
        else:
            qkv_a, = _proj_a(xb, w_in_b, l, tab_a, seq)
        qk_b = _proj_qk_b(xb, w_in_b, l, tab_b, seq)
        vt_b, o_a = _vt_attn_a(xb, w_in_b, l, qkv_a, sinks[l].astype(F32), batch, seq, ATTN_B_TK)
        gates = _proj_gates(xb, w_in_b, l)
        lam_init = 0.8 - 0.6 * math.exp(-0.3 * l)
        o_b = _attn_b(qk_b, vt_b, row(lambda_q1[l]), row(lambda_k1[l]), row(lambda_q2[l]), row(lambda_k2[l]),
                      subln_g[l].reshape(-1, 1).astype(F32), lam_init, batch, seq, tk=ATTN_B_TK)
        merged = _merge(o_a, o_b, w_pa, w_pb, l, gates)
        x, xb = _out_ln(merged, w_o, l, x, row(ln1_g[l]), row(ln1_b[l]))
        act = _ffn_up(xb, w_u, l, conv_w, conv_b, seq)
        last = l == DEPTH - 1
        outs = _down_ln(act, w_d, l, x, row(ln2_g[l]), row(ln2_b[l]), with_bf16=not last)
        x = outs[0]
        if not last:
            xb = outs[1]
    return x.reshape(batch, seq, D_MODEL)
```

```python
import functools
import math

import jax
import jax.numpy as jnp
from jax import lax
from jax.experimental import pallas as pl
from jax.experimental.pallas import tpu as pltpu

D_MODEL = 2048
DEPTH = 2
HEAD_DIM_A = 64
N_Q_A = 32
N_KV_A = 4
GROUP_A = N_Q_A // N_KV_A
WINDOW = 128
HEAD_DIM_B = 128
N_HEADS_B = D_MODEL // (2 * HEAD_DIM_B)
SUBLN_EPS = 1e-5
ROPE_THETA = 10000.0
D_FF = ((8 * D_MODEL // 3 + 255) // 256) * 256
CONV_WIDTH = 3
LN_EPS = 1e-5
ALPHA = (2 * DEPTH) ** 0.25
NEG_INF = -1e30
LOG2_E = math.log2(math.e)

W_QA = N_Q_A * HEAD_DIM_A
W_KA = N_KV_A * HEAD_DIM_A
W_A = W_QA + 2 * W_KA
W_QB = N_HEADS_B * 2 * HEAD_DIM_B

LANES = 128
MXU_COLS = 256
BF16_ROWS = 16
VMEM_LIMIT = 56 * 1024 * 1024
ATTN_B_TK = 512

F32 = jnp.float32
BF16 = jnp.bfloat16


def _params(*sem):
    return pltpu.CompilerParams(dimension_semantics=sem, vmem_limit_bytes=VMEM_LIMIT)


def _dot(a, b):
    return jnp.dot(a, b, preferred_element_type=F32)


def _dot_nt(a, b):
    return lax.dot_general(a, b, (((1,), (1,)), ((), ())), preferred_element_type=F32)


def _layer_norm(z, g, b):
    mu = jnp.mean(z, axis=-1, keepdims=True)
    zc = z - mu
    var = jnp.mean(zc * zc, axis=-1, keepdims=True)
    return zc * lax.rsqrt(var + LN_EPS) * g + b


def _call(kernel_fn, inputs, in_specs, out_specs, out_shape, *, grid, sem, name, cast=()):
    n_in, n_out, n_cast = len(inputs), len(out_specs), len(cast)
    n_steps = math.prod(grid)

    def step(*g):
        s = 0
        for g_d, n_d in zip(g, grid):
            s = s * n_d + g_d
        return s

    def slabs(offset):
        specs = []
        for w, first_row, n_rows in cast:
            slab = n_rows // n_steps
            assert n_rows % (n_steps * BF16_ROWS) == 0 and first_row % slab == 0
            first = first_row // slab if offset else 0
            specs.append(pl.BlockSpec((slab, w.shape[1]), lambda *g, first=first: (first + step(*g), 0)))
        return specs

    def body(*refs):
        ins, src = refs[:n_in], refs[n_in:n_in + n_cast]
        outs = refs[n_in + n_cast:n_in + n_cast + n_out]
        dst = refs[n_in + n_cast + n_out:]
        for s_ref, d_ref in zip(src, dst):
            d_ref[...] = s_ref[...].astype(d_ref.dtype)
        kernel_fn(*ins, *outs)

    return pl.pallas_call(
        body,
        grid=grid,
        in_specs=[*in_specs, *slabs(True)],
        out_specs=[*out_specs, *slabs(False)],
        out_shape=[*out_shape, *[jax.ShapeDtypeStruct((n_rows, w.shape[1]), BF16) for w, _, n_rows in cast]],
        compiler_params=_params(*sem),
        name=name,
    )(*inputs, *[w for w, _, _ in cast])


def _rope_tables(seq):
    def cs(dim):
        inv = 1.0 / (ROPE_THETA ** (jnp.arange(0, dim, 2, dtype=F32) / dim))
        ang = jnp.arange(seq, dtype=F32)[:, None] * inv[None, :]
        return jnp.cos(ang), jnp.sin(ang)

    cos_a, sin_a = cs(HEAD_DIM_A)
    cos_b, sin_b = cs(HEAD_DIM_B)
    zeros_a = jnp.zeros_like(sin_a)
    tab_a = jnp.stack([
        jnp.tile(cos_a, (1, 4)),
        jnp.tile(jnp.concatenate([zeros_a, sin_a], axis=1), (1, 2)),
        jnp.tile(jnp.concatenate([-sin_a, zeros_a], axis=1), (1, 2)),
    ])
    tab_b = jnp.stack([
        jnp.tile(cos_b, (1, 2)),
        jnp.concatenate([-sin_b, sin_b], axis=1),
    ])
    return tab_a, tab_b


def _proj_a_kernel(x_ref, w_ref, tab_ref, o_ref, *xb_refs):
    x = x_ref[...].astype(BF16)
    if xb_refs:
        xb_refs[0][...] = x
    acc = _dot(x, w_ref[...])
    cos, hi, lo = tab_ref[0], tab_ref[1], tab_ref[2]
    scale = LOG2_E * HEAD_DIM_A ** -0.5
    n_rope = (W_QA + W_KA) // LANES
    for g in range(W_A // LANES):
        xg = acc[:, g * LANES:(g + 1) * LANES]
        if g < n_rope:
            xg = xg * cos + pltpu.roll(xg, 32, axis=1) * hi + pltpu.roll(xg, 96, axis=1) * lo
            if g < W_QA // LANES:
                xg = xg * scale
        o_ref[:, g * LANES:(g + 1) * LANES] = xg.astype(o_ref.dtype)


def _proj_a(x, w_in_b, layer, tab_a, seq, tm=512):
    t = x.shape[0]
    out_specs = [pl.BlockSpec((tm, W_A), lambda i: (i, 0))]
    out_shape = [jax.ShapeDtypeStruct((t, W_A), BF16)]
    if x.dtype != BF16:
        out_specs.append(pl.BlockSpec((tm, D_MODEL), lambda i: (i, 0)))
        out_shape.append(jax.ShapeDtypeStruct((t, D_MODEL), BF16))
    return pl.pallas_call(
        _proj_a_kernel,
        grid=(t // tm,),
        in_specs=[
            pl.BlockSpec((tm, D_MODEL), lambda i: (i, 0)),
            pl.BlockSpec((D_MODEL, W_A), lambda i: (layer, 0)),
            pl.BlockSpec((3, tm, LANES), lambda i: (0, i % (seq // tm), 0)),
        ],
        out_specs=out_specs,
        out_shape=out_shape,
        compiler_params=_params("parallel"),
        name="proj_a",
    )(x, w_in_b, tab_a)


def _proj_rope_b_kernel(x_ref, w_ref, tab_ref, o_ref, *, tn, n_scaled):
    j = pl.program_id(1)
    scale = jnp.where(j < n_scaled, LOG2_E * HEAD_DIM_B ** -0.5, 1.0)
    cos = tab_ref[0] * scale
    sgn = tab_ref[1] * scale
    x = x_ref[...]
    for c in range(tn // MXU_COLS):
        acc = _dot(x, w_ref[:, c * MXU_COLS:(c + 1) * MXU_COLS])
        for g in range(MXU_COLS // LANES):
            xg = acc[:, g * LANES:(g + 1) * LANES]
            lo = c * MXU_COLS + g * LANES
            o_ref[:, lo:lo + LANES] = (xg * cos + pltpu.roll(xg, 64, axis=1) * sgn).astype(o_ref.dtype)


def _w_in_cols(layer, col0, width):
    return pl.BlockSpec((pl.Element(D_MODEL), pl.Element(width)),
                        lambda i, j: (layer * D_MODEL, pl.multiple_of(col0 + j * width, LANES)))


def _proj_qk_b(xb, w_in_b, layer, tab_b, seq, tm=1024, tn=2048, cast=()):
    t = xb.shape[0]
    return _call(
        functools.partial(_proj_rope_b_kernel, tn=tn, n_scaled=W_QB // tn),
        [xb, w_in_b, tab_b],
        [
            pl.BlockSpec((tm, D_MODEL), lambda i, j: (i, 0)),
            _w_in_cols(layer, W_A, tn),
            pl.BlockSpec((2, tm, LANES), lambda i, j: (0, i % (seq // tm), 0)),
        ],
        [pl.BlockSpec((tm, tn), lambda i, j: (i, j))],
        [jax.ShapeDtypeStruct((t, 2 * W_QB), BF16)],
        grid=(t // tm, 2 * W_QB // tn), sem=("parallel", "arbitrary"), name="proj_qk_b", cast=cast)


def _proj_gates_kernel(x_ref, w_ref, o_ref, *, tn):
    x = x_ref[...]
    for c in range(tn // MXU_COLS):
        cs = slice(c * MXU_COLS, (c + 1) * MXU_COLS)
        o_ref[:, cs] = jax.nn.sigmoid(_dot(x, w_ref[:, cs])).astype(o_ref.dtype)


def _proj_gates(xb, w_in_b, layer, tm=1024, tn=2048, cast=()):
    t = xb.shape[0]
    n = 2 * D_MODEL
    return _call(
        functools.partial(_proj_gates_kernel, tn=tn),
        [xb, w_in_b],
        [
            pl.BlockSpec((tm, D_MODEL), lambda i, j: (i, 0)),
            _w_in_cols(layer, W_A + 3 * W_QB, tn),
        ],
        [pl.BlockSpec((tm, tn), lambda i, j: (i, j))],
        [jax.ShapeDtypeStruct((t, n), BF16)],
        grid=(t // tm, n // tn), sem=("parallel", "arbitrary"), name="proj_gates", cast=cast)


def _attn_a_block(sinks_ref, q_ref, kvp_ref, kvc_ref, o_ref, has_prev):
    blk = WINDOW
    hd = HEAD_DIM_A
    key = lax.broadcasted_iota(jnp.int32, (2 * blk, blk), 0)
    qry = lax.broadcasted_iota(jnp.int32, (2 * blk, blk), 1)
    mask = (key <= blk + qry) & (key > qry)
    if has_prev is not True:
        mask = mask & ((key >= blk) | has_prev)
    vpt = kvp_ref[:, W_KA:].T
    vct = kvc_ref[:, W_KA:].T

    for g in range(N_KV_A):
        ks = slice(g * hd, (g + 1) * hd)
        kband = jnp.concatenate([kvp_ref[:, ks], kvc_ref[:, ks]], axis=0)
        vt = jnp.concatenate([vpt[ks], vct[ks]], axis=1)
        qts = []
        for pr in range(GROUP_A // 2):
            h0 = g * GROUP_A + 2 * pr
            t = q_ref[:, h0 * hd:(h0 + 2) * hd].T
            qts += [t[:hd], t[hd:]]
        qt = jnp.concatenate(qts, axis=1)
        s = _dot(kband, qt)
        ps, rs = [], []
        for hh in range(GROUP_A):
            sink = sinks_ref[g * GROUP_A + hh] * LOG2_E
            sh = jnp.where(mask, s[:, hh * blk:(hh + 1) * blk], NEG_INF)
            m = jnp.maximum(jnp.max(sh, axis=0, keepdims=True), sink)
            p = jnp.exp2(sh - m)
            denom = jnp.sum(p, axis=0, keepdims=True) + jnp.exp2(sink - m)
            ps.append(p.astype(BF16))
            rs.append(1.0 / denom)
        ot = _dot(vt, jnp.concatenate(ps, axis=1))
        for pr in range(GROUP_A // 2):
            h0 = g * GROUP_A + 2 * pr
            pair = jnp.concatenate([ot[:, (2 * pr + e) * blk:(2 * pr + e + 1) * blk] * rs[2 * pr + e]
                                    for e in range(2)], axis=0)
            o_ref[:, h0 * hd:(h0 + 2) * hd] = pair.astype(o_ref.dtype).T


def _vt_attn_a_kernel(sinks_ref, x_ref, w_ref, q_ref, kvp_ref, kvc_ref, vt_ref, o_ref,
                      *, tk, blocks_per_seq):
    blk = WINDOW
    n_blocks = o_ref.shape[0] // blk
    first = (pl.program_id(0) * pl.num_programs(1) + pl.program_id(1)) * n_blocks
    x = x_ref[...]
    w = 2 * HEAD_DIM_B
    for c in range(max(vt_ref.shape[0], n_blocks)):
        if c < vt_ref.shape[0]:
            acc = _dot(x, w_ref[:, c * w:(c + 1) * w]).astype(vt_ref.dtype)
            for kk in range(vt_ref.shape[1]):
                vt_ref[c, kk] = acc[kk * tk:(kk + 1) * tk].T
        if c < n_blocks:
            rows = pl.ds(c * blk, blk)
            prev = kvp_ref if c == 0 else kvc_ref.at[pl.ds((c - 1) * blk, blk)]
            has_prev = (first % blocks_per_seq != 0) if c == 0 else True
            _attn_a_block(sinks_ref, q_ref.at[rows], prev, kvc_ref.at[rows], o_ref.at[rows], has_prev)


def _vt_attn_a(xb, w_in_b, layer, qkv_a, sinks, batch, seq, tk, tm=1024, heads=4, n_blocks=4, cast=()):
    t = xb.shape[0]
    blk = WINDOW
    w = 2 * HEAD_DIM_B
    tiles_per_seq = seq // tm
    nj = N_HEADS_B // heads
    assert (t // tm) * nj * n_blocks * blk == t and (seq // blk) % n_blocks == 0
    kv_col = W_QA // (2 * W_KA)
    step = lambda i, j: i * nj + j
    return _call(
        functools.partial(_vt_attn_a_kernel, tk=tk, blocks_per_seq=seq // blk),
        [sinks, xb, w_in_b, qkv_a, qkv_a, qkv_a],
        [
            pl.BlockSpec(memory_space=pltpu.SMEM),
            pl.BlockSpec((tm, D_MODEL), lambda i, j: (i, 0)),
            _w_in_cols(layer, W_A + 2 * W_QB, heads * w),
            pl.BlockSpec((n_blocks * blk, W_QA), lambda i, j: (step(i, j), 0)),
            pl.BlockSpec((blk, 2 * W_KA), lambda i, j: (jnp.maximum(step(i, j) * n_blocks - 1, 0), kv_col)),
            pl.BlockSpec((n_blocks * blk, 2 * W_KA), lambda i, j: (step(i, j), kv_col)),
        ],
        [
            pl.BlockSpec((None, heads, tm // tk, w, tk),
                         lambda i, j: (i // tiles_per_seq, j, i % tiles_per_seq, 0, 0)),
            pl.BlockSpec((n_blocks * blk, W_QA), lambda i, j: (step(i, j), 0)),
        ],
        [
            jax.ShapeDtypeStruct((batch, N_HEADS_B, seq // tk, w, tk), BF16),
            jax.ShapeDtypeStruct((t, W_QA), BF16),
        ],
        grid=(t // tm, nj), sem=("parallel", "arbitrary"), name="vt_attn_a", cast=cast)


def _attn_b_kernel(lq1_ref, lk1_ref, lq2_ref, lk2_ref, g_ref, q_ref, k_ref, vt_ref, o_ref,
                   qt_ref, m_ref, l_ref, acc_ref, *, tq, tk, n_q_tiles, lam_init):
    qi = pl.program_id(2)
    hd = HEAD_DIM_B
    cg = MXU_COLS
    n_groups = tq // cg
    for c in range(2):
        qt_ref[c] = q_ref[:, c * hd:(c + 1) * hd].T

    def chain(c, j, k, vt, first_key, start):
        js = slice(j * cg, (j + 1) * cg)
        s = _dot(k[:, c * hd:(c + 1) * hd], qt_ref[c, :, js])
        if first_key is not None:
            key = lax.broadcasted_iota(jnp.int32, s.shape, 0)
            qry = lax.broadcasted_iota(jnp.int32, s.shape, 1)
            s = jnp.where(key + first_key <= qry, s, NEG_INF)
        m_new = jnp.max(s, axis=0, keepdims=True)
        if not start:
            m_old = m_ref[c, :, js]
            m_new = jnp.maximum(m_old, m_new)
            a = jnp.exp2(m_old - m_new)
        p = jnp.exp2(s - m_new)
        l_new = jnp.sum(p, axis=0, keepdims=True)
        acc_new = _dot(vt, p.astype(BF16))
        if not start:
            l_new = a * l_ref[c, :, js] + l_new
            acc_new = a * acc_ref[c, :, js] + acc_new
        l_ref[c, :, js] = l_new
        acc_ref[c, :, js] = acc_new
        m_ref[c, :, js] = m_new

    blocks_per_tile = tq // tk

    def finish(j):
        js = slice(j * cg, (j + 1) * cg)
        o = acc_ref[0, :, js] * (1.0 / l_ref[0, :, js]) - lam * (acc_ref[1, :, js] * (1.0 / l_ref[1, :, js]))
        o = o * lax.rsqrt(jnp.mean(o * o, axis=0, keepdims=True) + SUBLN_EPS)
        o = o * g_ref[...] * (1.0 - lam_init)
        o_ref[js, :] = o.T.astype(o_ref.dtype)

    def tile(q_tile):
        started = set()

        def run(c, j, k, vt, first_key):
            chain(c, j, k, vt, first_key, (c, j) not in started)
            started.add((c, j))

        for kb in range(q_tile * blocks_per_tile):
            k = k_ref[kb * tk:(kb + 1) * tk, :]
            vt = vt_ref[kb]
            for j in range(n_groups):
                for c in range(2):
                    run(c, j, k, vt, None)
        for u in range(blocks_per_tile):
            kb = q_tile * blocks_per_tile + u
            k = k_ref[kb * tk:(kb + 1) * tk, :]
            vt = vt_ref[kb]
            for j in range(n_groups):
                first_key = u * tk - j * cg
                n_vis = min(tk, (j + 1) * cg - u * tk)
                if n_vis <= 0:
                    continue
                masked = first_key + n_vis - 1 > 0
                for c in range(2):
                    run(c, j, k[:n_vis], vt[:, :n_vis], first_key if masked else None)
        for j in range(n_groups):
            finish(j)

    lam = (jnp.exp(jnp.sum(lq1_ref[...] * lk1_ref[...], axis=-1, keepdims=True))
           - jnp.exp(jnp.sum(lq2_ref[...] * lk2_ref[...], axis=-1, keepdims=True))
           + lam_init)
    for q_tile in range(n_q_tiles):
        pl.when(qi == q_tile)(functools.partial(tile, q_tile))


def _attn_b(qk_b, vt, lq1, lk1, lq2, lk2, subln_g, lam_init, batch, seq, tq=1024, tk=512):
    nq = seq // tq
    w = 2 * HEAD_DIM_B
    vec = lambda n: pl.BlockSpec((1, n), lambda b, h, i: (0, 0))
    return pl.pallas_call(
        functools.partial(_attn_b_kernel, tq=tq, tk=tk, n_q_tiles=nq, lam_init=lam_init),
        grid=(batch, N_HEADS_B, nq),
        in_specs=[
            vec(HEAD_DIM_B), vec(HEAD_DIM_B), vec(HEAD_DIM_B), vec(HEAD_DIM_B),
            pl.BlockSpec((w, 1), lambda b, h, i: (0, 0)),
            pl.BlockSpec((tq, w), lambda b, h, i: (b * nq + i, h)),
            pl.BlockSpec((seq, w), lambda b, h, i: (b, N_HEADS_B + h)),
            pl.BlockSpec((None, None, seq // tk, w, tk), lambda b, h, i: (b, h, 0, 0, 0)),
        ],
        out_specs=pl.BlockSpec((tq, w), lambda b, h, i: (b * nq + i, h)),
        out_shape=jax.ShapeDtypeStruct((batch * seq, W_QB), BF16),
        scratch_shapes=[
            pltpu.VMEM((2, HEAD_DIM_B, tq), BF16),
            pltpu.VMEM((2, 1, tq), F32),
            pltpu.VMEM((2, 1, tq), F32),
            pltpu.VMEM((2, w, tq), F32),
        ],
        compiler_params=_params("parallel", "parallel", "arbitrary"),
        name="attn_b",
    )(lq1, lk1, lq2, lk2, subln_g, qk_b, qk_b, vt)


def _merge_kernel(oa_ref, ob_ref, wa_ref, wb_ref, ga_ref, gb_ref, o_ref):
    pa = _dot(oa_ref[...], wa_ref[...])
    pb = _dot(ob_ref[...], wb_ref[...])
    o_ref[...] = (ga_ref[...].astype(F32) * pa + gb_ref[...].astype(F32) * pb).astype(o_ref.dtype)


def _merge(o_a, o_b, w_pa, w_pb, layer, gates, tm=1024, tn=1024, cast=()):
    t = o_a.shape[0]
    return _call(
        _merge_kernel,
        [o_a, o_b, w_pa, w_pb, gates, gates],
        [
            pl.BlockSpec((tm, W_QA), lambda i, j: (i, 0)),
            pl.BlockSpec((tm, W_QB), lambda i, j: (i, 0)),
            pl.BlockSpec((W_QA, tn), lambda i, j: (layer, j)),
            pl.BlockSpec((W_QB, tn), lambda i, j: (layer, j)),
            pl.BlockSpec((tm, tn), lambda i, j: (i, j)),
            pl.BlockSpec((tm, tn), lambda i, j: (i, D_MODEL // tn + j)),
        ],
        [pl.BlockSpec((tm, tn), lambda i, j: (i, j))],
        [jax.ShapeDtypeStruct((t, D_MODEL), BF16)],
        grid=(t // tm, D_MODEL // tn), sem=("parallel", "arbitrary"), name="merge", cast=cast)


def _out_ln_kernel(m_ref, w_ref, x_ref, g_ref, b_ref, y_ref, yb_ref, *, rows):
    g, b = g_ref[...], b_ref[...]
    for r in range(m_ref.shape[0] // rows):
        rs = slice(r * rows, (r + 1) * rows)
        z = ALPHA * x_ref[rs, :] + _dot(m_ref[rs, :], w_ref[...])
        y = _layer_norm(z, g, b)
        y_ref[rs, :] = y
        yb_ref[rs, :] = y.astype(yb_ref.dtype)


def _out_ln(merged, w_out, layer, x, g, b, tm=512, rows=256, cast=()):
    t = x.shape[0]
    row = lambda: pl.BlockSpec((tm, D_MODEL), lambda i: (i, 0))
    vec = lambda: pl.BlockSpec((1, D_MODEL), lambda i: (0, 0))
    return _call(
        functools.partial(_out_ln_kernel, rows=rows),
        [merged, w_out, x, g, b],
        [row(), pl.BlockSpec((D_MODEL, D_MODEL), lambda i: (layer, 0), pipeline_mode=pl.Buffered(1)),
         row(), vec(), vec()],
        [row(), row()],
        [jax.ShapeDtypeStruct((t, D_MODEL), F32), jax.ShapeDtypeStruct((t, D_MODEL), BF16)],
        grid=(t // tm,), sem=("parallel",), name="out_ln", cast=cast)


def _ffn_up_kernel(xp_ref, x_ref, wg_ref, wv_ref, cwg_ref, cwv_ref, cbg_ref, cbv_ref, o_ref,
                   *, tm, tn, tiles_per_seq):
    i = pl.program_id(1)
    halo = BF16_ROWS
    xp = jnp.where(i % tiles_per_seq == 0, jnp.zeros((), BF16), xp_ref[...])
    xe = jnp.concatenate([xp, x_ref[...]], axis=0)

    def conv(w_ref, cw_ref, cb_ref, cs):
        u = _dot(xe, w_ref[:, cs])
        out = cb_ref[:, cs] + cw_ref[CONV_WIDTH - 1:CONV_WIDTH, cs] * u[halo:]
        for back in range(1, CONV_WIDTH):
            tap = CONV_WIDTH - 1 - back
            out = out + cw_ref[tap:tap + 1, cs] * pltpu.roll(u, back, axis=0)[halo:]
        return out

    for c in range(tn // MXU_COLS):
        cs = slice(c * MXU_COLS, (c + 1) * MXU_COLS)
        gate = conv(wg_ref, cwg_ref, cbg_ref, cs)
        val = conv(wv_ref, cwv_ref, cbv_ref, cs)
        o_ref[:, cs] = (jax.nn.silu(gate) * val).astype(o_ref.dtype)


def _ffn_up(xb, w_up, layer, conv_w, conv_b, conv_layer, seq, tm=1024, tn=D_FF // 2):
    t = xb.shape[0]
    nf = D_FF // tn
    halo = BF16_ROWS
    resident = pl.Buffered(1)
    return pl.pallas_call(
        functools.partial(_ffn_up_kernel, tm=tm, tn=tn, tiles_per_seq=seq // tm),
        grid=(nf, t // tm),
        in_specs=[
            pl.BlockSpec((halo, D_MODEL), lambda j, i: (jnp.maximum(i * (tm // halo) - 1, 0), 0)),
            pl.BlockSpec((tm, D_MODEL), lambda j, i: (i, 0)),
            pl.BlockSpec((D_MODEL, tn), lambda j, i: (layer, j), pipeline_mode=resident),
            pl.BlockSpec((D_MODEL, tn), lambda j, i: (layer, nf + j), pipeline_mode=resident),
            pl.BlockSpec((None, CONV_WIDTH, tn), lambda j, i: (conv_layer, 0, j)),
            pl.BlockSpec((None, CONV_WIDTH, tn), lambda j, i: (conv_layer, 0, nf + j)),
            pl.BlockSpec((None, 1, tn), lambda j, i: (conv_layer, 0, j)),
            pl.BlockSpec((None, 1, tn), lambda j, i: (conv_layer, 0, nf + j)),
        ],
        out_specs=pl.BlockSpec((tm, tn), lambda j, i: (i, j)),
        out_shape=jax.ShapeDtypeStruct((t, D_FF), BF16),
        compiler_params=_params("arbitrary", "parallel"),
        name="ffn_up",
    )(xb, xb, w_up, w_up, conv_w, conv_w, conv_b, conv_b)


def _down_ln_kernel(a_ref, w_ref, x_ref, g_ref, b_ref, *out_refs):
    z = ALPHA * x_ref[...] + _dot(a_ref[...], w_ref[...])
    y = _layer_norm(z, g_ref[...], b_ref[...])
    out_refs[0][...] = y
    if len(out_refs) > 1:
        out_refs[1][...] = y.astype(BF16)


def _down_ln(act, w_down, layer, x, g, b, with_bf16, tm=256, cast=()):
    t = x.shape[0]
    row = lambda: pl.BlockSpec((tm, D_MODEL), lambda i: (i, 0))
    vec = lambda: pl.BlockSpec((1, D_MODEL), lambda i: (0, 0))
    out_shape = [jax.ShapeDtypeStruct((t, D_MODEL), F32)]
    if with_bf16:
        out_shape.append(jax.ShapeDtypeStruct((t, D_MODEL), BF16))
    return _call(
        _down_ln_kernel,
        [act, w_down, x, g, b],
        [
            pl.BlockSpec((tm, D_FF), lambda i: (i, 0)),
            pl.BlockSpec((D_FF, D_MODEL), lambda i: (layer, 0), pipeline_mode=pl.Buffered(1)),
            row(), vec(), vec(),
        ],
        [row() for _ in out_shape],
        out_shape,
        grid=(t // tm,), sem=("parallel",), name="down_ln", cast=cast)


def kernel(x, w_in, sinks, lambda_q1, lambda_k1, lambda_q2, lambda_k2, subln_g, w_proj_a, w_proj_b,
           w_out, ln1_g, ln1_b, w_up, conv_w, conv_b, w_down, ln2_g, ln2_b):
    batch, seq, _ = x.shape
    t = batch * seq
    tab_a, tab_b = _rope_tables(seq)
    x = x.reshape(t, D_MODEL)
    xb = None
    row = lambda v: v.reshape(1, -1).astype(F32)
    rows = lambda w: w.reshape(-1, w.shape[-1])
    whole = lambda w: (rows(w), 0, rows(w).shape[0])
    layer0 = lambda w: (rows(w), 0, w.shape[1])
    later = lambda w: (rows(w), w.shape[1], (DEPTH - 1) * w.shape[1])
    conv_w = conv_w.astype(F32)
    conv_b = conv_b.astype(F32).reshape(DEPTH, 1, 2 * D_FF)
    w_in_b, w_in_row = w_in[0].astype(BF16), 0
    w_up_b, w_up_row = None, 0
    for l in range(DEPTH):
        first = l == 0
        if xb is None:
            qkv_a, xb = _proj_a(x, w_in_b, w_in_row, tab_a, seq)
        else:
            qkv_a, = _proj_a(xb, w_in_b, w_in_row, tab_a, seq)
        qk_b, *done = _proj_qk_b(xb, w_in_b, w_in_row, tab_b, seq, cast=[layer0(w_up)] if first else [])
        if first:
            w_up_b, = done
        vt_b, o_a, *done = _vt_attn_a(xb, w_in_b, w_in_row, qkv_a, sinks[l].astype(F32), batch, seq, ATTN_B_TK,
                                      cast=[whole(w_proj_a), whole(w_proj_b), whole(w_out)] if first else [])
        if first:
            w_pa, w_pb, w_o = done
        gates, *done = _proj_gates(xb, w_in_b, w_in_row, cast=[later(w_up)] if first and DEPTH > 1 else [])
        if done:
            w_up_rest, = done
        lam_init = 0.8 - 0.6 * math.exp(-0.3 * l)
        o_b = _attn_b(qk_b, vt_b, row(lambda_q1[l]), row(lambda_k1[l]), row(lambda_q2[l]), row(lambda_k2[l]),
                      subln_g[l].reshape(-1, 1).astype(F32), lam_init, batch, seq, tk=ATTN_B_TK)
        merged, = _merge(o_a, o_b, w_pa, w_pb, l, gates)
        x, xb, *done = _out_ln(merged, w_o, l, x, row(ln1_g[l]), row(ln1_b[l]),
                               cast=[whole(w_down)] if first else [])
        if first:
            w_d, = done
        act = _ffn_up(xb, w_up_b, w_up_row, conv_w, conv_b, l, seq)
        last = l == DEPTH - 1
        x, *rest = _down_ln(act, w_d, l, x, row(ln2_g[l]), row(ln2_b[l]), with_bf16=not last,
                            cast=[later(w_in)] if first and DEPTH > 1 else [])
        if not last:
            xb = rest[0]
        if first and DEPTH > 1:
            w_in_b, w_up_b = rest[-1], w_up_rest
        if not last:
            w_in_row, w_up_row = l, l
    return x.reshape(batch, seq, D_MODEL)
```

```python
import functools
import math

import jax
import jax.numpy as jnp
from jax import lax
from jax.experimental import pallas as pl
from jax.experimental.pallas import tpu as pltpu

D_MODEL = 2048
DEPTH = 2
HEAD_DIM_A = 64
N_Q_A = 32
N_KV_A = 4
GROUP_A = N_Q_A // N_KV_A
WINDOW = 128
HEAD_DIM_B = 128
N_HEADS_B = D_MODEL // (2 * HEAD_DIM_B)
SUBLN_EPS = 1e-5
ROPE_THETA = 10000.0
D_FF = ((8 * D_MODEL // 3 + 255) // 256) * 256
CONV_WIDTH = 3
LN_EPS = 1e-5
ALPHA = (2 * DEPTH) ** 0.25
NEG_INF = -1e30
LOG2_E = math.log2(math.e)

W_QA = N_Q_A * HEAD_DIM_A
W_KA = N_KV_A * HEAD_DIM_A
W_A = W_QA + 2 * W_KA
W_QB = N_HEADS_B * 2 * HEAD_DIM_B

LANES = 128
MXU_COLS = 256
BF16_ROWS = 16
VMEM_LIMIT = 56 * 1024 * 1024
ATTN_B_TK = 512

F32 = jnp.float32
BF16 = jnp.bfloat16


def _params(*sem):
    return pltpu.CompilerParams(dimension_semantics=sem, vmem_limit_bytes=VMEM_LIMIT)


def _dot(a, b):
    return jnp.dot(a, b, preferred_element_type=F32)


def _dot_nt(a, b):
    return lax.dot_general(a, b, (((1,), (1,)), ((), ())), preferred_element_type=F32)


def _layer_norm(z, g, b):
    mu = jnp.mean(z, axis=-1, keepdims=True)
    zc = z - mu
    var = jnp.mean(zc * zc, axis=-1, keepdims=True)
    return zc * lax.rsqrt(var + LN_EPS) * g + b


def _call(kernel_fn, inputs, in_specs, out_specs, out_shape, *, grid, sem, name, cast=()):
    n_in, n_out, n_cast = len(inputs), len(out_specs), len(cast)
    n_steps = math.prod(grid)

    def step(*g):
        s = 0
        for g_d, n_d in zip(g, grid):
            s = s * n_d + g_d
        return s

    def slabs(offset):
        specs = []
        for w, first_row, n_rows in cast:
            slab = n_rows // n_steps
            assert n_rows % (n_steps * BF16_ROWS) == 0 and first_row % slab == 0
            first = first_row // slab if offset else 0
            specs.append(pl.BlockSpec((slab, w.shape[1]), lambda *g, first=first: (first + step(*g), 0)))
        return specs

    def body(*refs):
        ins, src = refs[:n_in], refs[n_in:n_in + n_cast]
        outs = refs[n_in + n_cast:n_in + n_cast + n_out]
        dst = refs[n_in + n_cast + n_out:]
        for s_ref, d_ref in zip(src, dst):
            d_ref[...] = s_ref[...].astype(d_ref.dtype)
        kernel_fn(*ins, *outs)

    return pl.pallas_call(
        body,
        grid=grid,
        in_specs=[*in_specs, *slabs(True)],
        out_specs=[*out_specs, *slabs(False)],
        out_shape=[*out_shape, *[jax.ShapeDtypeStruct((n_rows, w.shape[1]), BF16) for w, _, n_rows in cast]],
        compiler_params=_params(*sem),
        name=name,
    )(*inputs, *[w for w, _, _ in cast])


def _rope_tables(seq):
    def cs(dim):
        inv = 1.0 / (ROPE_THETA ** (jnp.arange(0, dim, 2, dtype=F32) / dim))
        ang = jnp.arange(seq, dtype=F32)[:, None] * inv[None, :]
        return jnp.cos(ang), jnp.sin(ang)

    cos_a, sin_a = cs(HEAD_DIM_A)
    cos_b, sin_b = cs(HEAD_DIM_B)
    zeros_a = jnp.zeros_like(sin_a)
    heads = LANES // HEAD_DIM_A
    tab_a = jnp.stack([
        jnp.tile(cos_a, (1, 2 * heads)),
        jnp.tile(jnp.concatenate([zeros_a, sin_a], axis=1), (1, heads)),
        jnp.tile(jnp.concatenate([-sin_a, zeros_a], axis=1), (1, heads)),
    ])
    assert HEAD_DIM_B == LANES
    tab_b = jnp.stack([
        jnp.concatenate([cos_b, cos_b], axis=1),
        jnp.concatenate([-sin_b, sin_b], axis=1),
    ])
    return tab_a, tab_b


def _proj_a_kernel(x_ref, w_ref, tab_ref, o_ref, *xb_refs):
    x = x_ref[...].astype(BF16)
    if xb_refs:
        xb_refs[0][...] = x
    acc = _dot(x, w_ref[...])
    cos, hi, lo = tab_ref[0], tab_ref[1], tab_ref[2]
    scale = LOG2_E * HEAD_DIM_A ** -0.5
    n_rope = (W_QA + W_KA) // LANES
    for g in range(W_A // LANES):
        xg = acc[:, g * LANES:(g + 1) * LANES]
        if g < n_rope:
            half = HEAD_DIM_A // 2
            xg = xg * cos + pltpu.roll(xg, half, axis=1) * hi + pltpu.roll(xg, LANES - half, axis=1) * lo
            if g < W_QA // LANES:
                xg = xg * scale
        o_ref[:, g * LANES:(g + 1) * LANES] = xg.astype(o_ref.dtype)


def _proj_a(x, w_in_b, layer, tab_a, seq, tm=512, cast=()):
    t = x.shape[0]
    out_specs = [pl.BlockSpec((tm, W_A), lambda i: (i, 0))]
    out_shape = [jax.ShapeDtypeStruct((t, W_A), BF16)]
    if x.dtype != BF16:
        out_specs.append(pl.BlockSpec((tm, D_MODEL), lambda i: (i, 0)))
        out_shape.append(jax.ShapeDtypeStruct((t, D_MODEL), BF16))
    return _call(
        _proj_a_kernel,
        [x, w_in_b, tab_a],
        [
            pl.BlockSpec((tm, D_MODEL), lambda i: (i, 0)),
            pl.BlockSpec((D_MODEL, W_A), lambda i: (layer, 0)),
            pl.BlockSpec((3, tm, LANES), lambda i: (0, i % (seq // tm), 0)),
        ],
        out_specs,
        out_shape,
        grid=(t // tm,), sem=("parallel",), name="proj_a", cast=cast)


def _proj_rope_b_kernel(x_ref, w_ref, tab_ref, qt_ref, k_ref):
    scale = LOG2_E * HEAD_DIM_B ** -0.5
    x = x_ref[...]
    w = 2 * HEAD_DIM_B
    assert w == MXU_COLS and HEAD_DIM_B == LANES
    for c in range(2 * N_HEADS_B):
        acc = _dot(x, w_ref[:, c * w:(c + 1) * w])
        for g in range(2):
            xg = acc[:, g * LANES:(g + 1) * LANES]
            xg = xg * tab_ref[0] + pltpu.roll(xg, HEAD_DIM_B // 2, axis=1) * tab_ref[1]
            if c < N_HEADS_B:
                qt_ref[c, g] = (xg * scale).astype(qt_ref.dtype).T
            else:
                lo = (c - N_HEADS_B) * w + g * LANES
                k_ref[:, lo:lo + LANES] = xg.astype(k_ref.dtype)


def _w_in_cols(layer, col0, width):
    return pl.BlockSpec((pl.Element(D_MODEL), pl.Element(width)),
                        lambda i, j: (layer * D_MODEL, pl.multiple_of(col0 + j * width, LANES)))


def _proj_qk_b(xb, w_in_b, layer, tab_b, batch, seq, tm=1024, cast=()):
    t = xb.shape[0]
    tiles_per_seq = seq // tm
    w_spec = pl.BlockSpec((pl.Element(D_MODEL), pl.Element(2 * W_QB)), lambda i: (layer * D_MODEL, W_A),
                          pipeline_mode=pl.Buffered(1))
    return _call(
        _proj_rope_b_kernel,
        [xb, w_in_b, tab_b],
        [
            pl.BlockSpec((tm, D_MODEL), lambda i: (i, 0)),
            w_spec,
            pl.BlockSpec((2, tm, LANES), lambda i: (0, i % tiles_per_seq, 0)),
        ],
        [
            pl.BlockSpec((None, N_HEADS_B, 2, HEAD_DIM_B, tm),
                         lambda i: (i // tiles_per_seq, 0, 0, 0, i % tiles_per_seq)),
            pl.BlockSpec((tm, W_QB), lambda i: (i, 0)),
        ],
        [
            jax.ShapeDtypeStruct((batch, N_HEADS_B, 2, HEAD_DIM_B, seq), BF16),
            jax.ShapeDtypeStruct((t, W_QB), BF16),
        ],
        grid=(t // tm,), sem=("parallel",), name="proj_qk_b", cast=cast)


def _proj_gates_kernel(x_ref, w_ref, o_ref, *, tn):
    x = x_ref[...]
    for c in range(tn // MXU_COLS):
        cs = slice(c * MXU_COLS, (c + 1) * MXU_COLS)
        o_ref[:, cs] = jax.nn.sigmoid(_dot(x, w_ref[:, cs])).astype(o_ref.dtype)


def _proj_gates(xb, w_in_b, layer, tm=1024, tn=2048, cast=()):
    t = xb.shape[0]
    n = 2 * D_MODEL
    return _call(
        functools.partial(_proj_gates_kernel, tn=tn),
        [xb, w_in_b],
        [
            pl.BlockSpec((tm, D_MODEL), lambda i, j: (i, 0)),
            _w_in_cols(layer, W_A + 3 * W_QB, tn),
        ],
        [pl.BlockSpec((tm, tn), lambda i, j: (i, j))],
        [jax.ShapeDtypeStruct((t, n), BF16)],
        grid=(t // tm, n // tn), sem=("parallel", "arbitrary"), name="proj_gates", cast=cast)


def _attn_a_block(sinks_ref, q_ref, kvp_ref, kvc_ref, o_ref, has_prev):
    blk = WINDOW
    hd = HEAD_DIM_A
    key = lax.broadcasted_iota(jnp.int32, (2 * blk, blk), 0)
    qry = lax.broadcasted_iota(jnp.int32, (2 * blk, blk), 1)
    mask = (key <= blk + qry) & (key > qry)
    if has_prev is not True:
        mask = mask & ((key >= blk) | has_prev)
    vpt = kvp_ref[:, W_KA:].T
    vct = kvc_ref[:, W_KA:].T

    for g in range(N_KV_A):
        ks = slice(g * hd, (g + 1) * hd)
        kband = jnp.concatenate([kvp_ref[:, ks], kvc_ref[:, ks]], axis=0)
        vt = jnp.concatenate([vpt[ks], vct[ks]], axis=1)
        qts = []
        for pr in range(GROUP_A // 2):
            h0 = g * GROUP_A + 2 * pr
            t = q_ref[:, h0 * hd:(h0 + 2) * hd].T
            qts += [t[:hd], t[hd:]]
        qt = jnp.concatenate(qts, axis=1)
        s = _dot(kband, qt)
        ps, rs = [], []
        for hh in range(GROUP_A):
            sink = sinks_ref[g * GROUP_A + hh] * LOG2_E
            sh = jnp.where(mask, s[:, hh * blk:(hh + 1) * blk], NEG_INF)
            m = jnp.maximum(jnp.max(sh, axis=0, keepdims=True), sink)
            p = jnp.exp2(sh - m)
            denom = jnp.sum(p, axis=0, keepdims=True) + jnp.exp2(sink - m)
            ps.append(p.astype(BF16))
            rs.append(1.0 / denom)
        ot = _dot(vt, jnp.concatenate(ps, axis=1))
        for pr in range(GROUP_A // 2):
            h0 = g * GROUP_A + 2 * pr
            pair = jnp.concatenate([ot[:, (2 * pr + e) * blk:(2 * pr + e + 1) * blk] * rs[2 * pr + e]
                                    for e in range(2)], axis=0)
            o_ref[:, h0 * hd:(h0 + 2) * hd] = pair.astype(o_ref.dtype).T


def _vt_attn_a_kernel(sinks_ref, x_ref, w_ref, q_ref, kvp_ref, kvc_ref, vt_ref, o_ref,
                      *, tk, blocks_per_seq):
    blk = WINDOW
    n_blocks = o_ref.shape[0] // blk
    first = (pl.program_id(0) * pl.num_programs(1) + pl.program_id(1)) * n_blocks
    x = x_ref[...]
    w = 2 * HEAD_DIM_B
    for c in range(max(vt_ref.shape[0], n_blocks)):
        if c < vt_ref.shape[0]:
            acc = _dot(x, w_ref[:, c * w:(c + 1) * w]).astype(vt_ref.dtype)
            for kk in range(vt_ref.shape[1]):
                vt_ref[c, kk] = acc[kk * tk:(kk + 1) * tk].T
        if c < n_blocks:
            rows = pl.ds(c * blk, blk)
            prev = kvp_ref if c == 0 else kvc_ref.at[pl.ds((c - 1) * blk, blk)]
            has_prev = (first % blocks_per_seq != 0) if c == 0 else True
            _attn_a_block(sinks_ref, q_ref.at[rows], prev, kvc_ref.at[rows], o_ref.at[rows], has_prev)


def _vt_attn_a(xb, w_in_b, layer, qkv_a, sinks, batch, seq, tk, tm=1024, heads=4, n_blocks=4, cast=()):
    t = xb.shape[0]
    blk = WINDOW
    w = 2 * HEAD_DIM_B
    tiles_per_seq = seq // tm
    nj = N_HEADS_B // heads
    assert (t // tm) * nj * n_blocks * blk == t and (seq // blk) % n_blocks == 0
    kv_col = W_QA // (2 * W_KA)
    step = lambda i, j: i * nj + j
    return _call(
        functools.partial(_vt_attn_a_kernel, tk=tk, blocks_per_seq=seq // blk),
        [sinks, xb, w_in_b, qkv_a, qkv_a, qkv_a],
        [
            pl.BlockSpec(memory_space=pltpu.SMEM),
            pl.BlockSpec((tm, D_MODEL), lambda i, j: (i, 0)),
            _w_in_cols(layer, W_A + 2 * W_QB, heads * w),
            pl.BlockSpec((n_blocks * blk, W_QA), lambda i, j: (step(i, j), 0)),
            pl.BlockSpec((blk, 2 * W_KA), lambda i, j: (jnp.maximum(step(i, j) * n_blocks - 1, 0), kv_col)),
            pl.BlockSpec((n_blocks * blk, 2 * W_KA), lambda i, j: (step(i, j), kv_col)),
        ],
        [
            pl.BlockSpec((None, heads, tm // tk, w, tk),
                         lambda i, j: (i // tiles_per_seq, j, i % tiles_per_seq, 0, 0)),
            pl.BlockSpec((n_blocks * blk, W_QA), lambda i, j: (step(i, j), 0)),
        ],
        [
            jax.ShapeDtypeStruct((batch, N_HEADS_B, seq // tk, w, tk), BF16),
            jax.ShapeDtypeStruct((t, W_QA), BF16),
        ],
        grid=(t // tm, nj), sem=("parallel", "arbitrary"), name="vt_attn_a", cast=cast)


def _attn_b_kernel(lq1_ref, lk1_ref, lq2_ref, lk2_ref, g_ref, qt_ref, k_ref, vt_ref, o_ref,
                   m_ref, l_ref, acc_ref, *, tq, tk, n_q_tiles, lam_init):
    qi = pl.program_id(2)
    hd = HEAD_DIM_B
    cg = MXU_COLS
    n_groups = tq // cg

    def chain(c, j, k, vt, first_key, start):
        js = slice(j * cg, (j + 1) * cg)
        s = _dot(k[:, c * hd:(c + 1) * hd], qt_ref[c, :, js])
        if first_key is not None:
            key = lax.broadcasted_iota(jnp.int32, s.shape, 0)
            qry = lax.broadcasted_iota(jnp.int32, s.shape, 1)
            s = jnp.where(key + first_key <= qry, s, NEG_INF)
        m_new = jnp.max(s, axis=0, keepdims=True)
        if not start:
            m_old = m_ref[c, :, js]
            m_new = jnp.maximum(m_old, m_new)
            a = jnp.exp2(m_old - m_new)
        p = jnp.exp2(s - m_new)
        l_new = jnp.sum(p, axis=0, keepdims=True)
        acc_new = _dot(vt, p.astype(BF16))
        if not start:
            l_new = a * l_ref[c, :, js] + l_new
            acc_new = a * acc_ref[c, :, js] + acc_new
        l_ref[c, :, js] = l_new
        acc_ref[c, :, js] = acc_new
        m_ref[c, :, js] = m_new

    blocks_per_tile = tq // tk

    def finish(j):
        js = slice(j * cg, (j + 1) * cg)
        o = acc_ref[0, :, js] * (1.0 / l_ref[0, :, js]) - lam * (acc_ref[1, :, js] * (1.0 / l_ref[1, :, js]))
        o = o * lax.rsqrt(jnp.mean(o * o, axis=0, keepdims=True) + SUBLN_EPS)
        o = o * g_ref[...] * (1.0 - lam_init)
        o_ref[js, :] = o.T.astype(o_ref.dtype)

    def tile(q_tile):
        started = set()

        def run(c, j, k, vt, first_key):
            chain(c, j, k, vt, first_key, (c, j) not in started)
            started.add((c, j))

        for kb in range(q_tile * blocks_per_tile):
            k = k_ref[kb * tk:(kb + 1) * tk, :]
            vt = vt_ref[kb]
            for j in range(n_groups):
                for c in range(2):
                    run(c, j, k, vt, None)
        for u in range(blocks_per_tile):
            kb = q_tile * blocks_per_tile + u
            k = k_ref[kb * tk:(kb + 1) * tk, :]
            vt = vt_ref[kb]
            for j in range(n_groups):
                first_key = u * tk - j * cg
                n_vis = min(tk, (j + 1) * cg - u * tk)
                if n_vis <= 0:
                    continue
                masked = first_key + n_vis - 1 > 0
                for c in range(2):
                    run(c, j, k[:n_vis], vt[:, :n_vis], first_key if masked else None)
        for j in range(n_groups):
            finish(j)

    lam = (jnp.exp(jnp.sum(lq1_ref[...] * lk1_ref[...], axis=-1, keepdims=True))
           - jnp.exp(jnp.sum(lq2_ref[...] * lk2_ref[...], axis=-1, keepdims=True))
           + lam_init)
    for q_tile in range(n_q_tiles):
        pl.when(qi == q_tile)(functools.partial(tile, q_tile))


def _attn_b(qt_b, k_b, vt, lq1, lk1, lq2, lk2, subln_g, lam_init, batch, seq, tq=1024, tk=512):
    nq = seq // tq
    w = 2 * HEAD_DIM_B
    vec = lambda n: pl.BlockSpec((1, n), lambda b, h, i: (0, 0))
    return pl.pallas_call(
        functools.partial(_attn_b_kernel, tq=tq, tk=tk, n_q_tiles=nq, lam_init=lam_init),
        grid=(batch, N_HEADS_B, nq),
        in_specs=[
            vec(HEAD_DIM_B), vec(HEAD_DIM_B), vec(HEAD_DIM_B), vec(HEAD_DIM_B),
            pl.BlockSpec((w, 1), lambda b, h, i: (0, 0)),
            pl.BlockSpec((None, None, 2, HEAD_DIM_B, tq), lambda b, h, i: (b, h, 0, 0, i)),
            pl.BlockSpec((seq, w), lambda b, h, i: (b, h)),
            pl.BlockSpec((None, None, seq // tk, w, tk), lambda b, h, i: (b, h, 0, 0, 0)),
        ],
        out_specs=pl.BlockSpec((tq, w), lambda b, h, i: (b * nq + i, h)),
        out_shape=jax.ShapeDtypeStruct((batch * seq, W_QB), BF16),
        scratch_shapes=[
            pltpu.VMEM((2, 1, tq), F32),
            pltpu.VMEM((2, 1, tq), F32),
            pltpu.VMEM((2, w, tq), F32),
        ],
        compiler_params=_params("parallel", "parallel", "arbitrary"),
        name="attn_b",
    )(lq1, lk1, lq2, lk2, subln_g, qt_b, k_b, vt)


def _merge_kernel(oa_ref, ob_ref, wa_ref, wb_ref, ga_ref, gb_ref, o_ref):
    pa = _dot(oa_ref[...], wa_ref[...])
    pb = _dot(ob_ref[...], wb_ref[...])
    o_ref[...] = (ga_ref[...].astype(F32) * pa + gb_ref[...].astype(F32) * pb).astype(o_ref.dtype)


def _merge(o_a, o_b, w_pa, w_pb, layer, gates, tm=1024, tn=1024, cast=()):
    t = o_a.shape[0]
    return _call(
        _merge_kernel,
        [o_a, o_b, w_pa, w_pb, gates, gates],
        [
            pl.BlockSpec((tm, W_QA), lambda i, j: (i, 0)),
            pl.BlockSpec((tm, W_QB), lambda i, j: (i, 0)),
            pl.BlockSpec((W_QA, tn), lambda i, j: (layer, j)),
            pl.BlockSpec((W_QB, tn), lambda i, j: (layer, j)),
            pl.BlockSpec((tm, tn), lambda i, j: (i, j)),
            pl.BlockSpec((tm, tn), lambda i, j: (i, D_MODEL // tn + j)),
        ],
        [pl.BlockSpec((tm, tn), lambda i, j: (i, j))],
        [jax.ShapeDtypeStruct((t, D_MODEL), BF16)],
        grid=(t // tm, D_MODEL // tn), sem=("parallel", "arbitrary"), name="merge", cast=cast)


def _out_ln_kernel(m_ref, w_ref, x_ref, g_ref, b_ref, y_ref, yb_ref, *, rows):
    g, b = g_ref[...], b_ref[...]
    for r in range(m_ref.shape[0] // rows):
        rs = slice(r * rows, (r + 1) * rows)
        z = ALPHA * x_ref[rs, :] + _dot(m_ref[rs, :], w_ref[...])
        y = _layer_norm(z, g, b)
        y_ref[rs, :] = y
        yb_ref[rs, :] = y.astype(yb_ref.dtype)


def _out_ln(merged, w_out, layer, x, g, b, tm=512, rows=256, cast=()):
    t = x.shape[0]
    row = lambda: pl.BlockSpec((tm, D_MODEL), lambda i: (i, 0))
    vec = lambda: pl.BlockSpec((1, D_MODEL), lambda i: (0, 0))
    return _call(
        functools.partial(_out_ln_kernel, rows=rows),
        [merged, w_out, x, g, b],
        [row(), pl.BlockSpec((D_MODEL, D_MODEL), lambda i: (layer, 0), pipeline_mode=pl.Buffered(1)),
         row(), vec(), vec()],
        [row(), row()],
        [jax.ShapeDtypeStruct((t, D_MODEL), F32), jax.ShapeDtypeStruct((t, D_MODEL), BF16)],
        grid=(t // tm,), sem=("parallel",), name="out_ln", cast=cast)


def _ffn_up_kernel(xp_ref, x_ref, wg_ref, wv_ref, cwg_ref, cwv_ref, cbg_ref, cbv_ref, o_ref,
                   *, tm, tn, tiles_per_seq):
    i = pl.program_id(1)
    halo = BF16_ROWS
    xp = jnp.where(i % tiles_per_seq == 0, jnp.zeros((), BF16), xp_ref[...])
    xe = jnp.concatenate([xp, x_ref[...]], axis=0)

    def conv(w_ref, cw_ref, cb_ref, cs):
        u = _dot(xe, w_ref[:, cs])
        out = cb_ref[:, cs] + cw_ref[CONV_WIDTH - 1:CONV_WIDTH, cs] * u[halo:]
        for back in range(1, CONV_WIDTH):
            tap = CONV_WIDTH - 1 - back
            out = out + cw_ref[tap:tap + 1, cs] * pltpu.roll(u, back, axis=0)[halo:]
        return out

    for c in range(tn // MXU_COLS):
        cs = slice(c * MXU_COLS, (c + 1) * MXU_COLS)
        gate = conv(wg_ref, cwg_ref, cbg_ref, cs)
        val = conv(wv_ref, cwv_ref, cbv_ref, cs)
        o_ref[:, cs] = (jax.nn.silu(gate) * val).astype(o_ref.dtype)


def _ffn_up(xb, w_up, layer, conv_w, conv_b, conv_layer, seq, tm=1024, tn=D_FF // 2):
    t = xb.shape[0]
    nf = D_FF // tn
    halo = BF16_ROWS
    resident = pl.Buffered(1)
    return pl.pallas_call(
        functools.partial(_ffn_up_kernel, tm=tm, tn=tn, tiles_per_seq=seq // tm),
        grid=(nf, t // tm),
        in_specs=[
            pl.BlockSpec((halo, D_MODEL), lambda j, i: (jnp.maximum(i * (tm // halo) - 1, 0), 0)),
            pl.BlockSpec((tm, D_MODEL), lambda j, i: (i, 0)),
            pl.BlockSpec((D_MODEL, tn), lambda j, i: (layer, j), pipeline_mode=resident),
            pl.BlockSpec((D_MODEL, tn), lambda j, i: (layer, nf + j), pipeline_mode=resident),
            pl.BlockSpec((None, CONV_WIDTH, tn), lambda j, i: (conv_layer, 0, j)),
            pl.BlockSpec((None, CONV_WIDTH, tn), lambda j, i: (conv_layer, 0, nf + j)),
            pl.BlockSpec((None, 1, tn), lambda j, i: (conv_layer, 0, j)),
            pl.BlockSpec((None, 1, tn), lambda j, i: (conv_layer, 0, nf + j)),
        ],
        out_specs=pl.BlockSpec((tm, tn), lambda j, i: (i, j)),
        out_shape=jax.ShapeDtypeStruct((t, D_FF), BF16),
        compiler_params=_params("arbitrary", "parallel"),
        name="ffn_up",
    )(xb, xb, w_up, w_up, conv_w, conv_w, conv_b, conv_b)


def _down_ln_kernel(a_ref, w_ref, x_ref, g_ref, b_ref, *out_refs):
    z = ALPHA * x_ref[...] + _dot(a_ref[...], w_ref[...])
    y = _layer_norm(z, g_ref[...], b_ref[...])
    out_refs[0][...] = y
    if len(out_refs) > 1:
        out_refs[1][...] = y.astype(BF16)


def _down_ln(act, w_down, layer, x, g, b, with_bf16, tm=256, cast=()):
    t = x.shape[0]
    row = lambda: pl.BlockSpec((tm, D_MODEL), lambda i: (i, 0))
    vec = lambda: pl.BlockSpec((1, D_MODEL), lambda i: (0, 0))
    out_shape = [jax.ShapeDtypeStruct((t, D_MODEL), F32)]
    if with_bf16:
        out_shape.append(jax.ShapeDtypeStruct((t, D_MODEL), BF16))
    return _call(
        _down_ln_kernel,
        [act, w_down, x, g, b],
        [
            pl.BlockSpec((tm, D_FF), lambda i: (i, 0)),
            pl.BlockSpec((D_FF, D_MODEL), lambda i: (layer, 0), pipeline_mode=pl.Buffered(1)),
            row(), vec(), vec(),
        ],
        [row() for _ in out_shape],
        out_shape,
        grid=(t // tm,), sem=("parallel",), name="down_ln", cast=cast)


def kernel(x, w_in, sinks, lambda_q1, lambda_k1, lambda_q2, lambda_k2, subln_g, w_proj_a, w_proj_b,
           w_out, ln1_g, ln1_b, w_up, conv_w, conv_b, w_down, ln2_g, ln2_b):
    batch, seq, _ = x.shape
    t = batch * seq
    tab_a, tab_b = _rope_tables(seq)
    x = x.reshape(t, D_MODEL)
    xb = None
    row = lambda v: v.reshape(1, -1).astype(F32)
    rows = lambda w: w.reshape(-1, w.shape[-1])
    whole = lambda w: (rows(w), 0, rows(w).shape[0])
    layer0 = lambda w: (rows(w), 0, w.shape[1])
    later = lambda w: (rows(w), w.shape[1], (DEPTH - 1) * w.shape[1])
    conv_w = conv_w.astype(F32)
    conv_b = conv_b.astype(F32).reshape(DEPTH, 1, 2 * D_FF)
    w_in_b, w_in_row = None, 0
    w_up_b, w_up_row = None, 0
    for l in range(DEPTH):
        first = l == 0
        if first:
            qkv_a, xb, w_in_b, w_up_b = _proj_a(x, w_in[0, :, :W_A].astype(BF16), 0, tab_a, seq,
                                                cast=[layer0(w_in), layer0(w_up)])
        else:
            qkv_a, = _proj_a(xb, w_in_b, w_in_row, tab_a, seq)
        qt_b, k_b = _proj_qk_b(xb, w_in_b, w_in_row, tab_b, batch, seq)
        vt_b, o_a, *done = _vt_attn_a(xb, w_in_b, w_in_row, qkv_a, sinks[l].astype(F32), batch, seq, ATTN_B_TK,
                                      cast=[whole(w_proj_a), whole(w_proj_b), whole(w_out)] if first else [])
        if first:
            w_pa, w_pb, w_o = done
        gates, *done = _proj_gates(xb, w_in_b, w_in_row, cast=[later(w_up)] if first and DEPTH > 1 else [])
        if done:
            w_up_rest, = done
        lam_init = 0.8 - 0.6 * math.exp(-0.3 * l)
        o_b = _attn_b(qt_b, k_b, vt_b, row(lambda_q1[l]), row(lambda_k1[l]), row(lambda_q2[l]), row(lambda_k2[l]),
                      subln_g[l].reshape(-1, 1).astype(F32), lam_init, batch, seq, tk=ATTN_B_TK)
        merged, = _merge(o_a, o_b, w_pa, w_pb, l, gates)
        x, xb, *done = _out_ln(merged, w_o, l, x, row(ln1_g[l]), row(ln1_b[l]),
                               cast=[whole(w_down)] if first else [])
        if first:
            w_d, = done
        act = _ffn_up(xb, w_up_b, w_up_row, conv_w, conv_b, l, seq)
        last = l == DEPTH - 1
        x, *rest = _down_ln(act, w_d, l, x, row(ln2_g[l]), row(ln2_b[l]), with_bf16=not last,
                            cast=[later(w_in)] if first and DEPTH > 1 else [])
        if not last:
            xb = rest[0]
        if first and DEPTH > 1:
            w_in_b, w_up_b = rest[-1], w_up_rest
        if not last:
            w_in_row, w_up_row = l, l
    return x.reshape(batch, seq, D_MODEL)
```

```python
import functools
import math

import jax
import jax.numpy as jnp
from jax import lax
from jax.experimental import pallas as pl
from jax.experimental.pallas import tpu as pltpu

D_MODEL = 2048
DEPTH = 2
HEAD_DIM_A = 64
N_Q_A = 32
N_KV_A = 4
GROUP_A = N_Q_A // N_KV_A
WINDOW = 128
HEAD_DIM_B = 128
N_HEADS_B = D_MODEL // (2 * HEAD_DIM_B)
SUBLN_EPS = 1e-5
ROPE_THETA = 10000.0
D_FF = ((8 * D_MODEL // 3 + 255) // 256) * 256
CONV_WIDTH = 3
LN_EPS = 1e-5
ALPHA = (2 * DEPTH) ** 0.25
NEG_INF = -1e30
LOG2_E = math.log2(math.e)

W_QA = N_Q_A * HEAD_DIM_A
W_KA = N_KV_A * HEAD_DIM_A
W_A = W_QA + 2 * W_KA
W_QB = N_HEADS_B * 2 * HEAD_DIM_B

LANES = 128
MXU_COLS = 256
BF16_ROWS = 16
VMEM_LIMIT = 56 * 1024 * 1024
ATTN_B_TK = 512

F32 = jnp.float32
BF16 = jnp.bfloat16


def _params(*sem):
    return pltpu.CompilerParams(dimension_semantics=sem, vmem_limit_bytes=VMEM_LIMIT)


def _dot(a, b):
    return jnp.dot(a, b, preferred_element_type=F32)


def _dot_nt(a, b):
    return lax.dot_general(a, b, (((1,), (1,)), ((), ())), preferred_element_type=F32)


def _layer_norm(z, g, b):
    mu = jnp.mean(z, axis=-1, keepdims=True)
    zc = z - mu
    var = jnp.mean(zc * zc, axis=-1, keepdims=True)
    return zc * lax.rsqrt(var + LN_EPS) * g + b


def _call(kernel_fn, inputs, in_specs, out_specs, out_shape, *, grid, sem, name, cast=()):
    n_in, n_out, n_cast = len(inputs), len(out_specs), len(cast)
    n_steps = math.prod(grid)

    def step(*g):
        s = 0
        for g_d, n_d in zip(g, grid):
            s = s * n_d + g_d
        return s

    def slabs(offset):
        specs = []
        for w, first_row, n_rows in cast:
            slab = n_rows // n_steps
            assert n_rows % (n_steps * BF16_ROWS) == 0 and first_row % slab == 0
            first = first_row // slab if offset else 0
            specs.append(pl.BlockSpec((slab, w.shape[1]), lambda *g, first=first: (first + step(*g), 0)))
        return specs

    def body(*refs):
        ins, src = refs[:n_in], refs[n_in:n_in + n_cast]
        outs = refs[n_in + n_cast:n_in + n_cast + n_out]
        dst = refs[n_in + n_cast + n_out:]
        for s_ref, d_ref in zip(src, dst):
            d_ref[...] = s_ref[...].astype(d_ref.dtype)
        kernel_fn(*ins, *outs)

    return pl.pallas_call(
        body,
        grid=grid,
        in_specs=[*in_specs, *slabs(True)],
        out_specs=[*out_specs, *slabs(False)],
        out_shape=[*out_shape, *[jax.ShapeDtypeStruct((n_rows, w.shape[1]), BF16) for w, _, n_rows in cast]],
        compiler_params=_params(*sem),
        name=name,
    )(*inputs, *[w for w, _, _ in cast])


def _rope_tables(seq):
    def cs(dim):
        inv = 1.0 / (ROPE_THETA ** (jnp.arange(0, dim, 2, dtype=F32) / dim))
        ang = jnp.arange(seq, dtype=F32)[:, None] * inv[None, :]
        return jnp.cos(ang), jnp.sin(ang)

    cos_a, sin_a = cs(HEAD_DIM_A)
    cos_b, sin_b = cs(HEAD_DIM_B)
    zeros_a = jnp.zeros_like(sin_a)
    heads = LANES // HEAD_DIM_A
    tab_a = jnp.stack([
        jnp.tile(cos_a, (1, 2 * heads)),
        jnp.tile(jnp.concatenate([zeros_a, sin_a], axis=1), (1, heads)),
        jnp.tile(jnp.concatenate([-sin_a, zeros_a], axis=1), (1, heads)),
    ])
    assert HEAD_DIM_B == LANES
    tab_b = jnp.stack([
        jnp.concatenate([cos_b, cos_b], axis=1),
        jnp.concatenate([-sin_b, sin_b], axis=1),
    ])
    return tab_a, tab_b


def _proj_a_kernel(x_ref, w_ref, tab_ref, o_ref, *xb_refs):
    x = x_ref[...].astype(BF16)
    if xb_refs:
        xb_refs[0][...] = x
    acc = _dot(x, w_ref[...])
    cos, hi, lo = tab_ref[0], tab_ref[1], tab_ref[2]
    scale = LOG2_E * HEAD_DIM_A ** -0.5
    n_rope = (W_QA + W_KA) // LANES
    for g in range(W_A // LANES):
        xg = acc[:, g * LANES:(g + 1) * LANES]
        if g < n_rope:
            half = HEAD_DIM_A // 2
            xg = xg * cos + pltpu.roll(xg, half, axis=1) * hi + pltpu.roll(xg, LANES - half, axis=1) * lo
            if g < W_QA // LANES:
                xg = xg * scale
        o_ref[:, g * LANES:(g + 1) * LANES] = xg.astype(o_ref.dtype)


def _proj_a(x, w_in_b, layer, tab_a, seq, tm=512, cast=()):
    t = x.shape[0]
    out_specs = [pl.BlockSpec((tm, W_A), lambda i: (i, 0))]
    out_shape = [jax.ShapeDtypeStruct((t, W_A), BF16)]
    if x.dtype != BF16:
        out_specs.append(pl.BlockSpec((tm, D_MODEL), lambda i: (i, 0)))
        out_shape.append(jax.ShapeDtypeStruct((t, D_MODEL), BF16))
    return _call(
        _proj_a_kernel,
        [x, w_in_b, tab_a],
        [
            pl.BlockSpec((tm, D_MODEL), lambda i: (i, 0)),
            pl.BlockSpec((D_MODEL, W_A), lambda i: (layer, 0)),
            pl.BlockSpec((3, tm, LANES), lambda i: (0, i % (seq // tm), 0)),
        ],
        out_specs,
        out_shape,
        grid=(t // tm,), sem=("parallel",), name="proj_a", cast=cast)


def _proj_rope_b_kernel(x_ref, w_ref, tab_ref, qt_ref, k_ref):
    scale = LOG2_E * HEAD_DIM_B ** -0.5
    x = x_ref[...]
    w = 2 * HEAD_DIM_B
    assert w == MXU_COLS and HEAD_DIM_B == LANES
    for c in range(2 * N_HEADS_B):
        acc = _dot(x, w_ref[:, c * w:(c + 1) * w])
        for g in range(2):
            xg = acc[:, g * LANES:(g + 1) * LANES]
            xg = xg * tab_ref[0] + pltpu.roll(xg, HEAD_DIM_B // 2, axis=1) * tab_ref[1]
            if c < N_HEADS_B:
                qt_ref[c, g] = (xg * scale).astype(qt_ref.dtype).T
            else:
                lo = (c - N_HEADS_B) * w + g * LANES
                k_ref[:, lo:lo + LANES] = xg.astype(k_ref.dtype)


def _w_in_cols(layer, col0, width):
    return pl.BlockSpec((pl.Element(D_MODEL), pl.Element(width)),
                        lambda i, j: (layer * D_MODEL, pl.multiple_of(col0 + j * width, LANES)))


def _proj_qk_b(xb, w_in_b, layer, tab_b, batch, seq, tm=1024, cast=()):
    t = xb.shape[0]
    tiles_per_seq = seq // tm
    w_spec = pl.BlockSpec((pl.Element(D_MODEL), pl.Element(2 * W_QB)), lambda i: (layer * D_MODEL, W_A),
                          pipeline_mode=pl.Buffered(1))
    return _call(
        _proj_rope_b_kernel,
        [xb, w_in_b, tab_b],
        [
            pl.BlockSpec((tm, D_MODEL), lambda i: (i, 0)),
            w_spec,
            pl.BlockSpec((2, tm, LANES), lambda i: (0, i % tiles_per_seq, 0)),
        ],
        [
            pl.BlockSpec((None, N_HEADS_B, 2, HEAD_DIM_B, tm),
                         lambda i: (i // tiles_per_seq, 0, 0, 0, i % tiles_per_seq)),
            pl.BlockSpec((tm, W_QB), lambda i: (i, 0)),
        ],
        [
            jax.ShapeDtypeStruct((batch, N_HEADS_B, 2, HEAD_DIM_B, seq), BF16),
            jax.ShapeDtypeStruct((t, W_QB), BF16),
        ],
        grid=(t // tm,), sem=("parallel",), name="proj_qk_b", cast=cast)


def _proj_gates_kernel(x_ref, w_ref, o_ref, *, tn):
    x = x_ref[...]
    for c in range(tn // MXU_COLS):
        cs = slice(c * MXU_COLS, (c + 1) * MXU_COLS)
        o_ref[:, cs] = jax.nn.sigmoid(_dot(x, w_ref[:, cs])).astype(o_ref.dtype)


def _proj_gates(xb, w_in_b, layer, tm=1024, tn=2048, cast=()):
    t = xb.shape[0]
    n = 2 * D_MODEL
    return _call(
        functools.partial(_proj_gates_kernel, tn=tn),
        [xb, w_in_b],
        [
            pl.BlockSpec((tm, D_MODEL), lambda i, j: (i, 0)),
            _w_in_cols(layer, W_A + 3 * W_QB, tn),
        ],
        [pl.BlockSpec((tm, tn), lambda i, j: (i, j))],
        [jax.ShapeDtypeStruct((t, n), BF16)],
        grid=(t // tm, n // tn), sem=("parallel", "arbitrary"), name="proj_gates", cast=cast)


def _attn_a_block(sinks_ref, q_ref, kvp_ref, kvc_ref, o_ref, has_prev):
    blk = WINDOW
    hd = HEAD_DIM_A
    key = lax.broadcasted_iota(jnp.int32, (2 * blk, blk), 0)
    qry = lax.broadcasted_iota(jnp.int32, (2 * blk, blk), 1)
    mask = (key <= blk + qry) & (key > qry)
    if has_prev is not True:
        mask = mask & ((key >= blk) | has_prev)
    vpt = kvp_ref[:, W_KA:].T
    vct = kvc_ref[:, W_KA:].T

    for g in range(N_KV_A):
        ks = slice(g * hd, (g + 1) * hd)
        kband = jnp.concatenate([kvp_ref[:, ks], kvc_ref[:, ks]], axis=0)
        vt = jnp.concatenate([vpt[ks], vct[ks]], axis=1)
        qts = []
        for pr in range(GROUP_A // 2):
            h0 = g * GROUP_A + 2 * pr
            t = q_ref[:, h0 * hd:(h0 + 2) * hd].T
            qts += [t[:hd], t[hd:]]
        qt = jnp.concatenate(qts, axis=1)
        s = _dot(kband, qt)
        ps, rs = [], []
        for hh in range(GROUP_A):
            sink = sinks_ref[g * GROUP_A + hh] * LOG2_E
            sh = jnp.where(mask, s[:, hh * blk:(hh + 1) * blk], NEG_INF)
            m = jnp.maximum(jnp.max(sh, axis=0, keepdims=True), sink)
            p = jnp.exp2(sh - m)
            denom = jnp.sum(p, axis=0, keepdims=True) + jnp.exp2(sink - m)
            ps.append(p.astype(BF16))
            rs.append(1.0 / denom)
        ot = _dot(vt, jnp.concatenate(ps, axis=1))
        for pr in range(GROUP_A // 2):
            h0 = g * GROUP_A + 2 * pr
            pair = jnp.concatenate([ot[:, (2 * pr + e) * blk:(2 * pr + e + 1) * blk] * rs[2 * pr + e]
                                    for e in range(2)], axis=0)
            o_ref[:, h0 * hd:(h0 + 2) * hd] = pair.astype(o_ref.dtype).T


def _vt_attn_a_kernel(sinks_ref, x_ref, w_ref, q_ref, kvp_ref, kvc_ref, vt_ref, o_ref,
                      *, tk, blocks_per_seq):
    blk = WINDOW
    n_blocks = o_ref.shape[0] // blk
    first = (pl.program_id(0) * pl.num_programs(1) + pl.program_id(1)) * n_blocks
    x = x_ref[...]
    w = 2 * HEAD_DIM_B
    for c in range(max(vt_ref.shape[0], n_blocks)):
        if c < vt_ref.shape[0]:
            acc = _dot(x, w_ref[:, c * w:(c + 1) * w]).astype(vt_ref.dtype)
            for kk in range(vt_ref.shape[1]):
                vt_ref[c, kk] = acc[kk * tk:(kk + 1) * tk].T
        if c < n_blocks:
            rows = pl.ds(c * blk, blk)
            prev = kvp_ref if c == 0 else kvc_ref.at[pl.ds((c - 1) * blk, blk)]
            has_prev = (first % blocks_per_seq != 0) if c == 0 else True
            _attn_a_block(sinks_ref, q_ref.at[rows], prev, kvc_ref.at[rows], o_ref.at[rows], has_prev)


def _vt_attn_a(xb, w_in_b, layer, qkv_a, sinks, batch, seq, tk, tm=1024, heads=4, n_blocks=4, cast=()):
    t = xb.shape[0]
    blk = WINDOW
    w = 2 * HEAD_DIM_B
    tiles_per_seq = seq // tm
    nj = N_HEADS_B // heads
    assert (t // tm) * nj * n_blocks * blk == t and (seq // blk) % n_blocks == 0
    kv_col = W_QA // (2 * W_KA)
    step = lambda i, j: i * nj + j
    return _call(
        functools.partial(_vt_attn_a_kernel, tk=tk, blocks_per_seq=seq // blk),
        [sinks, xb, w_in_b, qkv_a, qkv_a, qkv_a],
        [
            pl.BlockSpec(memory_space=pltpu.SMEM),
            pl.BlockSpec((tm, D_MODEL), lambda i, j: (i, 0)),
            _w_in_cols(layer, W_A + 2 * W_QB, heads * w),
            pl.BlockSpec((n_blocks * blk, W_QA), lambda i, j: (step(i, j), 0)),
            pl.BlockSpec((blk, 2 * W_KA), lambda i, j: (jnp.maximum(step(i, j) * n_blocks - 1, 0), kv_col)),
            pl.BlockSpec((n_blocks * blk, 2 * W_KA), lambda i, j: (step(i, j), kv_col)),
        ],
        [
            pl.BlockSpec((None, heads, tm // tk, w, tk),
                         lambda i, j: (i // tiles_per_seq, j, i % tiles_per_seq, 0, 0)),
            pl.BlockSpec((n_blocks * blk, W_QA), lambda i, j: (step(i, j), 0)),
        ],
        [
            jax.ShapeDtypeStruct((batch, N_HEADS_B, seq // tk, w, tk), BF16),
            jax.ShapeDtypeStruct((t, W_QA), BF16),
        ],
        grid=(t // tm, nj), sem=("parallel", "arbitrary"), name="vt_attn_a", cast=cast)


def _attn_b_kernel(lq1_ref, lk1_ref, lq2_ref, lk2_ref, g_ref, qt_ref, k_ref, vt_ref, o_ref,
                   m_ref, l_ref, acc_ref, *, tq, tk, n_q_tiles, lam_init):
    qi = pl.program_id(2)
    hd = HEAD_DIM_B
    cg = MXU_COLS
    n_groups = tq // cg

    def chain(c, j, k, vt, first_key, start):
        js = slice(j * cg, (j + 1) * cg)
        s = _dot(k[:, c * hd:(c + 1) * hd], qt_ref[c, :, js])
        if first_key is not None:
            key = lax.broadcasted_iota(jnp.int32, s.shape, 0)
            qry = lax.broadcasted_iota(jnp.int32, s.shape, 1)
            s = jnp.where(key + first_key <= qry, s, NEG_INF)
        m_new = jnp.max(s, axis=0, keepdims=True)
        if not start:
            m_old = m_ref[c, :, js]
            m_new = jnp.maximum(m_old, m_new)
            a = jnp.exp2(m_old - m_new)
        p = jnp.exp2(s - m_new)
        l_new = jnp.sum(p, axis=0, keepdims=True)
        acc_new = _dot(vt, p.astype(BF16))
        if not start:
            l_new = a * l_ref[c, :, js] + l_new
            acc_new = a * acc_ref[c, :, js] + acc_new
        l_ref[c, :, js] = l_new
        acc_ref[c, :, js] = acc_new
        m_ref[c, :, js] = m_new

    blocks_per_tile = tq // tk

    def finish(j):
        js = slice(j * cg, (j + 1) * cg)
        o = acc_ref[0, :, js] * (1.0 / l_ref[0, :, js]) - lam * (acc_ref[1, :, js] * (1.0 / l_ref[1, :, js]))
        o = o * lax.rsqrt(jnp.mean(o * o, axis=0, keepdims=True) + SUBLN_EPS)
        o = o * g_ref[...] * (1.0 - lam_init)
        o_ref[js, :] = o.T.astype(o_ref.dtype)

    def tile(q_tile):
        started = set()

        def run(c, j, k, vt, first_key):
            chain(c, j, k, vt, first_key, (c, j) not in started)
            started.add((c, j))

        for kb in range(q_tile * blocks_per_tile):
            k = k_ref[kb * tk:(kb + 1) * tk, :]
            vt = vt_ref[kb]
            for j in range(n_groups):
                for c in range(2):
                    run(c, j, k, vt, None)
        for u in range(blocks_per_tile):
            kb = q_tile * blocks_per_tile + u
            k = k_ref[kb * tk:(kb + 1) * tk, :]
            vt = vt_ref[kb]
            for j in range(n_groups):
                first_key = u * tk - j * cg
                n_vis = min(tk, (j + 1) * cg - u * tk)
                if n_vis <= 0:
                    continue
                masked = first_key + n_vis - 1 > 0
                for c in range(2):
                    run(c, j, k[:n_vis], vt[:, :n_vis], first_key if masked else None)
        for j in range(n_groups):
            finish(j)

    lam = (jnp.exp(jnp.sum(lq1_ref[...] * lk1_ref[...], axis=-1, keepdims=True))
           - jnp.exp(jnp.sum(lq2_ref[...] * lk2_ref[...], axis=-1, keepdims=True))
           + lam_init)
    for q_tile in range(n_q_tiles):
        pl.when(qi == q_tile)(functools.partial(tile, q_tile))


def _attn_b(qt_b, k_b, vt, lq1, lk1, lq2, lk2, subln_g, lam_init, batch, seq, tq=4096, tk=512):
    nq = seq // tq
    w = 2 * HEAD_DIM_B
    vec = lambda n: pl.BlockSpec((1, n), lambda b, h, i: (0, 0))
    return pl.pallas_call(
        functools.partial(_attn_b_kernel, tq=tq, tk=tk, n_q_tiles=nq, lam_init=lam_init),
        grid=(batch, N_HEADS_B, nq),
        in_specs=[
            vec(HEAD_DIM_B), vec(HEAD_DIM_B), vec(HEAD_DIM_B), vec(HEAD_DIM_B),
            pl.BlockSpec((w, 1), lambda b, h, i: (0, 0)),
            pl.BlockSpec((None, None, 2, HEAD_DIM_B, tq), lambda b, h, i: (b, h, 0, 0, i)),
            pl.BlockSpec((seq, w), lambda b, h, i: (b, h)),
            pl.BlockSpec((None, None, seq // tk, w, tk), lambda b, h, i: (b, h, 0, 0, 0)),
        ],
        out_specs=pl.BlockSpec((tq, w), lambda b, h, i: (b * nq + i, h)),
        out_shape=jax.ShapeDtypeStruct((batch * seq, W_QB), BF16),
        scratch_shapes=[
            pltpu.VMEM((2, 1, tq), F32),
            pltpu.VMEM((2, 1, tq), F32),
            pltpu.VMEM((2, w, tq), F32),
        ],
        compiler_params=_params("parallel", "parallel", "arbitrary"),
        name="attn_b",
    )(lq1, lk1, lq2, lk2, subln_g, qt_b, k_b, vt)


def _merge_kernel(oa_ref, ob_ref, wa_ref, wb_ref, ga_ref, gb_ref, o_ref):
    pa = _dot(oa_ref[...], wa_ref[...])
    pb = _dot(ob_ref[...], wb_ref[...])
    o_ref[...] = (ga_ref[...].astype(F32) * pa + gb_ref[...].astype(F32) * pb).astype(o_ref.dtype)


def _merge(o_a, o_b, w_pa, w_pb, layer, gates, tm=1024, tn=1024, cast=()):
    t = o_a.shape[0]
    return _call(
        _merge_kernel,
        [o_a, o_b, w_pa, w_pb, gates, gates],
        [
            pl.BlockSpec((tm, W_QA), lambda i, j: (i, 0)),
            pl.BlockSpec((tm, W_QB), lambda i, j: (i, 0)),
            pl.BlockSpec((W_QA, tn), lambda i, j: (layer, j)),
            pl.BlockSpec((W_QB, tn), lambda i, j: (layer, j)),
            pl.BlockSpec((tm, tn), lambda i, j: (i, j)),
            pl.BlockSpec((tm, tn), lambda i, j: (i, D_MODEL // tn + j)),
        ],
        [pl.BlockSpec((tm, tn), lambda i, j: (i, j))],
        [jax.ShapeDtypeStruct((t, D_MODEL), BF16)],
        grid=(t // tm, D_MODEL // tn), sem=("parallel", "arbitrary"), name="merge", cast=cast)


def _out_ln_kernel(m_ref, w_ref, x_ref, g_ref, b_ref, y_ref, yb_ref, *, rows):
    g, b = g_ref[...], b_ref[...]
    for r in range(m_ref.shape[0] // rows):
        rs = slice(r * rows, (r + 1) * rows)
        z = ALPHA * x_ref[rs, :] + _dot(m_ref[rs, :], w_ref[...])
        y = _layer_norm(z, g, b)
        y_ref[rs, :] = y
        yb_ref[rs, :] = y.astype(yb_ref.dtype)


def _out_ln(merged, w_out, layer, x, g, b, tm=512, rows=256, cast=()):
    t = x.shape[0]
    row = lambda: pl.BlockSpec((tm, D_MODEL), lambda i: (i, 0))
    vec = lambda: pl.BlockSpec((1, D_MODEL), lambda i: (0, 0))
    return _call(
        functools.partial(_out_ln_kernel, rows=rows),
        [merged, w_out, x, g, b],
        [row(), pl.BlockSpec((D_MODEL, D_MODEL), lambda i: (layer, 0), pipeline_mode=pl.Buffered(1)),
         row(), vec(), vec()],
        [row(), row()],
        [jax.ShapeDtypeStruct((t, D_MODEL), F32), jax.ShapeDtypeStruct((t, D_MODEL), BF16)],
        grid=(t // tm,), sem=("parallel",), name="out_ln", cast=cast)


def _ffn_up_kernel(xp_ref, x_ref, wg_ref, wv_ref, cwg_ref, cwv_ref, cbg_ref, cbv_ref, o_ref,
                   *, tm, tn, tiles_per_seq):
    i = pl.program_id(1)
    halo = BF16_ROWS
    xp = jnp.where(i % tiles_per_seq == 0, jnp.zeros((), BF16), xp_ref[...])
    xe = jnp.concatenate([xp, x_ref[...]], axis=0)

    def conv(w_ref, cw_ref, cb_ref, cs):
        u = _dot(xe, w_ref[:, cs])
        out = cb_ref[:, cs] + cw_ref[CONV_WIDTH - 1:CONV_WIDTH, cs] * u[halo:]
        for back in range(1, CONV_WIDTH):
            tap = CONV_WIDTH - 1 - back
            out = out + cw_ref[tap:tap + 1, cs] * pltpu.roll(u, back, axis=0)[halo:]
        return out

    for c in range(tn // MXU_COLS):
        cs = slice(c * MXU_COLS, (c + 1) * MXU_COLS)
        gate = conv(wg_ref, cwg_ref, cbg_ref, cs)
        val = conv(wv_ref, cwv_ref, cbv_ref, cs)
        o_ref[:, cs] = (jax.nn.silu(gate) * val).astype(o_ref.dtype)


def _ffn_up(xb, w_up, layer, conv_w, conv_b, conv_layer, seq, tm=1024, tn=D_FF // 2):
    t = xb.shape[0]
    nf = D_FF // tn
    halo = BF16_ROWS
    resident = pl.Buffered(1)
    return pl.pallas_call(
        functools.partial(_ffn_up_kernel, tm=tm, tn=tn, tiles_per_seq=seq // tm),
        grid=(nf, t // tm),
        in_specs=[
            pl.BlockSpec((halo, D_MODEL), lambda j, i: (jnp.maximum(i * (tm // halo) - 1, 0), 0)),
            pl.BlockSpec((tm, D_MODEL), lambda j, i: (i, 0)),
            pl.BlockSpec((D_MODEL, tn), lambda j, i: (layer, j), pipeline_mode=resident),
            pl.BlockSpec((D_MODEL, tn), lambda j, i: (layer, nf + j), pipeline_mode=resident),
            pl.BlockSpec((None, CONV_WIDTH, tn), lambda j, i: (conv_layer, 0, j)),
            pl.BlockSpec((None, CONV_WIDTH, tn), lambda j, i: (conv_layer, 0, nf + j)),
            pl.BlockSpec((None, 1, tn), lambda j, i: (conv_layer, 0, j)),
            pl.BlockSpec((None, 1, tn), lambda j, i: (conv_layer, 0, nf + j)),
        ],
        out_specs=pl.BlockSpec((tm, tn), lambda j, i: (i, j)),
        out_shape=jax.ShapeDtypeStruct((t, D_FF), BF16),
        compiler_params=_params("arbitrary", "parallel"),
        name="ffn_up",
    )(xb, xb, w_up, w_up, conv_w, conv_w, conv_b, conv_b)


def _down_ln_kernel(a_ref, w_ref, x_ref, g_ref, b_ref, *out_refs):
    z = ALPHA * x_ref[...] + _dot(a_ref[...], w_ref[...])
    y = _layer_norm(z, g_ref[...], b_ref[...])
    out_refs[0][...] = y
    if len(out_refs) > 1:
        out_refs[1][...] = y.astype(BF16)


def _down_ln(act, w_down, layer, x, g, b, with_bf16, tm=256, cast=()):
    t = x.shape[0]
    row = lambda: pl.BlockSpec((tm, D_MODEL), lambda i: (i, 0))
    vec = lambda: pl.BlockSpec((1, D_MODEL), lambda i: (0, 0))
    out_shape = [jax.ShapeDtypeStruct((t, D_MODEL), F32)]
    if with_bf16:
        out_shape.append(jax.ShapeDtypeStruct((t, D_MODEL), BF16))
    return _call(
        _down_ln_kernel,
        [act, w_down, x, g, b],
        [
            pl.BlockSpec((tm, D_FF), lambda i: (i, 0)),
            pl.BlockSpec((D_FF, D_MODEL), lambda i: (layer, 0), pipeline_mode=pl.Buffered(1)),
            row(), vec(), vec(),
        ],
        [row() for _ in out_shape],
        out_shape,
        grid=(t // tm,), sem=("parallel",), name="down_ln", cast=cast)


def kernel(x, w_in, sinks, lambda_q1, lambda_k1, lambda_q2, lambda_k2, subln_g, w_proj_a, w_proj_b,
           w_out, ln1_g, ln1_b, w_up, conv_w, conv_b, w_down, ln2_g, ln2_b):
    batch, seq, _ = x.shape
    t = batch * seq
    tab_a, tab_b = _rope_tables(seq)
    x = x.reshape(t, D_MODEL)
    xb = None
    row = lambda v: v.reshape(1, -1).astype(F32)
    rows = lambda w: w.reshape(-1, w.shape[-1])
    whole = lambda w: (rows(w), 0, rows(w).shape[0])
    layer0 = lambda w: (rows(w), 0, w.shape[1])
    later = lambda w: (rows(w), w.shape[1], (DEPTH - 1) * w.shape[1])
    conv_w = conv_w.astype(F32)
    conv_b = conv_b.astype(F32).reshape(DEPTH, 1, 2 * D_FF)
    w_in_b, w_in_row = None, 0
    w_up_b, w_up_row = None, 0
    for l in range(DEPTH):
        first = l == 0
        if first:
            qkv_a, xb, w_in_b, w_up_b = _proj_a(x, w_in[0, :, :W_A].astype(BF16), 0, tab_a, seq,
                                                cast=[layer0(w_in), layer0(w_up)])
        else:
            qkv_a, = _proj_a(xb, w_in_b, w_in_row, tab_a, seq)
        qt_b, k_b = _proj_qk_b(xb, w_in_b, w_in_row, tab_b, batch, seq)
        vt_b, o_a, *done = _vt_attn_a(xb, w_in_b, w_in_row, qkv_a, sinks[l].astype(F32), batch, seq, ATTN_B_TK,
                                      cast=[whole(w_proj_a), whole(w_proj_b), whole(w_out)] if first else [])
        if first:
            w_pa, w_pb, w_o = done
        gates, *done = _proj_gates(xb, w_in_b, w_in_row, cast=[later(w_up)] if first and DEPTH > 1 else [])
        if done:
            w_up_rest, = done
        lam_init = 0.8 - 0.6 * math.exp(-0.3 * l)
        o_b = _attn_b(qt_b, k_b, vt_b, row(lambda_q1[l]), row(lambda_k1[l]), row(lambda_q2[l]), row(lambda_k2[l]),
                      subln_g[l].reshape(-1, 1).astype(F32), lam_init, batch, seq, tk=ATTN_B_TK)
        merged, = _merge(o_a, o_b, w_pa, w_pb, l, gates)
        x, xb, *done = _out_ln(merged, w_o, l, x, row(ln1_g[l]), row(ln1_b[l]),
                               cast=[whole(w_down)] if first else [])
        if first:
            w_d, = done
        act = _ffn_up(xb, w_up_b, w_up_row, conv_w, conv_b, l, seq)
        last = l == DEPTH - 1
        x, *rest = _down_ln(act, w_d, l, x, row(ln2_g[l]), row(ln2_b[l]), with_bf16=not last,
                            cast=[later(w_in)] if first and DEPTH > 1 else [])
        if not last:
            xb = rest[0]
        if first and DEPTH > 1:
            w_in_b, w_up_b = rest[-1], w_up_rest
        if not last:
            w_in_row, w_up_row = l, l
    return x.reshape(batch, seq, D_MODEL)
```

```python
import functools
import math

import jax
import jax.numpy as jnp
from jax import lax
from jax.experimental import pallas as pl
from jax.experimental.pallas import tpu as pltpu

D_MODEL = 2048
DEPTH = 2
HEAD_DIM_A = 64
N_Q_A = 32
N_KV_A = 4
GROUP_A = N_Q_A // N_KV_A
WINDOW = 128
HEAD_DIM_B = 128
N_HEADS_B = D_MODEL // (2 * HEAD_DIM_B)
SUBLN_EPS = 1e-5
ROPE_THETA = 10000.0
D_FF = ((8 * D_MODEL // 3 + 255) // 256) * 256
CONV_WIDTH = 3
LN_EPS = 1e-5
ALPHA = (2 * DEPTH) ** 0.25
NEG_INF = -1e30
LOG2_E = math.log2(math.e)

W_QA = N_Q_A * HEAD_DIM_A
W_KA = N_KV_A * HEAD_DIM_A
W_A = W_QA + 2 * W_KA
W_QB = N_HEADS_B * 2 * HEAD_DIM_B

LANES = 128
MXU_COLS = 256
BF16_ROWS = 16
VMEM_LIMIT = 56 * 1024 * 1024
ATTN_B_TK = 512

F32 = jnp.float32
BF16 = jnp.bfloat16


def _params(*sem):
    return pltpu.CompilerParams(dimension_semantics=sem, vmem_limit_bytes=VMEM_LIMIT)


def _dot(a, b):
    return jnp.dot(a, b, preferred_element_type=F32)


def _dot_nt(a, b):
    return lax.dot_general(a, b, (((1,), (1,)), ((), ())), preferred_element_type=F32)


def _layer_norm(z, g, b):
    mu = jnp.mean(z, axis=-1, keepdims=True)
    zc = z - mu
    var = jnp.mean(zc * zc, axis=-1, keepdims=True)
    return zc * lax.rsqrt(var + LN_EPS) * g + b


def _call(kernel_fn, inputs, in_specs, out_specs, out_shape, *, grid, sem, name, cast=(), aliases=None):
    n_in, n_out, n_cast = len(inputs), len(out_specs), len(cast)
    n_steps = math.prod(grid)

    def step(*g):
        s = 0
        for g_d, n_d in zip(g, grid):
            s = s * n_d + g_d
        return s

    def slabs(offset):
        specs = []
        for w, first_row, n_rows in cast:
            slab = n_rows // n_steps
            assert n_rows % (n_steps * BF16_ROWS) == 0 and first_row % slab == 0
            first = first_row // slab if offset else 0
            specs.append(pl.BlockSpec((slab, w.shape[1]), lambda *g, first=first: (first + step(*g), 0)))
        return specs

    def body(*refs):
        ins, src = refs[:n_in], refs[n_in:n_in + n_cast]
        outs = refs[n_in + n_cast:n_in + n_cast + n_out]
        dst = refs[n_in + n_cast + n_out:]
        for s_ref, d_ref in zip(src, dst):
            d_ref[...] = s_ref[...].astype(d_ref.dtype)
        kernel_fn(*ins, *outs)

    return pl.pallas_call(
        body,
        grid=grid,
        in_specs=[*in_specs, *slabs(True)],
        out_specs=[*out_specs, *slabs(False)],
        out_shape=[*out_shape, *[jax.ShapeDtypeStruct((n_rows, w.shape[1]), BF16) for w, _, n_rows in cast]],
        input_output_aliases=aliases or {},
        compiler_params=_params(*sem),
        name=name,
    )(*inputs, *[w for w, _, _ in cast])


def _rope_tables(seq):
    def cs(dim):
        inv = 1.0 / (ROPE_THETA ** (jnp.arange(0, dim, 2, dtype=F32) / dim))
        ang = jnp.arange(seq, dtype=F32)[:, None] * inv[None, :]
        return jnp.cos(ang), jnp.sin(ang)

    cos_a, sin_a = cs(HEAD_DIM_A)
    cos_b, sin_b = cs(HEAD_DIM_B)
    zeros_a = jnp.zeros_like(sin_a)
    heads = LANES // HEAD_DIM_A
    tab_a = jnp.stack([
        jnp.tile(cos_a, (1, 2 * heads)),
        jnp.tile(jnp.concatenate([zeros_a, sin_a], axis=1), (1, heads)),
        jnp.tile(jnp.concatenate([-sin_a, zeros_a], axis=1), (1, heads)),
    ])
    assert HEAD_DIM_B == LANES
    tab_b = jnp.stack([
        jnp.concatenate([cos_b, cos_b], axis=1),
        jnp.concatenate([-sin_b, sin_b], axis=1),
    ])
    return tab_a, tab_b


def _proj_a_kernel(x_ref, w_ref, tab_ref, o_ref, *xb_refs):
    x = x_ref[...].astype(BF16)
    if xb_refs:
        xb_refs[0][...] = x
    acc = _dot(x, w_ref[...])
    cos, hi, lo = tab_ref[0], tab_ref[1], tab_ref[2]
    scale = LOG2_E * HEAD_DIM_A ** -0.5
    n_rope = (W_QA + W_KA) // LANES
    for g in range(W_A // LANES):
        xg = acc[:, g * LANES:(g + 1) * LANES]
        if g < n_rope:
            half = HEAD_DIM_A // 2
            xg = xg * cos + pltpu.roll(xg, half, axis=1) * hi + pltpu.roll(xg, LANES - half, axis=1) * lo
            if g < W_QA // LANES:
                xg = xg * scale
        o_ref[:, g * LANES:(g + 1) * LANES] = xg.astype(o_ref.dtype)


def _proj_a(x, w_in_b, layer, tab_a, seq, tm=512, cast=()):
    t = x.shape[0]
    out_specs = [pl.BlockSpec((tm, W_A), lambda i: (i, 0))]
    out_shape = [jax.ShapeDtypeStruct((t, W_A), BF16)]
    if x.dtype != BF16:
        out_specs.append(pl.BlockSpec((tm, D_MODEL), lambda i: (i, 0)))
        out_shape.append(jax.ShapeDtypeStruct((t, D_MODEL), BF16))
    return _call(
        _proj_a_kernel,
        [x, w_in_b, tab_a],
        [
            pl.BlockSpec((tm, D_MODEL), lambda i: (i, 0)),
            pl.BlockSpec((D_MODEL, W_A), lambda i: (layer, 0)),
            pl.BlockSpec((3, tm, LANES), lambda i: (0, i % (seq // tm), 0)),
        ],
        out_specs,
        out_shape,
        grid=(t // tm,), sem=("parallel",), name="proj_a", cast=cast)


def _proj_rope_b_kernel(x_ref, w_ref, tab_ref, qt_ref, k_ref):
    scale = LOG2_E * HEAD_DIM_B ** -0.5
    x = x_ref[...]
    w = 2 * HEAD_DIM_B
    assert w == MXU_COLS and HEAD_DIM_B == LANES
    for c in range(2 * N_HEADS_B):
        acc = _dot(x, w_ref[:, c * w:(c + 1) * w])
        for g in range(2):
            xg = acc[:, g * LANES:(g + 1) * LANES]
            xg = xg * tab_ref[0] + pltpu.roll(xg, HEAD_DIM_B // 2, axis=1) * tab_ref[1]
            if c < N_HEADS_B:
                qt_ref[c, g] = (xg * scale).astype(qt_ref.dtype).T
            else:
                lo = (c - N_HEADS_B) * w + g * LANES
                k_ref[:, lo:lo + LANES] = xg.astype(k_ref.dtype)


def _w_in_cols(layer, col0, width):
    return pl.BlockSpec((pl.Element(D_MODEL), pl.Element(width)),
                        lambda i, j: (layer * D_MODEL, pl.multiple_of(col0 + j * width, LANES)))


def _proj_qk_b(xb, w_in_b, layer, tab_b, batch, seq, tm=1024, cast=()):
    t = xb.shape[0]
    tiles_per_seq = seq // tm
    w_spec = pl.BlockSpec((pl.Element(D_MODEL), pl.Element(2 * W_QB)), lambda i: (layer * D_MODEL, W_A),
                          pipeline_mode=pl.Buffered(1))
    return _call(
        _proj_rope_b_kernel,
        [xb, w_in_b, tab_b],
        [
            pl.BlockSpec((tm, D_MODEL), lambda i: (i, 0)),
            w_spec,
            pl.BlockSpec((2, tm, LANES), lambda i: (0, i % tiles_per_seq, 0)),
        ],
        [
            pl.BlockSpec((None, N_HEADS_B, 2, HEAD_DIM_B, tm),
                         lambda i: (i // tiles_per_seq, 0, 0, 0, i % tiles_per_seq)),
            pl.BlockSpec((tm, W_QB), lambda i: (i, 0)),
        ],
        [
            jax.ShapeDtypeStruct((batch, N_HEADS_B, 2, HEAD_DIM_B, seq), BF16),
            jax.ShapeDtypeStruct((t, W_QB), BF16),
        ],
        grid=(t // tm,), sem=("parallel",), name="proj_qk_b", cast=cast)


def _attn_a_block(sinks_ref, q_ref, kvp_ref, kvc_ref, o_ref, has_prev):
    blk = WINDOW
    hd = HEAD_DIM_A
    key = lax.broadcasted_iota(jnp.int32, (2 * blk, blk), 0)
    qry = lax.broadcasted_iota(jnp.int32, (2 * blk, blk), 1)
    mask = (key <= blk + qry) & (key > qry)
    if has_prev is not True:
        mask = mask & ((key >= blk) | has_prev)
    vpt = kvp_ref[:, W_KA:].T
    vct = kvc_ref[:, W_KA:].T

    for g in range(N_KV_A):
        ks = slice(g * hd, (g + 1) * hd)
        kband = jnp.concatenate([kvp_ref[:, ks], kvc_ref[:, ks]], axis=0)
        vt = jnp.concatenate([vpt[ks], vct[ks]], axis=1)
        qts = []
        for pr in range(GROUP_A // 2):
            h0 = g * GROUP_A + 2 * pr
            t = q_ref[:, h0 * hd:(h0 + 2) * hd].T
            qts += [t[:hd], t[hd:]]
        qt = jnp.concatenate(qts, axis=1)
        s = _dot(kband, qt)
        ps, rs = [], []
        for hh in range(GROUP_A):
            sink = sinks_ref[g * GROUP_A + hh] * LOG2_E
            sh = jnp.where(mask, s[:, hh * blk:(hh + 1) * blk], NEG_INF)
            m = jnp.maximum(jnp.max(sh, axis=0, keepdims=True), sink)
            p = jnp.exp2(sh - m)
            denom = jnp.sum(p, axis=0, keepdims=True) + jnp.exp2(sink - m)
            ps.append(p.astype(BF16))
            rs.append(1.0 / denom)
        ot = _dot(vt, jnp.concatenate(ps, axis=1))
        for pr in range(GROUP_A // 2):
            h0 = g * GROUP_A + 2 * pr
            pair = jnp.concatenate([ot[:, (2 * pr + e) * blk:(2 * pr + e + 1) * blk] * rs[2 * pr + e]
                                    for e in range(2)], axis=0)
            o_ref[:, h0 * hd:(h0 + 2) * hd] = pair.astype(o_ref.dtype).T


def _attn_a_blocks(sinks_ref, q_ref, kvp_ref, kvc_ref, o_ref, first, blocks_per_seq, c):
    blk = WINDOW
    rows = pl.ds(c * blk, blk)
    prev = kvp_ref if c == 0 else kvc_ref.at[pl.ds((c - 1) * blk, blk)]
    has_prev = (first % blocks_per_seq != 0) if c == 0 else True
    _attn_a_block(sinks_ref, q_ref.at[rows], prev, kvc_ref.at[rows], o_ref.at[rows], has_prev)


def _attn_a_specs(first_block, n_blocks, step):
    blk = WINDOW
    kv_col = W_QA // (2 * W_KA)
    base = first_block // n_blocks
    assert first_block % n_blocks == 0
    tile = lambda i, j: base + step(i, j)
    return [
        pl.BlockSpec((n_blocks * blk, W_QA), lambda i, j: (tile(i, j), 0)),
        pl.BlockSpec((blk, 2 * W_KA), lambda i, j: (jnp.maximum(tile(i, j) * n_blocks - 1, 0), kv_col)),
        pl.BlockSpec((n_blocks * blk, 2 * W_KA), lambda i, j: (tile(i, j), kv_col)),
    ], pl.BlockSpec((n_blocks * blk, W_QA), lambda i, j: (tile(i, j), 0))


def _vt_attn_a_kernel(sinks_ref, x_ref, w_ref, q_ref, kvp_ref, kvc_ref, vt_ref, o_ref,
                      *, tk, blocks_per_seq):
    n_blocks = o_ref.shape[0] // WINDOW
    first = (pl.program_id(0) * pl.num_programs(1) + pl.program_id(1)) * n_blocks
    x = x_ref[...]
    w = 2 * HEAD_DIM_B
    for c in range(max(vt_ref.shape[0], n_blocks)):
        if c < vt_ref.shape[0]:
            acc = _dot(x, w_ref[:, c * w:(c + 1) * w]).astype(vt_ref.dtype)
            for kk in range(vt_ref.shape[1]):
                vt_ref[c, kk] = acc[kk * tk:(kk + 1) * tk].T
        if c < n_blocks:
            _attn_a_blocks(sinks_ref, q_ref, kvp_ref, kvc_ref, o_ref, first, blocks_per_seq, c)


def _vt_attn_a(xb, w_in_b, layer, qkv_a, sinks, batch, seq, tk, n_blocks, tm=1024, heads=4, cast=()):
    t = xb.shape[0]
    w = 2 * HEAD_DIM_B
    tiles_per_seq = seq // tm
    nj = N_HEADS_B // heads
    assert (seq // WINDOW) % n_blocks == 0
    attn_in, attn_out = _attn_a_specs(0, n_blocks, lambda i, j: i * nj + j)
    return _call(
        functools.partial(_vt_attn_a_kernel, tk=tk, blocks_per_seq=seq // WINDOW),
        [sinks, xb, w_in_b, qkv_a, qkv_a, qkv_a],
        [
            pl.BlockSpec(memory_space=pltpu.SMEM),
            pl.BlockSpec((tm, D_MODEL), lambda i, j: (i, 0)),
            _w_in_cols(layer, W_A + 2 * W_QB, heads * w),
            *attn_in,
        ],
        [
            pl.BlockSpec((None, heads, tm // tk, w, tk),
                         lambda i, j: (i // tiles_per_seq, j, i % tiles_per_seq, 0, 0)),
            attn_out,
        ],
        [
            jax.ShapeDtypeStruct((batch, N_HEADS_B, seq // tk, w, tk), BF16),
            jax.ShapeDtypeStruct((t, W_QA), BF16),
        ],
        grid=(t // tm, nj), sem=("parallel", "arbitrary"), name="vt_attn_a", cast=cast)


def _gates_attn_a_kernel(sinks_ref, x_ref, w_ref, q_ref, kvp_ref, kvc_ref, oa_hbm_ref, g_ref, o_ref,
                         *, tn, first_block, blocks_per_seq):
    del oa_hbm_ref
    n_blocks = o_ref.shape[0] // WINDOW
    first = first_block + (pl.program_id(0) * pl.num_programs(1) + pl.program_id(1)) * n_blocks
    x = x_ref[...]
    n_chunks = tn // MXU_COLS
    for c in range(n_chunks):
        cs = slice(c * MXU_COLS, (c + 1) * MXU_COLS)
        g_ref[:, cs] = jax.nn.sigmoid(_dot(x, w_ref[:, cs])).astype(g_ref.dtype)
        if c % (n_chunks // n_blocks) == 0:
            _attn_a_blocks(sinks_ref, q_ref, kvp_ref, kvc_ref, o_ref, first, blocks_per_seq,
                           c // (n_chunks // n_blocks))


def _gates_attn_a(xb, w_in_b, layer, qkv_a, sinks, o_a, first_block, seq, n_blocks, tm=1024, tn=2048, cast=()):
    t = xb.shape[0]
    n = 2 * D_MODEL
    nj = n // tn
    assert (t // tm) * nj * n_blocks + first_block == t // WINDOW and (seq // WINDOW) % n_blocks == 0
    attn_in, attn_out = _attn_a_specs(first_block, n_blocks, lambda i, j: i * nj + j)
    return _call(
        functools.partial(_gates_attn_a_kernel, tn=tn, first_block=first_block, blocks_per_seq=seq // WINDOW),
        [sinks, xb, w_in_b, qkv_a, qkv_a, qkv_a, o_a],
        [
            pl.BlockSpec(memory_space=pltpu.SMEM),
            pl.BlockSpec((tm, D_MODEL), lambda i, j: (i, 0)),
            _w_in_cols(layer, W_A + 3 * W_QB, tn),
            *attn_in,
            pl.BlockSpec(memory_space=pl.ANY),
        ],
        [pl.BlockSpec((tm, tn), lambda i, j: (i, j)), attn_out],
        [jax.ShapeDtypeStruct((t, n), BF16), jax.ShapeDtypeStruct(o_a.shape, o_a.dtype)],
        grid=(t // tm, nj), sem=("parallel", "arbitrary"), name="gates_attn_a", cast=cast, aliases={6: 1})


def _attn_b_kernel(lq1_ref, lk1_ref, lq2_ref, lk2_ref, g_ref, qt_ref, k_ref, vt_ref, o_ref,
                   m_ref, l_ref, acc_ref, *, tq, tk, n_q_tiles, lam_init):
    qi = pl.program_id(2)
    hd = HEAD_DIM_B
    cg = MXU_COLS
    n_groups = tq // cg

    def chain(c, j, k, vt, first_key, start):
        js = slice(j * cg, (j + 1) * cg)
        s = _dot(k[:, c * hd:(c + 1) * hd], qt_ref[c, :, js])
        if first_key is not None:
            key = lax.broadcasted_iota(jnp.int32, s.shape, 0)
            qry = lax.broadcasted_iota(jnp.int32, s.shape, 1)
            s = jnp.where(key + first_key <= qry, s, NEG_INF)
        m_new = jnp.max(s, axis=0, keepdims=True)
        if not start:
            m_old = m_ref[c, :, js]
            m_new = jnp.maximum(m_old, m_new)
            a = jnp.exp2(m_old - m_new)
        p = jnp.exp2(s - m_new)
        l_new = jnp.sum(p, axis=0, keepdims=True)
        acc_new = _dot(vt, p.astype(BF16))
        if not start:
            l_new = a * l_ref[c, :, js] + l_new
            acc_new = a * acc_ref[c, :, js] + acc_new
        l_ref[c, :, js] = l_new
        acc_ref[c, :, js] = acc_new
        m_ref[c, :, js] = m_new

    blocks_per_tile = tq // tk

    def finish(j):
        js = slice(j * cg, (j + 1) * cg)
        o = acc_ref[0, :, js] * (1.0 / l_ref[0, :, js]) - lam * (acc_ref[1, :, js] * (1.0 / l_ref[1, :, js]))
        o = o * lax.rsqrt(jnp.mean(o * o, axis=0, keepdims=True) + SUBLN_EPS)
        o = o * g_ref[...] * (1.0 - lam_init)
        o_ref[js, :] = o.T.astype(o_ref.dtype)

    def tile(q_tile):
        started = set()

        def run(c, j, k, vt, first_key):
            chain(c, j, k, vt, first_key, (c, j) not in started)
            started.add((c, j))

        for kb in range(q_tile * blocks_per_tile):
            k = k_ref[kb * tk:(kb + 1) * tk, :]
            vt = vt_ref[kb]
            for j in range(n_groups):
                for c in range(2):
                    run(c, j, k, vt, None)
        for u in range(blocks_per_tile):
            kb = q_tile * blocks_per_tile + u
            k = k_ref[kb * tk:(kb + 1) * tk, :]
            vt = vt_ref[kb]
            for j in range(n_groups):
                first_key = u * tk - j * cg
                n_vis = min(tk, (j + 1) * cg - u * tk)
                if n_vis <= 0:
                    continue
                masked = first_key + n_vis - 1 > 0
                for c in range(2):
                    run(c, j, k[:n_vis], vt[:, :n_vis], first_key if masked else None)
        for j in range(n_groups):
            finish(j)

    lam = (jnp.exp(jnp.sum(lq1_ref[...] * lk1_ref[...], axis=-1, keepdims=True))
           - jnp.exp(jnp.sum(lq2_ref[...] * lk2_ref[...], axis=-1, keepdims=True))
           + lam_init)
    for q_tile in range(n_q_tiles):
        pl.when(qi == q_tile)(functools.partial(tile, q_tile))


def _attn_b(qt_b, k_b, vt, lq1, lk1, lq2, lk2, subln_g, lam_init, batch, seq, tq=1024, tk=512):
    nq = seq // tq
    w = 2 * HEAD_DIM_B
    vec = lambda n: pl.BlockSpec((1, n), lambda b, h, i: (0, 0))
    return pl.pallas_call(
        functools.partial(_attn_b_kernel, tq=tq, tk=tk, n_q_tiles=nq, lam_init=lam_init),
        grid=(batch, N_HEADS_B, nq),
        in_specs=[
            vec(HEAD_DIM_B), vec(HEAD_DIM_B), vec(HEAD_DIM_B), vec(HEAD_DIM_B),
            pl.BlockSpec((w, 1), lambda b, h, i: (0, 0)),
            pl.BlockSpec((None, None, 2, HEAD_DIM_B, tq), lambda b, h, i: (b, h, 0, 0, i)),
            pl.BlockSpec((seq, w), lambda b, h, i: (b, h)),
            pl.BlockSpec((None, None, seq // tk, w, tk), lambda b, h, i: (b, h, 0, 0, 0)),
        ],
        out_specs=pl.BlockSpec((tq, w), lambda b, h, i: (b * nq + i, h)),
        out_shape=jax.ShapeDtypeStruct((batch * seq, W_QB), BF16),
        scratch_shapes=[
            pltpu.VMEM((2, 1, tq), F32),
            pltpu.VMEM((2, 1, tq), F32),
            pltpu.VMEM((2, w, tq), F32),
        ],
        compiler_params=_params("parallel", "parallel", "arbitrary"),
        name="attn_b",
    )(lq1, lk1, lq2, lk2, subln_g, qt_b, k_b, vt)


def _merge_kernel(oa_ref, ob_ref, wa_ref, wb_ref, ga_ref, gb_ref, o_ref):
    pa = _dot(oa_ref[...], wa_ref[...])
    pb = _dot(ob_ref[...], wb_ref[...])
    o_ref[...] = (ga_ref[...].astype(F32) * pa + gb_ref[...].astype(F32) * pb).astype(o_ref.dtype)


def _merge(o_a, o_b, w_pa, w_pb, layer, gates, tm=1024, tn=1024, cast=()):
    t = o_a.shape[0]
    return _call(
        _merge_kernel,
        [o_a, o_b, w_pa, w_pb, gates, gates],
        [
            pl.BlockSpec((tm, W_QA), lambda i, j: (i, 0)),
            pl.BlockSpec((tm, W_QB), lambda i, j: (i, 0)),
            pl.BlockSpec((W_QA, tn), lambda i, j: (layer, j)),
            pl.BlockSpec((W_QB, tn), lambda i, j: (layer, j)),
            pl.BlockSpec((tm, tn), lambda i, j: (i, j)),
            pl.BlockSpec((tm, tn), lambda i, j: (i, D_MODEL // tn + j)),
        ],
        [pl.BlockSpec((tm, tn), lambda i, j: (i, j))],
        [jax.ShapeDtypeStruct((t, D_MODEL), BF16)],
        grid=(t // tm, D_MODEL // tn), sem=("parallel", "arbitrary"), name="merge", cast=cast)


def _out_ln_kernel(m_ref, w_ref, x_ref, g_ref, b_ref, y_ref, yb_ref, *, rows):
    g, b = g_ref[...], b_ref[...]
    for r in range(m_ref.shape[0] // rows):
        rs = slice(r * rows, (r + 1) * rows)
        z = ALPHA * x_ref[rs, :] + _dot(m_ref[rs, :], w_ref[...])
        y = _layer_norm(z, g, b)
        y_ref[rs, :] = y
        yb_ref[rs, :] = y.astype(yb_ref.dtype)


def _out_ln(merged, w_out, layer, x, g, b, tm=512, rows=256, cast=()):
    t = x.shape[0]
    row = lambda: pl.BlockSpec((tm, D_MODEL), lambda i: (i, 0))
    vec = lambda: pl.BlockSpec((1, D_MODEL), lambda i: (0, 0))
    return _call(
        functools.partial(_out_ln_kernel, rows=rows),
        [merged, w_out, x, g, b],
        [row(), pl.BlockSpec((D_MODEL, D_MODEL), lambda i: (layer, 0), pipeline_mode=pl.Buffered(1)),
         row(), vec(), vec()],
        [row(), row()],
        [jax.ShapeDtypeStruct((t, D_MODEL), F32), jax.ShapeDtypeStruct((t, D_MODEL), BF16)],
        grid=(t // tm,), sem=("parallel",), name="out_ln", cast=cast)


def _ffn_up_kernel(xp_ref, x_ref, wg_ref, wv_ref, cwg_ref, cwv_ref, cbg_ref, cbv_ref, o_ref,
                   *, tm, tn, tiles_per_seq):
    i = pl.program_id(1)
    halo = BF16_ROWS
    xp = jnp.where(i % tiles_per_seq == 0, jnp.zeros((), BF16), xp_ref[...])
    xe = jnp.concatenate([xp, x_ref[...]], axis=0)

    def conv(w_ref, cw_ref, cb_ref, cs):
        u = _dot(xe, w_ref[:, cs])
        out = cb_ref[:, cs] + cw_ref[CONV_WIDTH - 1:CONV_WIDTH, cs] * u[halo:]
        for back in range(1, CONV_WIDTH):
            tap = CONV_WIDTH - 1 - back
            out = out + cw_ref[tap:tap + 1, cs] * pltpu.roll(u, back, axis=0)[halo:]
        return out

    for c in range(tn // MXU_COLS):
        cs = slice(c * MXU_COLS, (c + 1) * MXU_COLS)
        gate = conv(wg_ref, cwg_ref, cbg_ref, cs)
        val = conv(wv_ref, cwv_ref, cbv_ref, cs)
        o_ref[:, cs] = (jax.nn.silu(gate) * val).astype(o_ref.dtype)


def _ffn_up(xb, w_up, layer, conv_w, conv_b, conv_layer, seq, tm=1024, tn=D_FF // 2):
    t = xb.shape[0]
    nf = D_FF // tn
    halo = BF16_ROWS
    resident = pl.Buffered(1)
    return pl.pallas_call(
        functools.partial(_ffn_up_kernel, tm=tm, tn=tn, tiles_per_seq=seq // tm),
        grid=(nf, t // tm),
        in_specs=[
            pl.BlockSpec((halo, D_MODEL), lambda j, i: (jnp.maximum(i * (tm // halo) - 1, 0), 0)),
            pl.BlockSpec((tm, D_MODEL), lambda j, i: (i, 0)),
            pl.BlockSpec((D_MODEL, tn), lambda j, i: (layer, j), pipeline_mode=resident),
            pl.BlockSpec((D_MODEL, tn), lambda j, i: (layer, nf + j), pipeline_mode=resident),
            pl.BlockSpec((None, CONV_WIDTH, tn), lambda j, i: (conv_layer, 0, j)),
            pl.BlockSpec((None, CONV_WIDTH, tn), lambda j, i: (conv_layer, 0, nf + j)),
            pl.BlockSpec((None, 1, tn), lambda j, i: (conv_layer, 0, j)),
            pl.BlockSpec((None, 1, tn), lambda j, i: (conv_layer, 0, nf + j)),
        ],
        out_specs=pl.BlockSpec((tm, tn), lambda j, i: (i, j)),
        out_shape=jax.ShapeDtypeStruct((t, D_FF), BF16),
        compiler_params=_params("arbitrary", "parallel"),
        name="ffn_up",
    )(xb, xb, w_up, w_up, conv_w, conv_w, conv_b, conv_b)


def _down_ln_kernel(a_ref, w_ref, x_ref, g_ref, b_ref, *out_refs):
    z = ALPHA * x_ref[...] + _dot(a_ref[...], w_ref[...])
    y = _layer_norm(z, g_ref[...], b_ref[...])
    out_refs[0][...] = y
    if len(out_refs) > 1:
        out_refs[1][...] = y.astype(BF16)


def _down_ln(act, w_down, layer, x, g, b, with_bf16, tm=256, cast=()):
    t = x.shape[0]
    row = lambda: pl.BlockSpec((tm, D_MODEL), lambda i: (i, 0))
    vec = lambda: pl.BlockSpec((1, D_MODEL), lambda i: (0, 0))
    out_shape = [jax.ShapeDtypeStruct((t, D_MODEL), F32)]
    if with_bf16:
        out_shape.append(jax.ShapeDtypeStruct((t, D_MODEL), BF16))
    return _call(
        _down_ln_kernel,
        [act, w_down, x, g, b],
        [
            pl.BlockSpec((tm, D_FF), lambda i: (i, 0)),
            pl.BlockSpec((D_FF, D_MODEL), lambda i: (layer, 0), pipeline_mode=pl.Buffered(1)),
            row(), vec(), vec(),
        ],
        [row() for _ in out_shape],
        out_shape,
        grid=(t // tm,), sem=("parallel",), name="down_ln", cast=cast)


def kernel(x, w_in, sinks, lambda_q1, lambda_k1, lambda_q2, lambda_k2, subln_g, w_proj_a, w_proj_b,
           w_out, ln1_g, ln1_b, w_up, conv_w, conv_b, w_down, ln2_g, ln2_b):
    batch, seq, _ = x.shape
    t = batch * seq
    tab_a, tab_b = _rope_tables(seq)
    x = x.reshape(t, D_MODEL)
    xb = None
    row = lambda v: v.reshape(1, -1).astype(F32)
    rows = lambda w: w.reshape(-1, w.shape[-1])
    whole = lambda w: (rows(w), 0, rows(w).shape[0])
    layer0 = lambda w: (rows(w), 0, w.shape[1])
    later = lambda w: (rows(w), w.shape[1], (DEPTH - 1) * w.shape[1])
    conv_w = conv_w.astype(F32)
    conv_b = conv_b.astype(F32).reshape(DEPTH, 1, 2 * D_FF)
    w_in_b, w_in_row = None, 0
    w_up_b, w_up_row = None, 0
    for l in range(DEPTH):
        first = l == 0
        if first:
            qkv_a, xb, w_in_b, w_up_b = _proj_a(x, w_in[0, :, :W_A].astype(BF16), 0, tab_a, seq,
                                                cast=[layer0(w_in), layer0(w_up)])
        else:
            qkv_a, = _proj_a(xb, w_in_b, w_in_row, tab_a, seq)
        qt_b, k_b = _proj_qk_b(xb, w_in_b, w_in_row, tab_b, batch, seq)
        half = t // WINDOW // 2
        per_step = 2
        vt_b, o_a, *done = _vt_attn_a(xb, w_in_b, w_in_row, qkv_a, sinks[l].astype(F32), batch, seq, ATTN_B_TK,
                                      per_step,
                                      cast=[whole(w_proj_a), whole(w_proj_b), whole(w_out)] if first else [])
        if first:
            w_pa, w_pb, w_o = done
        gates, o_a, *done = _gates_attn_a(xb, w_in_b, w_in_row, qkv_a, sinks[l].astype(F32), o_a, half, seq,
                                          per_step, cast=[later(w_up)] if first and DEPTH > 1 else [])
        if done:
            w_up_rest, = done
        lam_init = 0.8 - 0.6 * math.exp(-0.3 * l)
        o_b = _attn_b(qt_b, k_b, vt_b, row(lambda_q1[l]), row(lambda_k1[l]), row(lambda_q2[l]), row(lambda_k2[l]),
                      subln_g[l].reshape(-1, 1).astype(F32), lam_init, batch, seq, tk=ATTN_B_TK)
        merged, = _merge(o_a, o_b, w_pa, w_pb, l, gates)
        x, xb, *done = _out_ln(merged, w_o, l, x, row(ln1_g[l]), row(ln1_b[l]),
                               cast=[whole(w_down)] if first else [])
        if first:
            w_d, = done
        act = _ffn_up(xb, w_up_b, w_up_row, conv_w, conv_b, l, seq)
        last = l == DEPTH - 1
        x, *rest = _down_ln(act, w_d, l, x, row(ln2_g[l]), row(ln2_b[l]), with_bf16=not last,
                            cast=[later(w_in)] if first and DEPTH > 1 else [])
        if not last:
            xb = rest[0]
        if first and DEPTH > 1:
            w_in_b, w_up_b = rest[-1], w_up_rest
        if not last:
            w_in_row, w_up_row = l, l
    return x.reshape(batch, seq, D_MODEL)
```

```python
import functools
import math

import jax
import jax.numpy as jnp
from jax import lax
from jax.experimental import pallas as pl
from jax.experimental.pallas import tpu as pltpu

D_MODEL = 2048
DEPTH = 2
HEAD_DIM_A = 64
N_Q_A = 32
N_KV_A = 4
GROUP_A = N_Q_A // N_KV_A
WINDOW = 128
HEAD_DIM_B = 128
N_HEADS_B = D_MODEL // (2 * HEAD_DIM_B)
SUBLN_EPS = 1e-5
ROPE_THETA = 10000.0
D_FF = ((8 * D_MODEL // 3 + 255) // 256) * 256
CONV_WIDTH = 3
LN_EPS = 1e-5
ALPHA = (2 * DEPTH) ** 0.25
NEG_INF = -1e30
LOG2_E = math.log2(math.e)

W_QA = N_Q_A * HEAD_DIM_A
W_KA = N_KV_A * HEAD_DIM_A
W_A = W_QA + 2 * W_KA
W_QB = N_HEADS_B * 2 * HEAD_DIM_B

LANES = 128
MXU_COLS = 256
BF16_ROWS = 16
VMEM_LIMIT = 56 * 1024 * 1024
ATTN_B_TK = 512

F32 = jnp.float32
BF16 = jnp.bfloat16


def _params(*sem):
    return pltpu.CompilerParams(dimension_semantics=sem, vmem_limit_bytes=VMEM_LIMIT)


def _dot(a, b):
    return jnp.dot(a, b, preferred_element_type=F32)


def _dot_nt(a, b):
    return lax.dot_general(a, b, (((1,), (1,)), ((), ())), preferred_element_type=F32)


def _layer_norm(z, g, b):
    mu = jnp.mean(z, axis=-1, keepdims=True)
    zc = z - mu
    var = jnp.mean(zc * zc, axis=-1, keepdims=True)
    return zc * lax.rsqrt(var + LN_EPS) * g + b


def _call(kernel_fn, inputs, in_specs, out_specs, out_shape, *, grid, sem, name, cast=(), aliases=None):
    n_in, n_out, n_cast = len(inputs), len(out_specs), len(cast)
    n_steps = math.prod(grid)

    def step(*g):
        s = 0
        for g_d, n_d in zip(g, grid):
            s = s * n_d + g_d
        return s

    def slabs(offset):
        specs = []
        for w, first_row, n_rows in cast:
            slab = n_rows // n_steps
            assert n_rows % (n_steps * BF16_ROWS) == 0 and first_row % slab == 0
            first = first_row // slab if offset else 0
            specs.append(pl.BlockSpec((slab, w.shape[1]), lambda *g, first=first: (first + step(*g), 0)))
        return specs

    def body(*refs):
        ins, src = refs[:n_in], refs[n_in:n_in + n_cast]
        outs = refs[n_in + n_cast:n_in + n_cast + n_out]
        dst = refs[n_in + n_cast + n_out:]
        for s_ref, d_ref in zip(src, dst):
            d_ref[...] = s_ref[...].astype(d_ref.dtype)
        kernel_fn(*ins, *outs)

    return pl.pallas_call(
        body,
        grid=grid,
        in_specs=[*in_specs, *slabs(True)],
        out_specs=[*out_specs, *slabs(False)],
        out_shape=[*out_shape, *[jax.ShapeDtypeStruct((n_rows, w.shape[1]), BF16) for w, _, n_rows in cast]],
        input_output_aliases=aliases or {},
        compiler_params=_params(*sem),
        name=name,
    )(*inputs, *[w for w, _, _ in cast])


def _rope_tables(seq):
    def cs(dim):
        inv = 1.0 / (ROPE_THETA ** (jnp.arange(0, dim, 2, dtype=F32) / dim))
        ang = jnp.arange(seq, dtype=F32)[:, None] * inv[None, :]
        return jnp.cos(ang), jnp.sin(ang)

    cos_a, sin_a = cs(HEAD_DIM_A)
    cos_b, sin_b = cs(HEAD_DIM_B)
    zeros_a = jnp.zeros_like(sin_a)
    heads = LANES // HEAD_DIM_A
    tab_a = jnp.stack([
        jnp.tile(cos_a, (1, 2 * heads)),
        jnp.tile(jnp.concatenate([zeros_a, sin_a], axis=1), (1, heads)),
        jnp.tile(jnp.concatenate([-sin_a, zeros_a], axis=1), (1, heads)),
    ])
    assert HEAD_DIM_B == LANES
    tab_b = jnp.stack([
        jnp.concatenate([cos_b, cos_b], axis=1),
        jnp.concatenate([-sin_b, sin_b], axis=1),
    ])
    return tab_a, tab_b


def _proj_a_kernel(x_ref, w_ref, tab_ref, o_ref, *xb_refs):
    x = x_ref[...].astype(BF16)
    if xb_refs:
        xb_refs[0][...] = x
    acc = _dot(x, w_ref[...])
    cos, hi, lo = tab_ref[0], tab_ref[1], tab_ref[2]
    scale = LOG2_E * HEAD_DIM_A ** -0.5
    n_rope = (W_QA + W_KA) // LANES
    for g in range(W_A // LANES):
        xg = acc[:, g * LANES:(g + 1) * LANES]
        if g < n_rope:
            half = HEAD_DIM_A // 2
            xg = xg * cos + pltpu.roll(xg, half, axis=1) * hi + pltpu.roll(xg, LANES - half, axis=1) * lo
            if g < W_QA // LANES:
                xg = xg * scale
        o_ref[:, g * LANES:(g + 1) * LANES] = xg.astype(o_ref.dtype)


def _proj_a(x, w_in_b, layer, tab_a, seq, tm=512, cast=()):
    t = x.shape[0]
    out_specs = [pl.BlockSpec((tm, W_A), lambda i: (i, 0))]
    out_shape = [jax.ShapeDtypeStruct((t, W_A), BF16)]
    if x.dtype != BF16:
        out_specs.append(pl.BlockSpec((tm, D_MODEL), lambda i: (i, 0)))
        out_shape.append(jax.ShapeDtypeStruct((t, D_MODEL), BF16))
    return _call(
        _proj_a_kernel,
        [x, w_in_b, tab_a],
        [
            pl.BlockSpec((tm, D_MODEL), lambda i: (i, 0)),
            pl.BlockSpec((D_MODEL, W_A), lambda i: (layer, 0)),
            pl.BlockSpec((3, tm, LANES), lambda i: (0, i % (seq // tm), 0)),
        ],
        out_specs,
        out_shape,
        grid=(t // tm,), sem=("parallel",), name="proj_a", cast=cast)


def _proj_rope_b_kernel(x_ref, w_ref, tab_ref, qt_ref, k_ref):
    scale = LOG2_E * HEAD_DIM_B ** -0.5
    x = x_ref[...]
    w = 2 * HEAD_DIM_B
    assert w == MXU_COLS and HEAD_DIM_B == LANES
    for c in range(2 * N_HEADS_B):
        acc = _dot(x, w_ref[:, c * w:(c + 1) * w])
        for g in range(2):
            xg = acc[:, g * LANES:(g + 1) * LANES]
            xg = xg * tab_ref[0] + pltpu.roll(xg, HEAD_DIM_B // 2, axis=1) * tab_ref[1]
            if c < N_HEADS_B:
                qt_ref[c, g] = (xg * scale).astype(qt_ref.dtype).T
            else:
                lo = (c - N_HEADS_B) * w + g * LANES
                k_ref[:, lo:lo + LANES] = xg.astype(k_ref.dtype)


def _w_in_cols(layer, col0, width):
    return pl.BlockSpec((pl.Element(D_MODEL), pl.Element(width)),
                        lambda i, j: (layer * D_MODEL, pl.multiple_of(col0 + j * width, LANES)))


def _proj_qk_b(xb, w_in_b, layer, tab_b, batch, seq, tm=1024, cast=()):
    t = xb.shape[0]
    tiles_per_seq = seq // tm
    w_spec = pl.BlockSpec((pl.Element(D_MODEL), pl.Element(2 * W_QB)), lambda i: (layer * D_MODEL, W_A),
                          pipeline_mode=pl.Buffered(1))
    return _call(
        _proj_rope_b_kernel,
        [xb, w_in_b, tab_b],
        [
            pl.BlockSpec((tm, D_MODEL), lambda i: (i, 0)),
            w_spec,
            pl.BlockSpec((2, tm, LANES), lambda i: (0, i % tiles_per_seq, 0)),
        ],
        [
            pl.BlockSpec((None, N_HEADS_B, 2, HEAD_DIM_B, tm),
                         lambda i: (i // tiles_per_seq, 0, 0, 0, i % tiles_per_seq)),
            pl.BlockSpec((tm, W_QB), lambda i: (i, 0)),
        ],
        [
            jax.ShapeDtypeStruct((batch, N_HEADS_B, 2, HEAD_DIM_B, seq), BF16),
            jax.ShapeDtypeStruct((t, W_QB), BF16),
        ],
        grid=(t // tm,), sem=("parallel",), name="proj_qk_b", cast=cast)


def _attn_a_block(sinks_ref, q_ref, kvp_ref, kvc_ref, o_ref, has_prev):
    blk = WINDOW
    hd = HEAD_DIM_A
    key = lax.broadcasted_iota(jnp.int32, (2 * blk, blk), 0)
    qry = lax.broadcasted_iota(jnp.int32, (2 * blk, blk), 1)
    mask = (key <= blk + qry) & (key > qry)
    if has_prev is not True:
        mask = mask & ((key >= blk) | has_prev)
    vpt = kvp_ref[:, W_KA:].T
    vct = kvc_ref[:, W_KA:].T

    for g in range(N_KV_A):
        ks = slice(g * hd, (g + 1) * hd)
        kband = jnp.concatenate([kvp_ref[:, ks], kvc_ref[:, ks]], axis=0)
        vt = jnp.concatenate([vpt[ks], vct[ks]], axis=1)
        qts = []
        for pr in range(GROUP_A // 2):
            h0 = g * GROUP_A + 2 * pr
            t = q_ref[:, h0 * hd:(h0 + 2) * hd].T
            qts += [t[:hd], t[hd:]]
        qt = jnp.concatenate(qts, axis=1)
        s = _dot(kband, qt)
        ps, rs = [], []
        for hh in range(GROUP_A):
            sink = sinks_ref[g * GROUP_A + hh] * LOG2_E
            sh = jnp.where(mask, s[:, hh * blk:(hh + 1) * blk], NEG_INF)
            m = jnp.maximum(jnp.max(sh, axis=0, keepdims=True), sink)
            p = jnp.exp2(sh - m)
            denom = jnp.sum(p, axis=0, keepdims=True) + jnp.exp2(sink - m)
            ps.append(p.astype(BF16))
            rs.append(1.0 / denom)
        ot = _dot(vt, jnp.concatenate(ps, axis=1))
        for pr in range(GROUP_A // 2):
            h0 = g * GROUP_A + 2 * pr
            pair = jnp.concatenate([ot[:, (2 * pr + e) * blk:(2 * pr + e + 1) * blk] * rs[2 * pr + e]
                                    for e in range(2)], axis=0)
            o_ref[:, h0 * hd:(h0 + 2) * hd] = pair.astype(o_ref.dtype).T


def _attn_a_blocks(sinks_ref, q_ref, kvp_ref, kvc_ref, o_ref, first, blocks_per_seq, c):
    blk = WINDOW
    rows = pl.ds(c * blk, blk)
    prev = kvp_ref if c == 0 else kvc_ref.at[pl.ds((c - 1) * blk, blk)]
    has_prev = (first % blocks_per_seq != 0) if c == 0 else True
    _attn_a_block(sinks_ref, q_ref.at[rows], prev, kvc_ref.at[rows], o_ref.at[rows], has_prev)


def _attn_a_specs(first_block, n_blocks, step):
    blk = WINDOW
    kv_col = W_QA // (2 * W_KA)
    base = first_block // n_blocks
    assert first_block % n_blocks == 0
    tile = lambda i, j: base + step(i, j)
    return [
        pl.BlockSpec((n_blocks * blk, W_QA), lambda i, j: (tile(i, j), 0)),
        pl.BlockSpec((blk, 2 * W_KA), lambda i, j: (jnp.maximum(tile(i, j) * n_blocks - 1, 0), kv_col)),
        pl.BlockSpec((n_blocks * blk, 2 * W_KA), lambda i, j: (tile(i, j), kv_col)),
    ], pl.BlockSpec((n_blocks * blk, W_QA), lambda i, j: (tile(i, j), 0))


def _vt_attn_a_kernel(sinks_ref, x_ref, w_ref, q_ref, kvp_ref, kvc_ref, oa_hbm_ref, vt_ref, o_ref,
                      *, tk, blocks_per_seq):
    del oa_hbm_ref
    n_blocks = o_ref.shape[0] // WINDOW
    first = (pl.program_id(0) * pl.num_programs(1) + pl.program_id(1)) * n_blocks
    x = x_ref[...]
    w = 2 * HEAD_DIM_B
    for c in range(max(vt_ref.shape[0], n_blocks)):
        if c < vt_ref.shape[0]:
            acc = _dot(x, w_ref[:, c * w:(c + 1) * w]).astype(vt_ref.dtype)
            for kk in range(vt_ref.shape[1]):
                vt_ref[c, kk] = acc[kk * tk:(kk + 1) * tk].T
        if c < n_blocks:
            _attn_a_blocks(sinks_ref, q_ref, kvp_ref, kvc_ref, o_ref, first, blocks_per_seq, c)


def _vt_attn_a(xb, w_in_b, layer, qkv_a, sinks, o_a, batch, seq, tk, n_blocks, tm=1024, heads=4, cast=()):
    t = xb.shape[0]
    w = 2 * HEAD_DIM_B
    tiles_per_seq = seq // tm
    nj = N_HEADS_B // heads
    assert (seq // WINDOW) % n_blocks == 0
    attn_in, attn_out = _attn_a_specs(0, n_blocks, lambda i, j: i * nj + j)
    return _call(
        functools.partial(_vt_attn_a_kernel, tk=tk, blocks_per_seq=seq // WINDOW),
        [sinks, xb, w_in_b, qkv_a, qkv_a, qkv_a, o_a],
        [
            pl.BlockSpec(memory_space=pltpu.SMEM),
            pl.BlockSpec((tm, D_MODEL), lambda i, j: (i, 0)),
            _w_in_cols(layer, W_A + 2 * W_QB, heads * w),
            *attn_in,
            pl.BlockSpec(memory_space=pl.ANY),
        ],
        [
            pl.BlockSpec((None, heads, tm // tk, w, tk),
                         lambda i, j: (i // tiles_per_seq, j, i % tiles_per_seq, 0, 0)),
            attn_out,
        ],
        [
            jax.ShapeDtypeStruct((batch, N_HEADS_B, seq // tk, w, tk), BF16),
            jax.ShapeDtypeStruct(o_a.shape, o_a.dtype),
        ],
        grid=(t // tm, nj), sem=("parallel", "arbitrary"), name="vt_attn_a", cast=cast, aliases={6: 1})


def _gates_attn_a_kernel(sinks_ref, x_ref, w_ref, q_ref, kvp_ref, kvc_ref, oa_hbm_ref, g_ref, o_ref,
                         *, tn, first_block, blocks_per_seq):
    del oa_hbm_ref
    n_blocks = o_ref.shape[0] // WINDOW
    first = first_block + (pl.program_id(0) * pl.num_programs(1) + pl.program_id(1)) * n_blocks
    x = x_ref[...]
    n_chunks = tn // MXU_COLS
    for c in range(n_chunks):
        cs = slice(c * MXU_COLS, (c + 1) * MXU_COLS)
        g_ref[:, cs] = jax.nn.sigmoid(_dot(x, w_ref[:, cs])).astype(g_ref.dtype)
        if c % (n_chunks // n_blocks) == 0:
            _attn_a_blocks(sinks_ref, q_ref, kvp_ref, kvc_ref, o_ref, first, blocks_per_seq,
                           c // (n_chunks // n_blocks))


def _gates_attn_a(xb, w_in_b, layer, qkv_a, sinks, o_a, first_block, seq, n_blocks, tm=1024, tn=2048, cast=()):
    t = xb.shape[0]
    n = 2 * D_MODEL
    nj = n // tn
    assert (t // tm) * nj * n_blocks + first_block == t // WINDOW and (seq // WINDOW) % n_blocks == 0
    attn_in, attn_out = _attn_a_specs(first_block, n_blocks, lambda i, j: i * nj + j)
    return _call(
        functools.partial(_gates_attn_a_kernel, tn=tn, first_block=first_block, blocks_per_seq=seq // WINDOW),
        [sinks, xb, w_in_b, qkv_a, qkv_a, qkv_a, o_a],
        [
            pl.BlockSpec(memory_space=pltpu.SMEM),
            pl.BlockSpec((tm, D_MODEL), lambda i, j: (i, 0)),
            _w_in_cols(layer, W_A + 3 * W_QB, tn),
            *attn_in,
            pl.BlockSpec(memory_space=pl.ANY),
        ],
        [pl.BlockSpec((tm, tn), lambda i, j: (i, j)), attn_out],
        [jax.ShapeDtypeStruct((t, n), BF16), jax.ShapeDtypeStruct(o_a.shape, o_a.dtype)],
        grid=(t // tm, nj), sem=("parallel", "arbitrary"), name="gates_attn_a", cast=cast, aliases={6: 1})


def _attn_b_kernel(lq1_ref, lk1_ref, lq2_ref, lk2_ref, g_ref, qt_ref, k_ref, vt_ref, o_ref,
                   m_ref, l_ref, acc_ref, *, tq, tk, n_q_tiles, lam_init):
    qi = pl.program_id(2)
    hd = HEAD_DIM_B
    cg = MXU_COLS
    n_groups = tq // cg

    def chain(c, j, k, vt, first_key, start):
        js = slice(j * cg, (j + 1) * cg)
        s = _dot(k[:, c * hd:(c + 1) * hd], qt_ref[c, :, js])
        if first_key is not None:
            key = lax.broadcasted_iota(jnp.int32, s.shape, 0)
            qry = lax.broadcasted_iota(jnp.int32, s.shape, 1)
            s = jnp.where(key + first_key <= qry, s, NEG_INF)
        m_new = jnp.max(s, axis=0, keepdims=True)
        if not start:
            m_old = m_ref[c, :, js]
            m_new = jnp.maximum(m_old, m_new)
            a = jnp.exp2(m_old - m_new)
        p = jnp.exp2(s - m_new)
        l_new = jnp.sum(p, axis=0, keepdims=True)
        acc_new = _dot(vt, p.astype(BF16))
        if not start:
            l_new = a * l_ref[c, :, js] + l_new
            acc_new = a * acc_ref[c, :, js] + acc_new
        l_ref[c, :, js] = l_new
        acc_ref[c, :, js] = acc_new
        m_ref[c, :, js] = m_new

    blocks_per_tile = tq // tk

    def finish(j):
        js = slice(j * cg, (j + 1) * cg)
        o = acc_ref[0, :, js] * (1.0 / l_ref[0, :, js]) - lam * (acc_ref[1, :, js] * (1.0 / l_ref[1, :, js]))
        o = o * lax.rsqrt(jnp.mean(o * o, axis=0, keepdims=True) + SUBLN_EPS)
        o = o * g_ref[...] * (1.0 - lam_init)
        o_ref[js, :] = o.T.astype(o_ref.dtype)

    def tile(q_tile):
        started = set()

        def run(c, j, k, vt, first_key):
            chain(c, j, k, vt, first_key, (c, j) not in started)
            started.add((c, j))

        for kb in range(q_tile * blocks_per_tile):
            k = k_ref[kb * tk:(kb + 1) * tk, :]
            vt = vt_ref[kb]
            for j in range(n_groups):
                for c in range(2):
                    run(c, j, k, vt, None)
        for u in range(blocks_per_tile):
            kb = q_tile * blocks_per_tile + u
            k = k_ref[kb * tk:(kb + 1) * tk, :]
            vt = vt_ref[kb]
            for j in range(n_groups):
                first_key = u * tk - j * cg
                n_vis = min(tk, (j + 1) * cg - u * tk)
                if n_vis <= 0:
                    continue
                masked = first_key + n_vis - 1 > 0
                for c in range(2):
                    run(c, j, k[:n_vis], vt[:, :n_vis], first_key if masked else None)
        for j in range(n_groups):
            finish(j)

    lam = (jnp.exp(jnp.sum(lq1_ref[...] * lk1_ref[...], axis=-1, keepdims=True))
           - jnp.exp(jnp.sum(lq2_ref[...] * lk2_ref[...], axis=-1, keepdims=True))
           + lam_init)
    for q_tile in range(n_q_tiles):
        pl.when(qi == q_tile)(functools.partial(tile, q_tile))


def _attn_b(qt_b, k_b, vt, lq1, lk1, lq2, lk2, subln_g, lam_init, batch, seq, tq=1024, tk=512):
    nq = seq // tq
    w = 2 * HEAD_DIM_B
    vec = lambda n: pl.BlockSpec((1, n), lambda b, h, i: (0, 0))
    return pl.pallas_call(
        functools.partial(_attn_b_kernel, tq=tq, tk=tk, n_q_tiles=nq, lam_init=lam_init),
        grid=(batch, N_HEADS_B, nq),
        in_specs=[
            vec(HEAD_DIM_B), vec(HEAD_DIM_B), vec(HEAD_DIM_B), vec(HEAD_DIM_B),
            pl.BlockSpec((w, 1), lambda b, h, i: (0, 0)),
            pl.BlockSpec((None, None, 2, HEAD_DIM_B, tq), lambda b, h, i: (b, h, 0, 0, i)),
            pl.BlockSpec((seq, w), lambda b, h, i: (b, h)),
            pl.BlockSpec((None, None, seq // tk, w, tk), lambda b, h, i: (b, h, 0, 0, 0)),
        ],
        out_specs=pl.BlockSpec((tq, w), lambda b, h, i: (b * nq + i, h)),
        out_shape=jax.ShapeDtypeStruct((batch * seq, W_QB), BF16),
        scratch_shapes=[
            pltpu.VMEM((2, 1, tq), F32),
            pltpu.VMEM((2, 1, tq), F32),
            pltpu.VMEM((2, w, tq), F32),
        ],
        compiler_params=_params("parallel", "parallel", "arbitrary"),
        name="attn_b",
    )(lq1, lk1, lq2, lk2, subln_g, qt_b, k_b, vt)


def _merge_kernel(oa_ref, ob_ref, wa_ref, wb_ref, ga_ref, gb_ref, o_ref):
    pa = _dot(oa_ref[...], wa_ref[...])
    pb = _dot(ob_ref[...], wb_ref[...])
    o_ref[...] = (ga_ref[...].astype(F32) * pa + gb_ref[...].astype(F32) * pb).astype(o_ref.dtype)


def _merge(o_a, o_b, w_pa, w_pb, layer, gates, tm=1024, tn=1024, cast=()):
    t = o_a.shape[0]
    return _call(
        _merge_kernel,
        [o_a, o_b, w_pa, w_pb, gates, gates],
        [
            pl.BlockSpec((tm, W_QA), lambda i, j: (i, 0)),
            pl.BlockSpec((tm, W_QB), lambda i, j: (i, 0)),
            pl.BlockSpec((W_QA, tn), lambda i, j: (layer, j)),
            pl.BlockSpec((W_QB, tn), lambda i, j: (layer, j)),
            pl.BlockSpec((tm, tn), lambda i, j: (i, j)),
            pl.BlockSpec((tm, tn), lambda i, j: (i, D_MODEL // tn + j)),
        ],
        [pl.BlockSpec((tm, tn), lambda i, j: (i, j))],
        [jax.ShapeDtypeStruct((t, D_MODEL), BF16)],
        grid=(t // tm, D_MODEL // tn), sem=("parallel", "arbitrary"), name="merge", cast=cast)


def _out_ln_kernel(m_ref, w_ref, x_ref, g_ref, b_ref, y_ref, yb_ref, *, rows):
    g, b = g_ref[...], b_ref[...]
    for r in range(m_ref.shape[0] // rows):
        rs = slice(r * rows, (r + 1) * rows)
        z = ALPHA * x_ref[rs, :] + _dot(m_ref[rs, :], w_ref[...])
        y = _layer_norm(z, g, b)
        y_ref[rs, :] = y
        yb_ref[rs, :] = y.astype(yb_ref.dtype)


def _out_ln(merged, w_out, layer, x, g, b, tm=512, rows=256, cast=()):
    t = x.shape[0]
    row = lambda: pl.BlockSpec((tm, D_MODEL), lambda i: (i, 0))
    vec = lambda: pl.BlockSpec((1, D_MODEL), lambda i: (0, 0))
    return _call(
        functools.partial(_out_ln_kernel, rows=rows),
        [merged, w_out, x, g, b],
        [row(), pl.BlockSpec((D_MODEL, D_MODEL), lambda i: (layer, 0), pipeline_mode=pl.Buffered(1)),
         row(), vec(), vec()],
        [row(), row()],
        [jax.ShapeDtypeStruct((t, D_MODEL), F32), jax.ShapeDtypeStruct((t, D_MODEL), BF16)],
        grid=(t // tm,), sem=("parallel",), name="out_ln", cast=cast)


def _ffn_up_kernel(xp_ref, x_ref, wg_ref, wv_ref, cwg_ref, cwv_ref, cbg_ref, cbv_ref, o_ref,
                   *, tm, tn, tiles_per_seq):
    i = pl.program_id(1)
    halo = BF16_ROWS
    xp = jnp.where(i % tiles_per_seq == 0, jnp.zeros((), BF16), xp_ref[...])
    xe = jnp.concatenate([xp, x_ref[...]], axis=0)

    def conv(w_ref, cw_ref, cb_ref, cs):
        u = _dot(xe, w_ref[:, cs])
        out = cb_ref[:, cs] + cw_ref[CONV_WIDTH - 1:CONV_WIDTH, cs] * u[halo:]
        for back in range(1, CONV_WIDTH):
            tap = CONV_WIDTH - 1 - back
            out = out + cw_ref[tap:tap + 1, cs] * pltpu.roll(u, back, axis=0)[halo:]
        return out

    for c in range(tn // MXU_COLS):
        cs = slice(c * MXU_COLS, (c + 1) * MXU_COLS)
        gate = conv(wg_ref, cwg_ref, cbg_ref, cs)
        val = conv(wv_ref, cwv_ref, cbv_ref, cs)
        o_ref[:, cs] = (jax.nn.silu(gate) * val).astype(o_ref.dtype)


def _ffn_up(xb, w_up, layer, conv_w, conv_b, conv_layer, seq, tm=1024, tn=D_FF // 2):
    t = xb.shape[0]
    nf = D_FF // tn
    halo = BF16_ROWS
    resident = pl.Buffered(1)
    return pl.pallas_call(
        functools.partial(_ffn_up_kernel, tm=tm, tn=tn, tiles_per_seq=seq // tm),
        grid=(nf, t // tm),
        in_specs=[
            pl.BlockSpec((halo, D_MODEL), lambda j, i: (jnp.maximum(i * (tm // halo) - 1, 0), 0)),
            pl.BlockSpec((tm, D_MODEL), lambda j, i: (i, 0)),
            pl.BlockSpec((D_MODEL, tn), lambda j, i: (layer, j), pipeline_mode=resident),
            pl.BlockSpec((D_MODEL, tn), lambda j, i: (layer, nf + j), pipeline_mode=resident),
            pl.BlockSpec((None, CONV_WIDTH, tn), lambda j, i: (conv_layer, 0, j)),
            pl.BlockSpec((None, CONV_WIDTH, tn), lambda j, i: (conv_layer, 0, nf + j)),
            pl.BlockSpec((None, 1, tn), lambda j, i: (conv_layer, 0, j)),
            pl.BlockSpec((None, 1, tn), lambda j, i: (conv_layer, 0, nf + j)),
        ],
        out_specs=pl.BlockSpec((tm, tn), lambda j, i: (i, j)),
        out_shape=jax.ShapeDtypeStruct((t, D_FF), BF16),
        compiler_params=_params("arbitrary", "parallel"),
        name="ffn_up",
    )(xb, xb, w_up, w_up, conv_w, conv_w, conv_b, conv_b)


def _down_ln_kernel(a_ref, w_ref, x_ref, g_ref, b_ref, *out_refs):
    z = ALPHA * x_ref[...] + _dot(a_ref[...], w_ref[...])
    y = _layer_norm(z, g_ref[...], b_ref[...])
    out_refs[0][...] = y
    if len(out_refs) > 1:
        out_refs[1][...] = y.astype(BF16)


def _down_ln(act, w_down, layer, x, g, b, with_bf16, tm=256, cast=()):
    t = x.shape[0]
    row = lambda: pl.BlockSpec((tm, D_MODEL), lambda i: (i, 0))
    vec = lambda: pl.BlockSpec((1, D_MODEL), lambda i: (0, 0))
    out_shape = [jax.ShapeDtypeStruct((t, D_MODEL), F32)]
    if with_bf16:
        out_shape.append(jax.ShapeDtypeStruct((t, D_MODEL), BF16))
    return _call(
        _down_ln_kernel,
        [act, w_down, x, g, b],
        [
            pl.BlockSpec((tm, D_FF), lambda i: (i, 0)),
            pl.BlockSpec((D_FF, D_MODEL), lambda i: (layer, 0), pipeline_mode=pl.Buffered(1)),
            row(), vec(), vec(),
        ],
        [row() for _ in out_shape],
        out_shape,
        grid=(t // tm,), sem=("parallel",), name="down_ln", cast=cast)


def kernel(x, w_in, sinks, lambda_q1, lambda_k1, lambda_q2, lambda_k2, subln_g, w_proj_a, w_proj_b,
           w_out, ln1_g, ln1_b, w_up, conv_w, conv_b, w_down, ln2_g, ln2_b):
    batch, seq, _ = x.shape
    t = batch * seq
    tab_a, tab_b = _rope_tables(seq)
    x = x.reshape(t, D_MODEL)
    xb = None
    row = lambda v: v.reshape(1, -1).astype(F32)
    rows = lambda w: w.reshape(-1, w.shape[-1])
    whole = lambda w: (rows(w), 0, rows(w).shape[0])
    layer0 = lambda w: (rows(w), 0, w.shape[1])
    later = lambda w: (rows(w), w.shape[1], (DEPTH - 1) * w.shape[1])
    conv_w = conv_w.astype(F32)
    conv_b = conv_b.astype(F32).reshape(DEPTH, 1, 2 * D_FF)
    w_in_b, w_in_row = None, 0
    w_up_b, w_up_row = None, 0
    for l in range(DEPTH):
        first = l == 0
        if first:
            qkv_a, xb, w_in_b, w_up_b = _proj_a(x, w_in[0, :, :W_A].astype(BF16), 0, tab_a, seq,
                                                cast=[layer0(w_in), layer0(w_up)])
        else:
            qkv_a, = _proj_a(xb, w_in_b, w_in_row, tab_a, seq)
        qt_b, k_b = _proj_qk_b(xb, w_in_b, w_in_row, tab_b, batch, seq)
        half = t // WINDOW // 2
        per_step = 2
        o_a = jnp.zeros((t, W_QA), BF16)
        vt_b, o_a, *done = _vt_attn_a(xb, w_in_b, w_in_row, qkv_a, sinks[l].astype(F32), o_a, batch, seq, ATTN_B_TK,
                                      per_step,
                                      cast=[whole(w_proj_a), whole(w_proj_b), whole(w_out)] if first else [])
        if first:
            w_pa, w_pb, w_o = done
        gates, o_a, *done = _gates_attn_a(xb, w_in_b, w_in_row, qkv_a, sinks[l].astype(F32), o_a, half, seq,
                                          per_step, cast=[later(w_up)] if first and DEPTH > 1 else [])
        if done:
            w_up_rest, = done
        lam_init = 0.8 - 0.6 * math.exp(-0.3 * l)
        o_b = _attn_b(qt_b, k_b, vt_b, row(lambda_q1[l]), row(lambda_k1[l]), row(lambda_q2[l]), row(lambda_k2[l]),
                      subln_g[l].reshape(-1, 1).astype(F32), lam_init, batch, seq, tk=ATTN_B_TK)
        merged, = _merge(o_a, o_b, w_pa, w_pb, l, gates)
        x, xb, *done = _out_ln(merged, w_o, l, x, row(ln1_g[l]), row(ln1_b[l]),
                               cast=[whole(w_down)] if first else [])
        if first:
            w_d, = done
        act = _ffn_up(xb, w_up_b, w_up_row, conv_w, conv_b, l, seq)
        last = l == DEPTH - 1
        x, *rest = _down_ln(act, w_d, l, x, row(ln2_g[l]), row(ln2_b[l]), with_bf16=not last,
                            cast=[later(w_in)] if first and DEPTH > 1 else [])
        if not last:
            xb = rest[0]
        if first and DEPTH > 1:
            w_in_b, w_up_b = rest[-1], w_up_rest
        if not last:
            w_in_row, w_up_row = l, l
    return x.reshape(batch, seq, D_MODEL)
```

```python
import functools
import math

import jax
import jax.numpy as jnp
from jax import lax
from jax.experimental import pallas as pl
from jax.experimental.pallas import tpu as pltpu

D_MODEL = 2048
DEPTH = 2
HEAD_DIM_A = 64
N_Q_A = 32
N_KV_A = 4
GROUP_A = N_Q_A // N_KV_A
WINDOW = 128
HEAD_DIM_B = 128
N_HEADS_B = D_MODEL // (2 * HEAD_DIM_B)
SUBLN_EPS = 1e-5
ROPE_THETA = 10000.0
D_FF = ((8 * D_MODEL // 3 + 255) // 256) * 256
CONV_WIDTH = 3
LN_EPS = 1e-5
ALPHA = (2 * DEPTH) ** 0.25
NEG_INF = -1e30
LOG2_E = math.log2(math.e)

W_QA = N_Q_A * HEAD_DIM_A
W_KA = N_KV_A * HEAD_DIM_A
W_A = W_QA + 2 * W_KA
W_QB = N_HEADS_B * 2 * HEAD_DIM_B

LANES = 128
MXU_COLS = 256
BF16_ROWS = 16
VMEM_LIMIT = 56 * 1024 * 1024
ATTN_B_TK = 512

F32 = jnp.float32
BF16 = jnp.bfloat16


def _params(*sem):
    return pltpu.CompilerParams(dimension_semantics=sem, vmem_limit_bytes=VMEM_LIMIT)


def _dot(a, b):
    return jnp.dot(a, b, preferred_element_type=F32)


def _dot_nt(a, b):
    return lax.dot_general(a, b, (((1,), (1,)), ((), ())), preferred_element_type=F32)


def _layer_norm(z, g, b):
    mu = jnp.mean(z, axis=-1, keepdims=True)
    zc = z - mu
    var = jnp.mean(zc * zc, axis=-1, keepdims=True)
    return zc * lax.rsqrt(var + LN_EPS) * g + b


def _call(kernel_fn, inputs, in_specs, out_specs, out_shape, *, grid, sem, name, cast=(), aliases=None):
    n_in, n_out, n_cast = len(inputs), len(out_specs), len(cast)
    n_steps = math.prod(grid)

    def step(*g):
        s = 0
        for g_d, n_d in zip(g, grid):
            s = s * n_d + g_d
        return s

    def slabs(offset):
        specs = []
        for w, first_row, n_rows in cast:
            slab = n_rows // n_steps
            assert n_rows % (n_steps * BF16_ROWS) == 0 and first_row % slab == 0
            first = first_row // slab if offset else 0
            specs.append(pl.BlockSpec((slab, w.shape[1]), lambda *g, first=first: (first + step(*g), 0)))
        return specs

    def body(*refs):
        ins, src = refs[:n_in], refs[n_in:n_in + n_cast]
        outs = refs[n_in + n_cast:n_in + n_cast + n_out]
        dst = refs[n_in + n_cast + n_out:]
        for s_ref, d_ref in zip(src, dst):
            d_ref[...] = s_ref[...].astype(d_ref.dtype)
        kernel_fn(*ins, *outs)

    return pl.pallas_call(
        body,
        grid=grid,
        in_specs=[*in_specs, *slabs(True)],
        out_specs=[*out_specs, *slabs(False)],
        out_shape=[*out_shape, *[jax.ShapeDtypeStruct((n_rows, w.shape[1]), BF16) for w, _, n_rows in cast]],
        input_output_aliases=aliases or {},
        compiler_params=_params(*sem),
        name=name,
    )(*inputs, *[w for w, _, _ in cast])


def _rope_tables(seq):
    def cs(dim):
        inv = 1.0 / (ROPE_THETA ** (jnp.arange(0, dim, 2, dtype=F32) / dim))
        ang = jnp.arange(seq, dtype=F32)[:, None] * inv[None, :]
        return jnp.cos(ang), jnp.sin(ang)

    cos_a, sin_a = cs(HEAD_DIM_A)
    cos_b, sin_b = cs(HEAD_DIM_B)
    zeros_a = jnp.zeros_like(sin_a)
    heads = LANES // HEAD_DIM_A
    tab_a = jnp.stack([
        jnp.tile(cos_a, (1, 2 * heads)),
        jnp.tile(jnp.concatenate([zeros_a, sin_a], axis=1), (1, heads)),
        jnp.tile(jnp.concatenate([-sin_a, zeros_a], axis=1), (1, heads)),
    ])
    assert HEAD_DIM_B == LANES
    tab_b = jnp.stack([
        jnp.concatenate([cos_b, cos_b], axis=1),
        jnp.concatenate([-sin_b, sin_b], axis=1),
    ])
    return tab_a, tab_b


def _proj_a_kernel(x_ref, w_ref, tab_ref, o_ref, *xb_refs):
    x = x_ref[...].astype(BF16)
    if xb_refs:
        xb_refs[0][...] = x
    acc = _dot(x, w_ref[...])
    cos, hi, lo = tab_ref[0], tab_ref[1], tab_ref[2]
    scale = LOG2_E * HEAD_DIM_A ** -0.5
    n_rope = (W_QA + W_KA) // LANES
    for g in range(W_A // LANES):
        xg = acc[:, g * LANES:(g + 1) * LANES]
        if g < n_rope:
            half = HEAD_DIM_A // 2
            xg = xg * cos + pltpu.roll(xg, half, axis=1) * hi + pltpu.roll(xg, LANES - half, axis=1) * lo
            if g < W_QA // LANES:
                xg = xg * scale
        o_ref[:, g * LANES:(g + 1) * LANES] = xg.astype(o_ref.dtype)


def _proj_a(x, w_in_b, layer, tab_a, seq, tm=512, cast=()):
    t = x.shape[0]
    out_specs = [pl.BlockSpec((tm, W_A), lambda i: (i, 0))]
    out_shape = [jax.ShapeDtypeStruct((t, W_A), BF16)]
    if x.dtype != BF16:
        out_specs.append(pl.BlockSpec((tm, D_MODEL), lambda i: (i, 0)))
        out_shape.append(jax.ShapeDtypeStruct((t, D_MODEL), BF16))
    return _call(
        _proj_a_kernel,
        [x, w_in_b, tab_a],
        [
            pl.BlockSpec((tm, D_MODEL), lambda i: (i, 0)),
            pl.BlockSpec((D_MODEL, W_A), lambda i: (layer, 0)),
            pl.BlockSpec((3, tm, LANES), lambda i: (0, i % (seq // tm), 0)),
        ],
        out_specs,
        out_shape,
        grid=(t // tm,), sem=("parallel",), name="proj_a", cast=cast)


def _proj_rope_b_kernel(x_ref, w_ref, tab_ref, qt_ref, k_ref):
    scale = LOG2_E * HEAD_DIM_B ** -0.5
    x = x_ref[...]
    w = 2 * HEAD_DIM_B
    assert w == MXU_COLS and HEAD_DIM_B == LANES
    for c in range(2 * N_HEADS_B):
        acc = _dot(x, w_ref[:, c * w:(c + 1) * w])
        for g in range(2):
            xg = acc[:, g * LANES:(g + 1) * LANES]
            xg = xg * tab_ref[0] + pltpu.roll(xg, HEAD_DIM_B // 2, axis=1) * tab_ref[1]
            if c < N_HEADS_B:
                qt_ref[c, g] = (xg * scale).astype(qt_ref.dtype).T
            else:
                lo = (c - N_HEADS_B) * w + g * LANES
                k_ref[:, lo:lo + LANES] = xg.astype(k_ref.dtype)


def _w_in_cols(layer, col0, width):
    return pl.BlockSpec((pl.Element(D_MODEL), pl.Element(width)),
                        lambda i, j: (layer * D_MODEL, pl.multiple_of(col0 + j * width, LANES)))


def _proj_qk_b(xb, w_in_b, layer, tab_b, batch, seq, tm=1024, cast=()):
    t = xb.shape[0]
    tiles_per_seq = seq // tm
    w_spec = pl.BlockSpec((pl.Element(D_MODEL), pl.Element(2 * W_QB)), lambda i: (layer * D_MODEL, W_A),
                          pipeline_mode=pl.Buffered(1))
    return _call(
        _proj_rope_b_kernel,
        [xb, w_in_b, tab_b],
        [
            pl.BlockSpec((tm, D_MODEL), lambda i: (i, 0)),
            w_spec,
            pl.BlockSpec((2, tm, LANES), lambda i: (0, i % tiles_per_seq, 0)),
        ],
        [
            pl.BlockSpec((None, N_HEADS_B, 2, HEAD_DIM_B, tm),
                         lambda i: (i // tiles_per_seq, 0, 0, 0, i % tiles_per_seq)),
            pl.BlockSpec((tm, W_QB), lambda i: (i, 0)),
        ],
        [
            jax.ShapeDtypeStruct((batch, N_HEADS_B, 2, HEAD_DIM_B, seq), BF16),
            jax.ShapeDtypeStruct((t, W_QB), BF16),
        ],
        grid=(t // tm,), sem=("parallel",), name="proj_qk_b", cast=cast)


def _attn_a_block(sinks_ref, q_ref, kvp_ref, kvc_ref, o_ref, has_prev):
    blk = WINDOW
    hd = HEAD_DIM_A
    key = lax.broadcasted_iota(jnp.int32, (2 * blk, blk), 0)
    qry = lax.broadcasted_iota(jnp.int32, (2 * blk, blk), 1)
    mask = (key <= blk + qry) & (key > qry)
    if has_prev is not True:
        mask = mask & ((key >= blk) | has_prev)
    vpt = kvp_ref[:, W_KA:].T
    vct = kvc_ref[:, W_KA:].T

    for g in range(N_KV_A):
        ks = slice(g * hd, (g + 1) * hd)
        kband = jnp.concatenate([kvp_ref[:, ks], kvc_ref[:, ks]], axis=0)
        vt = jnp.concatenate([vpt[ks], vct[ks]], axis=1)
        qts = []
        for pr in range(GROUP_A // 2):
            h0 = g * GROUP_A + 2 * pr
            t = q_ref[:, h0 * hd:(h0 + 2) * hd].T
            qts += [t[:hd], t[hd:]]
        qt = jnp.concatenate(qts, axis=1)
        s = _dot(kband, qt)
        ps, rs = [], []
        for hh in range(GROUP_A):
            sink = sinks_ref[g * GROUP_A + hh] * LOG2_E
            sh = jnp.where(mask, s[:, hh * blk:(hh + 1) * blk], NEG_INF)
            m = jnp.maximum(jnp.max(sh, axis=0, keepdims=True), sink)
            p = jnp.exp2(sh - m)
            denom = jnp.sum(p, axis=0, keepdims=True) + jnp.exp2(sink - m)
            ps.append(p.astype(BF16))
            rs.append(1.0 / denom)
        ot = _dot(vt, jnp.concatenate(ps, axis=1))
        for pr in range(GROUP_A // 2):
            h0 = g * GROUP_A + 2 * pr
            pair = jnp.concatenate([ot[:, (2 * pr + e) * blk:(2 * pr + e + 1) * blk] * rs[2 * pr + e]
                                    for e in range(2)], axis=0)
            o_ref[:, h0 * hd:(h0 + 2) * hd] = pair.astype(o_ref.dtype).T


def _attn_a_blocks(sinks_ref, q_ref, kvp_ref, kvc_ref, o_ref, first, blocks_per_seq, c):
    blk = WINDOW
    rows = pl.ds(c * blk, blk)
    prev = kvp_ref if c == 0 else kvc_ref.at[pl.ds((c - 1) * blk, blk)]
    has_prev = (first % blocks_per_seq != 0) if c == 0 else True
    _attn_a_block(sinks_ref, q_ref.at[rows], prev, kvc_ref.at[rows], o_ref.at[rows], has_prev)


def _attn_a_specs(first_block, n_blocks, step):
    blk = WINDOW
    kv_col = W_QA // (2 * W_KA)
    base = first_block // n_blocks
    assert first_block % n_blocks == 0
    tile = lambda i, j: base + step(i, j)
    return [
        pl.BlockSpec((n_blocks * blk, W_QA), lambda i, j: (tile(i, j), 0)),
        pl.BlockSpec((blk, 2 * W_KA), lambda i, j: (jnp.maximum(tile(i, j) * n_blocks - 1, 0), kv_col)),
        pl.BlockSpec((n_blocks * blk, 2 * W_KA), lambda i, j: (tile(i, j), kv_col)),
    ], pl.BlockSpec((n_blocks * blk, W_QA), lambda i, j: (tile(i, j), 0))


def _vt_attn_a_kernel(sinks_ref, x_ref, w_ref, q_ref, kvp_ref, kvc_ref, oa_hbm_ref, vt_ref, o_ref,
                      *, tk, blocks_per_seq):
    del oa_hbm_ref
    n_blocks = o_ref.shape[0] // WINDOW
    first = (pl.program_id(0) * pl.num_programs(1) + pl.program_id(1)) * n_blocks
    x = x_ref[...]
    w = 2 * HEAD_DIM_B
    for c in range(max(vt_ref.shape[0], n_blocks)):
        if c < vt_ref.shape[0]:
            acc = _dot(x, w_ref[:, c * w:(c + 1) * w]).astype(vt_ref.dtype)
            for kk in range(vt_ref.shape[1]):
                vt_ref[c, kk] = acc[kk * tk:(kk + 1) * tk].T
        if c < n_blocks:
            _attn_a_blocks(sinks_ref, q_ref, kvp_ref, kvc_ref, o_ref, first, blocks_per_seq, c)


def _vt_attn_a(xb, w_in_b, layer, qkv_a, sinks, o_a, batch, seq, tk, n_blocks, tm=1024, heads=4, cast=()):
    t = xb.shape[0]
    w = 2 * HEAD_DIM_B
    tiles_per_seq = seq // tm
    nj = N_HEADS_B // heads
    assert (seq // WINDOW) % n_blocks == 0
    attn_in, attn_out = _attn_a_specs(0, n_blocks, lambda i, j: i * nj + j)
    return _call(
        functools.partial(_vt_attn_a_kernel, tk=tk, blocks_per_seq=seq // WINDOW),
        [sinks, xb, w_in_b, qkv_a, qkv_a, qkv_a, o_a],
        [
            pl.BlockSpec(memory_space=pltpu.SMEM),
            pl.BlockSpec((tm, D_MODEL), lambda i, j: (i, 0)),
            _w_in_cols(layer, W_A + 2 * W_QB, heads * w),
            *attn_in,
            pl.BlockSpec(memory_space=pl.ANY),
        ],
        [
            pl.BlockSpec((None, heads, tm // tk, w, tk),
                         lambda i, j: (i // tiles_per_seq, j, i % tiles_per_seq, 0, 0)),
            attn_out,
        ],
        [
            jax.ShapeDtypeStruct((batch, N_HEADS_B, seq // tk, w, tk), BF16),
            jax.ShapeDtypeStruct(o_a.shape, o_a.dtype),
        ],
        grid=(t // tm, nj), sem=("parallel", "arbitrary"), name="vt_attn_a", cast=cast, aliases={6: 1})


def _gates_attn_a_kernel(sinks_ref, x_ref, w_ref, q_ref, kvp_ref, kvc_ref, oa_hbm_ref, g_ref, o_ref,
                         *, tn, first_block, blocks_per_seq):
    del oa_hbm_ref
    n_blocks = o_ref.shape[0] // WINDOW
    first = first_block + (pl.program_id(0) * pl.num_programs(1) + pl.program_id(1)) * n_blocks
    x = x_ref[...]
    n_chunks = tn // MXU_COLS
    for c in range(n_chunks):
        cs = slice(c * MXU_COLS, (c + 1) * MXU_COLS)
        g_ref[:, cs] = jax.nn.sigmoid(_dot(x, w_ref[:, cs])).astype(g_ref.dtype)
        if c % (n_chunks // n_blocks) == 0:
            _attn_a_blocks(sinks_ref, q_ref, kvp_ref, kvc_ref, o_ref, first, blocks_per_seq,
                           c // (n_chunks // n_blocks))


def _gates_attn_a(xb, w_in_b, layer, qkv_a, sinks, o_a, first_block, seq, n_blocks, tm=1024, tn=2048, cast=()):
    t = xb.shape[0]
    n = 2 * D_MODEL
    nj = n // tn
    assert (t // tm) * nj * n_blocks + first_block == t // WINDOW and (seq // WINDOW) % n_blocks == 0
    attn_in, attn_out = _attn_a_specs(first_block, n_blocks, lambda i, j: i * nj + j)
    return _call(
        functools.partial(_gates_attn_a_kernel, tn=tn, first_block=first_block, blocks_per_seq=seq // WINDOW),
        [sinks, xb, w_in_b, qkv_a, qkv_a, qkv_a, o_a],
        [
            pl.BlockSpec(memory_space=pltpu.SMEM),
            pl.BlockSpec((tm, D_MODEL), lambda i, j: (i, 0)),
            _w_in_cols(layer, W_A + 3 * W_QB, tn),
            *attn_in,
            pl.BlockSpec(memory_space=pl.ANY),
        ],
        [pl.BlockSpec((tm, tn), lambda i, j: (i, j)), attn_out],
        [jax.ShapeDtypeStruct((t, n), BF16), jax.ShapeDtypeStruct(o_a.shape, o_a.dtype)],
        grid=(t // tm, nj), sem=("parallel", "arbitrary"), name="gates_attn_a", cast=cast, aliases={6: 1})


def _attn_b_kernel(lq1_ref, lk1_ref, lq2_ref, lk2_ref, g_ref, qt_ref, k_ref, vt_ref, o_ref,
                   m_ref, l_ref, acc_ref, *, tq, tk, n_q_tiles, lam_init):
    qi = pl.program_id(2)
    hd = HEAD_DIM_B
    cg = MXU_COLS
    n_groups = tq // cg

    def chain(c, j, k, vt, first_key, start):
        js = slice(j * cg, (j + 1) * cg)
        s = _dot(k[:, c * hd:(c + 1) * hd], qt_ref[c, :, js])
        if first_key is not None:
            key = lax.broadcasted_iota(jnp.int32, s.shape, 0)
            qry = lax.broadcasted_iota(jnp.int32, s.shape, 1)
            s = jnp.where(key + first_key <= qry, s, NEG_INF)
        m_new = jnp.max(s, axis=0, keepdims=True)
        if not start:
            m_old = m_ref[c, :, js]
            m_new = jnp.maximum(m_old, m_new)
            a = jnp.exp2(m_old - m_new)
        p = jnp.exp2(s - m_new)
        l_new = jnp.sum(p, axis=0, keepdims=True)
        acc_new = _dot(vt, p.astype(BF16))
        if not start:
            l_new = a * l_ref[c, :, js] + l_new
            acc_new = a * acc_ref[c, :, js] + acc_new
        l_ref[c, :, js] = l_new
        acc_ref[c, :, js] = acc_new
        m_ref[c, :, js] = m_new

    blocks_per_tile = tq // tk

    def finish(j):
        js = slice(j * cg, (j + 1) * cg)
        o = acc_ref[0, :, js] * (1.0 / l_ref[0, :, js]) - lam * (acc_ref[1, :, js] * (1.0 / l_ref[1, :, js]))
        o = o * lax.rsqrt(jnp.mean(o * o, axis=0, keepdims=True) + SUBLN_EPS)
        o = o * g_ref[...] * (1.0 - lam_init)
        o_ref[js, :] = o.T.astype(o_ref.dtype)

    def tile(q_tile):
        started = set()

        def run(c, j, k, vt, first_key):
            chain(c, j, k, vt, first_key, (c, j) not in started)
            started.add((c, j))

        for kb in range(q_tile * blocks_per_tile):
            k = k_ref[kb * tk:(kb + 1) * tk, :]
            vt = vt_ref[kb]
            for j in range(n_groups):
                for c in range(2):
                    run(c, j, k, vt, None)
        for u in range(blocks_per_tile):
            kb = q_tile * blocks_per_tile + u
            k = k_ref[kb * tk:(kb + 1) * tk, :]
            vt = vt_ref[kb]
            for j in range(n_groups):
                first_key = u * tk - j * cg
                n_vis = min(tk, (j + 1) * cg - u * tk)
                if n_vis <= 0:
                    continue
                masked = first_key + n_vis - 1 > 0
                for c in range(2):
                    run(c, j, k[:n_vis], vt[:, :n_vis], first_key if masked else None)
        for j in range(n_groups):
            finish(j)

    lam = (jnp.exp(jnp.sum(lq1_ref[...] * lk1_ref[...], axis=-1, keepdims=True))
           - jnp.exp(jnp.sum(lq2_ref[...] * lk2_ref[...], axis=-1, keepdims=True))
           + lam_init)
    for q_tile in range(n_q_tiles):
        pl.when(qi == q_tile)(functools.partial(tile, q_tile))


def _attn_b(qt_b, k_b, vt, lq1, lk1, lq2, lk2, subln_g, lam_init, batch, seq, tq=1024, tk=512):
    nq = seq // tq
    w = 2 * HEAD_DIM_B
    vec = lambda n: pl.BlockSpec((1, n), lambda b, h, i: (0, 0))
    return pl.pallas_call(
        functools.partial(_attn_b_kernel, tq=tq, tk=tk, n_q_tiles=nq, lam_init=lam_init),
        grid=(batch, N_HEADS_B, nq),
        in_specs=[
            vec(HEAD_DIM_B), vec(HEAD_DIM_B), vec(HEAD_DIM_B), vec(HEAD_DIM_B),
            pl.BlockSpec((w, 1), lambda b, h, i: (0, 0)),
            pl.BlockSpec((None, None, 2, HEAD_DIM_B, tq), lambda b, h, i: (b, h, 0, 0, i)),
            pl.BlockSpec((seq, w), lambda b, h, i: (b, h)),
            pl.BlockSpec((None, None, seq // tk, w, tk), lambda b, h, i: (b, h, 0, 0, 0)),
        ],
        out_specs=pl.BlockSpec((tq, w), lambda b, h, i: (b * nq + i, h)),
        out_shape=jax.ShapeDtypeStruct((batch * seq, W_QB), BF16),
        scratch_shapes=[
            pltpu.VMEM((2, 1, tq), F32),
            pltpu.VMEM((2, 1, tq), F32),
            pltpu.VMEM((2, w, tq), F32),
        ],
        compiler_params=_params("parallel", "parallel", "arbitrary"),
        name="attn_b",
    )(lq1, lk1, lq2, lk2, subln_g, qt_b, k_b, vt)


def _merge_kernel(oa_ref, ob_ref, wa_ref, wb_ref, ga_ref, gb_ref, o_ref):
    pa = _dot(oa_ref[...], wa_ref[...])
    pb = _dot(ob_ref[...], wb_ref[...])
    o_ref[...] = (ga_ref[...].astype(F32) * pa + gb_ref[...].astype(F32) * pb).astype(o_ref.dtype)


def _merge(o_a, o_b, w_pa, w_pb, layer, gates, tm=1024, tn=1024, cast=()):
    t = o_a.shape[0]
    return _call(
        _merge_kernel,
        [o_a, o_b, w_pa, w_pb, gates, gates],
        [
            pl.BlockSpec((tm, W_QA), lambda i, j: (i, 0)),
            pl.BlockSpec((tm, W_QB), lambda i, j: (i, 0)),
            pl.BlockSpec((W_QA, tn), lambda i, j: (layer, j)),
            pl.BlockSpec((W_QB, tn), lambda i, j: (layer, j)),
            pl.BlockSpec((tm, tn), lambda i, j: (i, j)),
            pl.BlockSpec((tm, tn), lambda i, j: (i, D_MODEL // tn + j)),
        ],
        [pl.BlockSpec((tm, tn), lambda i, j: (i, j))],
        [jax.ShapeDtypeStruct((t, D_MODEL), BF16)],
        grid=(t // tm, D_MODEL // tn), sem=("parallel", "arbitrary"), name="merge", cast=cast)


def _out_ln_kernel(m_ref, w_ref, x_ref, g_ref, b_ref, y_ref, yb_ref, *, rows):
    g, b = g_ref[...], b_ref[...]
    for r in range(m_ref.shape[0] // rows):
        rs = slice(r * rows, (r + 1) * rows)
        z = ALPHA * x_ref[rs, :] + _dot(m_ref[rs, :], w_ref[...])
        y = _layer_norm(z, g, b)
        y_ref[rs, :] = y
        yb_ref[rs, :] = y.astype(yb_ref.dtype)


def _out_ln(merged, w_out, layer, x, g, b, tm=512, rows=256, cast=()):
    t = x.shape[0]
    row = lambda: pl.BlockSpec((tm, D_MODEL), lambda i: (i, 0))
    vec = lambda: pl.BlockSpec((1, D_MODEL), lambda i: (0, 0))
    return _call(
        functools.partial(_out_ln_kernel, rows=rows),
        [merged, w_out, x, g, b],
        [row(), pl.BlockSpec((D_MODEL, D_MODEL), lambda i: (layer, 0), pipeline_mode=pl.Buffered(1)),
         row(), vec(), vec()],
        [row(), row()],
        [jax.ShapeDtypeStruct((t, D_MODEL), F32), jax.ShapeDtypeStruct((t, D_MODEL), BF16)],
        grid=(t // tm,), sem=("parallel",), name="out_ln", cast=cast)


def _ffn_up_kernel(xp_ref, x_ref, wg_ref, wv_ref, cwg_ref, cwv_ref, cbg_ref, cbv_ref, o_ref,
                   *, tm, tn, tiles_per_seq):
    i = pl.program_id(1)
    halo = BF16_ROWS
    xp = jnp.where(i % tiles_per_seq == 0, jnp.zeros((), BF16), xp_ref[...])
    xe = jnp.concatenate([xp, x_ref[...]], axis=0)

    def conv(w_ref, cw_ref, cb_ref, cs):
        u = _dot(xe, w_ref[:, cs])
        out = cb_ref[:, cs] + cw_ref[CONV_WIDTH - 1:CONV_WIDTH, cs] * u[halo:]
        for back in range(1, CONV_WIDTH):
            tap = CONV_WIDTH - 1 - back
            out = out + cw_ref[tap:tap + 1, cs] * pltpu.roll(u, back, axis=0)[halo:]
        return out

    for c in range(tn // MXU_COLS):
        cs = slice(c * MXU_COLS, (c + 1) * MXU_COLS)
        gate = conv(wg_ref, cwg_ref, cbg_ref, cs)
        val = conv(wv_ref, cwv_ref, cbv_ref, cs)
        o_ref[:, cs] = (jax.nn.silu(gate) * val).astype(o_ref.dtype)


def _ffn_up(xb, w_up, layer, conv_w, conv_b, conv_layer, seq, tm=1024, tn=D_FF // 2):
    t = xb.shape[0]
    nf = D_FF // tn
    halo = BF16_ROWS
    resident = pl.Buffered(1)
    return pl.pallas_call(
        functools.partial(_ffn_up_kernel, tm=tm, tn=tn, tiles_per_seq=seq // tm),
        grid=(nf, t // tm),
        in_specs=[
            pl.BlockSpec((halo, D_MODEL), lambda j, i: (jnp.maximum(i * (tm // halo) - 1, 0), 0)),
            pl.BlockSpec((tm, D_MODEL), lambda j, i: (i, 0)),
            pl.BlockSpec((D_MODEL, tn), lambda j, i: (layer, j), pipeline_mode=resident),
            pl.BlockSpec((D_MODEL, tn), lambda j, i: (layer, nf + j), pipeline_mode=resident),
            pl.BlockSpec((None, CONV_WIDTH, tn), lambda j, i: (conv_layer, 0, j)),
            pl.BlockSpec((None, CONV_WIDTH, tn), lambda j, i: (conv_layer, 0, nf + j)),
            pl.BlockSpec((None, 1, tn), lambda j, i: (conv_layer, 0, j)),
            pl.BlockSpec((None, 1, tn), lambda j, i: (conv_layer, 0, nf + j)),
        ],
        out_specs=pl.BlockSpec((tm, tn), lambda j, i: (i, j)),
        out_shape=jax.ShapeDtypeStruct((t, D_FF), BF16),
        compiler_params=_params("arbitrary", "parallel"),
        name="ffn_up",
    )(xb, xb, w_up, w_up, conv_w, conv_w, conv_b, conv_b)


def _down_ln_kernel(a_ref, w_ref, x_ref, g_ref, b_ref, *out_refs):
    z = ALPHA * x_ref[...] + _dot(a_ref[...], w_ref[...])
    y = _layer_norm(z, g_ref[...], b_ref[...])
    out_refs[0][...] = y
    if len(out_refs) > 1:
        out_refs[1][...] = y.astype(BF16)


def _down_ln(act, w_down, layer, x, g, b, with_bf16, tm=256, cast=()):
    t = x.shape[0]
    row = lambda: pl.BlockSpec((tm, D_MODEL), lambda i: (i, 0))
    vec = lambda: pl.BlockSpec((1, D_MODEL), lambda i: (0, 0))
    out_shape = [jax.ShapeDtypeStruct((t, D_MODEL), F32)]
    if with_bf16:
        out_shape.append(jax.ShapeDtypeStruct((t, D_MODEL), BF16))
    return _call(
        _down_ln_kernel,
        [act, w_down, x, g, b],
        [
            pl.BlockSpec((tm, D_FF), lambda i: (i, 0)),
            pl.BlockSpec((D_FF, D_MODEL), lambda i: (layer, 0), pipeline_mode=pl.Buffered(1)),
            row(), vec(), vec(),
        ],
        [row() for _ in out_shape],
        out_shape,
        grid=(t // tm,), sem=("parallel",), name="down_ln", cast=cast)


def kernel(x, w_in, sinks, lambda_q1, lambda_k1, lambda_q2, lambda_k2, subln_g, w_proj_a, w_proj_b,
           w_out, ln1_g, ln1_b, w_up, conv_w, conv_b, w_down, ln2_g, ln2_b):
    batch, seq, _ = x.shape
    t = batch * seq
    tab_a, tab_b = _rope_tables(seq)
    x = x.reshape(t, D_MODEL)
    xb = None
    row = lambda v: v.reshape(1, -1).astype(F32)
    rows = lambda w: w.reshape(-1, w.shape[-1])
    whole = lambda w: (rows(w), 0, rows(w).shape[0])
    layer0 = lambda w: (rows(w), 0, w.shape[1])
    later = lambda w: (rows(w), w.shape[1], (DEPTH - 1) * w.shape[1])
    conv_w = conv_w.astype(F32)
    conv_b = conv_b.astype(F32).reshape(DEPTH, 1, 2 * D_FF)
    w_in_b, w_in_row = None, 0
    w_up_b, w_up_row = None, 0
    for l in range(DEPTH):
        first = l == 0
        if first:
            qkv_a, xb, w_in_b, w_up_b = _proj_a(x, w_in[0, :, :W_A].astype(BF16), 0, tab_a, seq,
                                                cast=[layer0(w_in), layer0(w_up)])
        else:
            qkv_a, = _proj_a(xb, w_in_b, w_in_row, tab_a, seq)
        qt_b, k_b = _proj_qk_b(xb, w_in_b, w_in_row, tab_b, batch, seq)
        half = t // WINDOW // 2
        per_step = 2
        if first:
            o_a = jnp.zeros((t, W_QA), BF16)
        vt_b, o_a, *done = _vt_attn_a(xb, w_in_b, w_in_row, qkv_a, sinks[l].astype(F32), o_a, batch, seq, ATTN_B_TK,
                                      per_step,
                                      cast=[whole(w_proj_a), whole(w_proj_b), whole(w_out)] if first else [])
        if first:
            w_pa, w_pb, w_o = done
        gates, o_a, *done = _gates_attn_a(xb, w_in_b, w_in_row, qkv_a, sinks[l].astype(F32), o_a, half, seq,
                                          per_step, cast=[later(w_up)] if first and DEPTH > 1 else [])
        if done:
            w_up_rest, = done
        lam_init = 0.8 - 0.6 * math.exp(-0.3 * l)
        o_b = _attn_b(qt_b, k_b, vt_b, row(lambda_q1[l]), row(lambda_k1[l]), row(lambda_q2[l]), row(lambda_k2[l]),
                      subln_g[l].reshape(-1, 1).astype(F32), lam_init, batch, seq, tk=ATTN_B_TK)
        merged, = _merge(o_a, o_b, w_pa, w_pb, l, gates)
        x, xb, *done = _out_ln(merged, w_o, l, x, row(ln1_g[l]), row(ln1_b[l]),
                               cast=[whole(w_down)] if first else [])
        if first:
            w_d, = done
        act = _ffn_up(xb, w_up_b, w_up_row, conv_w, conv_b, l, seq)
        last = l == DEPTH - 1
        x, *rest = _down_ln(act, w_d, l, x, row(ln2_g[l]), row(ln2_b[l]), with_bf16=not last,
                            cast=[later(w_in)] if first and DEPTH > 1 else [])
        if not last:
            xb = rest[0]
        if first and DEPTH > 1:
            w_in_b, w_up_b = rest[-1], w_up_rest
        if not last:
            w_in_row, w_up_row = l, l
    return x.reshape(batch, seq, D_MODEL)
```
